```python
import jax, jax.numpy as jnp
from jax import lax
import numpy as np

D_MODEL = 1024
BATCH = 16
SEQ = 2048
DEPTH = 1
DEC_BATCH = 128
DEC_SEQ = 8
PAST_LEN = 16384
PAGE_SIZE = 128

HEAD_DIM = 64
RWKV_WIDTH = D_MODEL // 2
RWKV_HEADS = RWKV_WIDTH // HEAD_DIM
ATTN_WIDTH = D_MODEL - RWKV_WIDTH
ATTN_Q_HEADS = ATTN_WIDTH // HEAD_DIM
ATTN_KV_HEADS = 2
ATTN_GROUP = ATTN_Q_HEADS // ATTN_KV_HEADS
KV_WIDTH = ATTN_KV_HEADS * HEAD_DIM
DECAY_LORA = 64
ICLR_LORA = 64
WINDOW = 128
BLOCK = WINDOW
ROPE_THETA = 500000.0
ROPE_DIM = HEAD_DIM // 4
PLE_DIM = 256
NORM_EPS = 1e-6
GN_EPS = 64e-5
NEG_INF = -1e30
SHIFT_DIM = 3 * RWKV_WIDTH + DECAY_LORA + ICLR_LORA
IN_DIM = SHIFT_DIM + RWKV_WIDTH + ATTN_WIDTH + 2 * KV_WIDTH + ATTN_WIDTH

kernel_name = 'hymba_rwkv7_swa_sink_step'


def rmsnorm(x, g):
    xf = x.astype(jnp.float32)
    y = xf * lax.rsqrt(jnp.mean(xf * xf, axis=-1, keepdims=True) + NORM_EPS)
    return (y * g.astype(jnp.float32)).astype(x.dtype)


def apply_rope(x, pos):
    half = ROPE_DIM // 2
    inv = ROPE_THETA ** (-jnp.arange(half, dtype=jnp.float32) / half)
    ang = pos.astype(jnp.float32)[:, None] * inv[None, :]
    cos = jnp.cos(ang)[:, None, :]
    sin = jnp.sin(ang)[:, None, :]
    xf = x[..., :ROPE_DIM].astype(jnp.float32)
    x1, x2 = xf[..., :half], xf[..., half:]
    rot = jnp.concatenate([x1 * cos - x2 * sin, x2 * cos + x1 * sin], axis=-1).astype(x.dtype)
    return jnp.concatenate([rot, x[..., ROPE_DIM:]], axis=-1)


def sink_softmax(s, mask, sink):
    s = jnp.where(mask, s, NEG_INF)
    sink = jnp.broadcast_to(sink.astype(jnp.float32), s.shape[:-1] + (1,))
    return jax.nn.softmax(jnp.concatenate([s, sink], axis=-1), axis=-1)[..., :-1]


def swa_prompt(q, k, v, sinks, pos):
    B, T = q.shape[:2]
    nb = T // BLOCK
    qb = q.reshape(B, nb, BLOCK, ATTN_KV_HEADS, ATTN_GROUP, HEAD_DIM)
    pad = ((0, 0), (BLOCK, 0), (0, 0), (0, 0))
    def band(t):
        tp = jnp.pad(t, pad)[:, :T].reshape(B, nb, BLOCK, ATTN_KV_HEADS, HEAD_DIM)
        return jnp.concatenate([tp, t.reshape(B, nb, BLOCK, ATTN_KV_HEADS, HEAD_DIM)], axis=2)
    kb, vb = band(k), band(v)
    qpos = pos.reshape(nb, BLOCK)
    kpos = jnp.concatenate([qpos - BLOCK, qpos], axis=1)
    diff = qpos[:, :, None] - kpos[:, None, :]
    mask = (diff >= 0) & (diff < WINDOW) & (kpos[:, None, :] >= 0)
    s = jnp.einsum('bnqhgd,bnkhd->bnhgqk', qb, kb).astype(jnp.float32) * (HEAD_DIM ** -0.5)
    p = sink_softmax(s, mask[None, :, None, None], sinks.reshape(ATTN_KV_HEADS, ATTN_GROUP)[:, :, None, None])
    o = jnp.einsum('bnhgqk,bnkhd->bnqhgd', p.astype(v.dtype), vb)
    return o.reshape(B, T, ATTN_WIDTH)


def swa_sample(q, k, v, kbuf, vbuf, sinks, pos):
    B, T = q.shape[:2]
    W = kbuf.shape[1]
    k_all = jnp.concatenate([kbuf.astype(k.dtype), k], axis=1)
    v_all = jnp.concatenate([vbuf.astype(v.dtype), v], axis=1)
    kpos = jnp.concatenate([PAST_LEN - W + jnp.arange(W), pos])
    diff = pos[:, None] - kpos[None, :]
    mask = (diff >= 0) & (diff < WINDOW)
    qg = q.reshape(B, T, ATTN_KV_HEADS, ATTN_GROUP, HEAD_DIM)
    s = jnp.einsum('bqhgd,bkhd->bhgqk', qg, k_all).astype(jnp.float32) * (HEAD_DIM ** -0.5)
    p = sink_softmax(s, mask, sinks.reshape(ATTN_KV_HEADS, ATTN_GROUP)[:, :, None, None])
    o = jnp.einsum('bhgqk,bkhd->bqhgd', p.astype(v.dtype), v_all).reshape(B, T, ATTN_WIDTH)
    return o, k_all[:, -W:], v_all[:, -W:]


def rwkv_time_mix(z_shift, shift0, S0, mu, w0, w2, a0, a2, k_k, k_a, r_k, ln_w, ln_b):
    B, T, _ = z_shift.shape
    H, D, R = RWKV_HEADS, HEAD_DIM, RWKV_WIDTH
    f32 = jnp.float32
    prev = jnp.concatenate([shift0[:, None].astype(z_shift.dtype), z_shift[:, :-1]], axis=1)
    zs = z_shift + mu * (prev - z_shift)
    r, kx, v = zs[..., :R], zs[..., R:2 * R], zs[..., 2 * R:3 * R]
    wd, ad = zs[..., 3 * R:3 * R + DECAY_LORA], zs[..., 3 * R + DECAY_LORA:]
    w_log = -jax.nn.softplus(-(w0 + jnp.tanh(wd) @ w2).astype(f32)) - 0.5
    decay = jnp.exp(-jnp.exp(w_log))
    a = jax.nn.sigmoid((a0 + ad @ a2).astype(f32))
    heads = lambda t: t.astype(f32).reshape(B, T, H, D)
    r, kx, v, decay, a = heads(r), heads(kx), heads(v), heads(decay), heads(a)
    kk = kx * k_k.astype(f32).reshape(H, D)
    kk = kk / jnp.maximum(jnp.linalg.norm(kk, axis=-1, keepdims=True), 1e-12)
    k = kx * (1.0 + (a - 1.0) * k_a.astype(f32).reshape(H, D))

    def step(S, inp):
        r_t, w_t, k_t, v_t, kk_t, a_t = inp
        sa = jnp.einsum('bhvk,bhk->bhv', S, kk_t)
        S = (S * w_t[:, :, None, :] - sa[..., None] * (kk_t * a_t)[:, :, None, :]
             + v_t[..., None] * k_t[:, :, None, :])
        return S, jnp.einsum('bhvk,bhk->bhv', S, r_t)

    xs = tuple(jnp.moveaxis(t, 1, 0) for t in (r, decay, k, v, kk, a))
    S_fin, y = lax.scan(step, S0.astype(f32), xs)
    y = jnp.moveaxis(y, 0, 1)
    mean = jnp.mean(y, axis=-1, keepdims=True)
    var = jnp.mean(jnp.square(y - mean), axis=-1, keepdims=True)
    yn = (y - mean) * lax.rsqrt(var + GN_EPS) * ln_w.astype(f32).reshape(H, D) + ln_b.astype(f32).reshape(H, D)
    bonus = jnp.sum(r * k * r_k.astype(f32).reshape(H, D), axis=-1, keepdims=True) * v
    out = (yn + bonus).reshape(B, T, R).astype(z_shift.dtype)
    return out, S_fin, z_shift[:, -1]


def mixer_layer(h, p, pos, S0, shift0, kbuf, vbuf, g_norm, w_in, mu, w0, w2, a0, a2,
                k_k, k_a, r_k, ln_w, ln_b, sinks, w_out, g_ple, w_pg, w_pp):
    B, T, _ = h.shape
    u = rmsnorm(h, g_norm)
    z = u @ w_in
    o1 = SHIFT_DIM
    o2 = o1 + RWKV_WIDTH
    o3 = o2 + ATTN_WIDTH
    o4 = o3 + KV_WIDTH
    o5 = o4 + KV_WIDTH
    z_shift, gate_r, q, k, v, gate_a = (z[..., :o1], z[..., o1:o2], z[..., o2:o3],
                                        z[..., o3:o4], z[..., o4:o5], z[..., o5:])
    o_r, S_new, shift_new = rwkv_time_mix(z_shift, shift0, S0, mu, w0, w2, a0, a2,
                                          k_k, k_a, r_k, ln_w, ln_b)
    q = apply_rope(q.reshape(B, T, ATTN_Q_HEADS, HEAD_DIM), pos)
    k = apply_rope(k.reshape(B, T, ATTN_KV_HEADS, HEAD_DIM), pos)
    v = v.reshape(B, T, ATTN_KV_HEADS, HEAD_DIM)
    if kbuf is None:
        o_a = swa_prompt(q, k, v, sinks, pos)
        n_keep = min(WINDOW, T)
        k_new, v_new = k[:, -n_keep:], v[:, -n_keep:]
    else:
        o_a, k_new, v_new = swa_sample(q, k, v, kbuf, vbuf, sinks, pos)
    mixed = jnp.concatenate([o_r * jax.nn.silu(gate_r), o_a * jax.nn.silu(gate_a)], axis=-1)
    h = h + mixed @ w_out
    gate = jax.nn.sigmoid(rmsnorm(h, g_ple) @ w_pg)
    h = h + gate * (p @ w_pp)
    return h, S_new.astype(h.dtype), shift_new, k_new, v_new


def setup_inputs(seed: int = 0) -> dict:
    key = jax.random.key(seed)
    ks = jax.random.split(key, 32)
    f32 = jnp.float32
    nrm = lambda k, shape, s: jax.random.normal(k, shape, f32) * s
    D, R = D_MODEL, RWKV_WIDTH
    WIN = min(WINDOW, PAST_LEN)
    return {
        'x_prompt': nrm(ks[0], (BATCH, SEQ, D), 1.0),
        'x_sample': nrm(ks[1], (DEC_BATCH, DEC_SEQ, D), 1.0),
        'state_rwkv_wkv': nrm(ks[2], (DEPTH, DEC_BATCH, RWKV_HEADS, HEAD_DIM, HEAD_DIM), 0.3),
        'state_rwkv_shift': nrm(ks[3], (DEPTH, DEC_BATCH, SHIFT_DIM), 1.0),
        'cache_swa_k': nrm(ks[4], (DEPTH, DEC_BATCH, WIN, ATTN_KV_HEADS, HEAD_DIM), 1.0),
        'cache_swa_v': nrm(ks[5], (DEPTH, DEC_BATCH, WIN, ATTN_KV_HEADS, HEAD_DIM), 1.0),
        'p_prompt': nrm(ks[6], (DEPTH, BATCH, SEQ, PLE_DIM), 1.0),
        'p_sample': nrm(ks[7], (DEPTH, DEC_BATCH, DEC_SEQ, PLE_DIM), 1.0),
        'g_norm': 1.0 + nrm(ks[8], (DEPTH, D), 0.05),
        'w_in': nrm(ks[9], (DEPTH, D, IN_DIM), D ** -0.5),
        'mu_shift': jax.random.uniform(ks[10], (DEPTH, SHIFT_DIM), f32, 0.1, 0.9),
        'w0': jax.random.uniform(ks[11], (DEPTH, R), f32, -4.0, 1.0),
        'w2': nrm(ks[12], (DEPTH, DECAY_LORA, R), 0.1),
        'a0': nrm(ks[13], (DEPTH, R), 0.1),
        'a2': nrm(ks[14], (DEPTH, ICLR_LORA, R), 0.1),
        'k_k': 0.85 + nrm(ks[15], (DEPTH, R), 0.05),
        'k_a': 1.0 + nrm(ks[16], (DEPTH, R), 0.05),
        'r_k': nrm(ks[17], (DEPTH, R), 0.1),
        'ln_w': 1.0 + nrm(ks[18], (DEPTH, R), 0.05),
        'ln_b': nrm(ks[19], (DEPTH, R), 0.02),
        'sinks': nrm(ks[20], (DEPTH, ATTN_Q_HEADS), 0.5),
        'w_out': nrm(ks[21], (DEPTH, D, D), D ** -0.5),
        'g_ple': 1.0 + nrm(ks[22], (DEPTH, D), 0.05),
        'w_ple_gate': nrm(ks[23], (DEPTH, D, D), D ** -0.5),
        'w_ple_proj': nrm(ks[24], (DEPTH, PLE_DIM, D), 0.5 * PLE_DIM ** -0.5),
        'g_final': 1.0 + nrm(ks[25], (D,), 0.05),
    }


def reference(x_prompt, x_sample, state_rwkv_wkv, state_rwkv_shift, cache_swa_k, cache_swa_v,
              p_prompt, p_sample, g_norm, w_in, mu_shift, w0, w2, a0, a2, k_k, k_a, r_k,
              ln_w, ln_b, sinks, w_out, g_ple, w_ple_gate, w_ple_proj, g_final):
    pos_p = jnp.arange(x_prompt.shape[1])
    pos_s = PAST_LEN + jnp.arange(x_sample.shape[1])
    hp, hs = x_prompt, x_sample
    Bp = x_prompt.shape[0]
    wkv_p, sh_p, k_p, v_p = [], [], [], []
    wkv_s, sh_s, k_s, v_s = [], [], [], []
    for l in range(DEPTH):
        wl = (g_norm[l], w_in[l], mu_shift[l], w0[l], w2[l], a0[l], a2[l], k_k[l], k_a[l], r_k[l],
              ln_w[l], ln_b[l], sinks[l], w_out[l], g_ple[l], w_ple_gate[l], w_ple_proj[l])
        S0p = jnp.zeros((Bp, RWKV_HEADS, HEAD_DIM, HEAD_DIM), jnp.float32)
        sh0p = jnp.zeros((Bp, SHIFT_DIM), hp.dtype)
        hp, S1, s1, k1, v1 = mixer_layer(hp, p_prompt[l], pos_p, S0p, sh0p, None, None, *wl)
        hs, S2, s2, k2, v2 = mixer_layer(hs, p_sample[l], pos_s, state_rwkv_wkv[l], state_rwkv_shift[l],
                                         cache_swa_k[l], cache_swa_v[l], *wl)
        wkv_p.append(S1); sh_p.append(s1); k_p.append(k1); v_p.append(v1)
        wkv_s.append(S2); sh_s.append(s2); k_s.append(k2); v_s.append(v2)
    y_prompt = rmsnorm(hp, g_final)
    y_sample = rmsnorm(hs, g_final)
    return (y_prompt, y_sample,
            jnp.stack(wkv_p), jnp.stack(sh_p), jnp.stack(k_p), jnp.stack(v_p),
            jnp.stack(wkv_s), jnp.stack(sh_s), jnp.stack(k_s), jnp.stack(v_s))
```

```python
import functools
import math

import jax
import jax.numpy as jnp
from jax import lax
from jax.experimental import pallas as pl
from jax.experimental.pallas import tpu as pltpu

D_MODEL = 1024
HEAD_DIM = 64
RWKV_WIDTH = 512
RWKV_HEADS = 8
ATTN_WIDTH = 512
ATTN_Q_HEADS = 8
ATTN_KV_HEADS = 2
ATTN_GROUP = 4
KV_WIDTH = 128
LORA = 64
WINDOW = 128
ROPE_THETA = 500000.0
ROPE_DIM = 16
PLE_DIM = 256
NORM_EPS = 1e-6
GN_EPS = 64e-5
NEG_INF = -1e30
PAST_LEN = 16384
SHIFT_DIM = 3 * RWKV_WIDTH + 2 * LORA
NAT_DIM = RWKV_WIDTH + ATTN_WIDTH + 2 * KV_WIDTH + ATTN_WIDTH

LANES = 128
SUBLANES = 8
VMEM_LIMIT = 48 * 1024 * 1024
DECAY_SCALE = math.exp(-0.5)

Q_KK, Q_W, Q_B, Q_K, Q_R, Q_V = range(6)
NQ = 6

f32 = jnp.float32
bf16 = jnp.bfloat16


def _cparams(n_axes):
    return pltpu.CompilerParams(dimension_semantics=("arbitrary",) * n_axes,
                                vmem_limit_bytes=VMEM_LIMIT)


def _rmsnorm(x, g):
    ms = jnp.mean(x * x, axis=-1, keepdims=True)
    return x * lax.rsqrt(ms + NORM_EPS) * g


def _sigmoid(x):
    return 1.0 / (1.0 + jnp.exp(-x))


def _dot_nt(a, b):
    return lax.dot_general(a, b, (((1,), (1,)), ((), ())), preferred_element_type=f32)


def _dot(a, b):
    return jnp.dot(a, b, preferred_element_type=f32)


def _chunk_transpose(xs, chunk):
    lane = lax.broadcasted_iota(jnp.int32, xs[0].shape, 1)
    xs = list(xs)
    for d in (4, 2, 1):
        hi_lanes = (lane & (chunk * d)) != 0
        nxt = list(xs)
        for i in range(8):
            if i & d:
                continue
            lo, hi = xs[i], xs[i + d]
            nxt[i] = jnp.where(hi_lanes, pltpu.roll(hi, chunk * d, 1), lo)
            nxt[i + d] = jnp.where(hi_lanes, hi, pltpu.roll(lo, LANES - chunk * d, 1))
        xs = nxt
    return xs


def _rwkv_proj_kernel(x_ref, g_ref, wt_ref, mu_ref, w0_ref, a0_ref, kk_ref, ka_ref, w2t_ref,
                      a2t_ref, shift0_ref, out_ref, shift_ref, carry_ref, *, tt, nb, heads_on_lanes):
    i = pl.program_id(0)

    @pl.when(i == 0)
    def _():
        carry_ref[...] = shift0_ref[...]

    n = tt * nb
    x = x_ref[...].reshape(n, D_MODEL)
    u = _rmsnorm(x, g_ref[...]).astype(bf16)
    zt = _dot_nt(wt_ref[...], u)

    mu = mu_ref[...]
    lane = lax.broadcasted_iota(jnp.int32, (SHIFT_DIM, LANES), 1)
    nblk = n // LANES
    prev_blk = carry_ref[...]
    for j in range(nblk):
        z = zt[:, j * LANES:(j + 1) * LANES]
        if nb % LANES == 0:
            prev = prev_blk
        else:
            prev = jnp.where(lane < nb, pltpu.roll(prev_blk, nb, 1), pltpu.roll(z, nb, 1))
        prev_blk = z
        zs = z + mu * (prev - z)
        r = zs[0:RWKV_WIDTH]
        kx = zs[RWKV_WIDTH:2 * RWKV_WIDTH]
        v = zs[2 * RWKV_WIDTH:3 * RWKV_WIDTH]
        wd = zs[3 * RWKV_WIDTH:3 * RWKV_WIDTH + LORA]
        ad = zs[3 * RWKV_WIDTH + LORA:]
        w_pre = w0_ref[...] + _dot(w2t_ref[...], jnp.tanh(wd).astype(bf16))
        decay = jnp.exp(-DECAY_SCALE * _sigmoid(w_pre))
        a = _sigmoid(a0_ref[...] + _dot(a2t_ref[...], ad.astype(bf16)))
        kkr = kx * kk_ref[...]
        ss = jnp.sum((kkr * kkr).reshape(RWKV_HEADS, HEAD_DIM, LANES), axis=1, keepdims=True)
        inv = 1.0 / jnp.maximum(jnp.sqrt(ss), 1e-12)
        kk = (kkr.reshape(RWKV_HEADS, HEAD_DIM, LANES) * inv).reshape(RWKV_WIDTH, LANES)
        k = kx * (1.0 + (a - 1.0) * ka_ref[...])
        b = kk * a
        quantities = {Q_KK: kk, Q_W: decay, Q_B: b, Q_K: k, Q_R: r, Q_V: v}
        for q, val in quantities.items():
            per_head = [val[h * HEAD_DIM:(h + 1) * HEAD_DIM] for h in range(RWKV_HEADS)]
            if heads_on_lanes:
                per_t = _chunk_transpose(per_head, nb)
                for t in range(8):
                    out_ref[0, j * 8 + t, q] = per_t[t]
            else:
                for h in range(RWKV_HEADS):
                    out_ref[h, j, q] = per_head[h]
    carry_ref[...] = prev_blk
    shift_ref[...] = prev_blk


def _rwkv_proj(xt, g_norm, wt, mu_t, w0_t, a0_t, kk_t, ka_t, w2t, a2t, shift0_t, *, tt, heads_on_lanes):
    T, nb, _ = xt.shape
    groups = 1 if heads_on_lanes else RWKV_HEADS
    const = lambda shape: pl.BlockSpec(shape, lambda i: (0,) * len(shape))
    kern = functools.partial(_rwkv_proj_kernel, tt=tt, nb=nb, heads_on_lanes=heads_on_lanes)
    return pl.pallas_call(
        kern,
        grid=(T // tt,),
        in_specs=[
            pl.BlockSpec((tt, nb, D_MODEL), lambda i: (i, 0, 0)),
            const((1, D_MODEL)),
            const((SHIFT_DIM, D_MODEL)),
            const((SHIFT_DIM, LANES)),
            const((RWKV_WIDTH, LANES)), const((RWKV_WIDTH, LANES)),
            const((RWKV_WIDTH, LANES)), const((RWKV_WIDTH, LANES)),
            const((RWKV_WIDTH, LORA)), const((RWKV_WIDTH, LORA)),
            const((SHIFT_DIM, LANES)),
        ],
        out_specs=[
            pl.BlockSpec((groups, tt, NQ, HEAD_DIM, LANES), lambda i: (0, i, 0, 0, 0)),
            const((SHIFT_DIM, LANES)),
        ],
        out_shape=[
            jax.ShapeDtypeStruct((groups, T, NQ, HEAD_DIM, LANES), f32),
            jax.ShapeDtypeStruct((SHIFT_DIM, LANES), f32),
        ],
        scratch_shapes=[pltpu.VMEM((SHIFT_DIM, LANES), f32)],
        compiler_params=_cparams(1),
        name="rwkv_proj",
    )(xt, g_norm, wt, mu_t, w0_t, a0_t, kk_t, ka_t, w2t, a2t, shift0_t)


def _rope(x, c, a, b):
    return x * c + pltpu.roll(x, LANES - ROPE_DIM // 2, 1) * a + pltpu.roll(x, ROPE_DIM // 2, 1) * b


def _nat_proj_kernel(x_ref, g_ref, w_ref, rc_ref, ra_ref, rb_ref,
                     sgr_ref, q_ref, k_ref, v_ref, sga_ref):
    u = _rmsnorm(x_ref[...], g_ref[...]).astype(bf16)
    z = _dot(u, w_ref[...])
    o_q = RWKV_WIDTH
    o_k = o_q + ATTN_WIDTH
    o_v = o_k + KV_WIDTH
    o_g = o_v + KV_WIDTH
    gr = z[:, :o_q]
    sgr_ref[...] = gr * _sigmoid(gr)
    rc, ra, rb = rc_ref[...], ra_ref[...], rb_ref[...]
    for j in range(ATTN_WIDTH // LANES):
        qj = z[:, o_q + j * LANES:o_q + (j + 1) * LANES]
        q_ref[:, j * LANES:(j + 1) * LANES] = _rope(qj, rc, ra, rb) * (HEAD_DIM ** -0.5)
    k_ref[...] = _rope(z[:, o_k:o_v], rc, ra, rb)
    v_ref[...] = z[:, o_v:o_g]
    ga = z[:, o_g:]
    sga_ref[...] = ga * _sigmoid(ga)


def _nat_proj(x2d, g_norm, w_nat, rope_c, rope_a, rope_b, *, tm):
    n_tok = x2d.shape[0]
    n_tab = rope_c.shape[0] // tm
    const = lambda shape: pl.BlockSpec(shape, lambda i: (0,) * len(shape))
    row = lambda w: pl.BlockSpec((tm, w), lambda i: (i, 0))
    tab = pl.BlockSpec((tm, LANES), lambda i: (i % n_tab, 0))
    return pl.pallas_call(
        _nat_proj_kernel,
        grid=(n_tok // tm,),
        in_specs=[row(D_MODEL), const((1, D_MODEL)), const((D_MODEL, NAT_DIM)), tab, tab, tab],
        out_specs=[row(RWKV_WIDTH), row(ATTN_WIDTH), row(KV_WIDTH), row(KV_WIDTH), row(ATTN_WIDTH)],
        out_shape=[jax.ShapeDtypeStruct((n_tok, w), f32)
                   for w in (RWKV_WIDTH, ATTN_WIDTH, KV_WIDTH, KV_WIDTH, ATTN_WIDTH)],
        compiler_params=_cparams(1),
        name="nat_proj",
    )(x2d, g_norm, w_nat, rope_c, rope_a, rope_b)


def _row_bcast(ref, idx, k):
    return jnp.broadcast_to(ref[idx + (pl.ds(k, 1), slice(None))], (HEAD_DIM, LANES))


def _wkv_scan_kernel(in_ref, s0_ref, rk_ref, lnw_ref, lnb_ref, o_ref, sout_ref, s_ref, *rest,
                     tc, natural_out):
    c = pl.program_id(1)

    @pl.when(c == 0)
    def _():
        s_ref[...] = s0_ref[0]

    obuf_ref = rest[0] if natural_out else None
    rk = rk_ref[0]
    lnw = lnw_ref[0]
    lnb = lnb_ref[0]

    def step(t, carry):
        vv = in_ref[0, t, Q_V]
        sa = jnp.zeros((HEAD_DIM, LANES), f32)
        for k in range(HEAD_DIM):
            sa = sa + s_ref[k] * _row_bcast(in_ref, (0, t, Q_KK), k)
        y = jnp.zeros((HEAD_DIM, LANES), f32)
        for k in range(HEAD_DIM):
            s_new = (s_ref[k] * _row_bcast(in_ref, (0, t, Q_W), k)
                     - sa * _row_bcast(in_ref, (0, t, Q_B), k)
                     + vv * _row_bcast(in_ref, (0, t, Q_K), k))
            s_ref[k] = s_new
            y = y + s_new * _row_bcast(in_ref, (0, t, Q_R), k)
        mean = jnp.mean(y, axis=0, keepdims=True)
        d = y - mean
        var = jnp.mean(d * d, axis=0, keepdims=True)
        yn = d * lax.rsqrt(var + GN_EPS) * lnw + lnb
        rkk = jnp.sum(in_ref[0, t, Q_R] * in_ref[0, t, Q_K] * rk, axis=0, keepdims=True)
        o = yn + rkk * vv
        if natural_out:
            obuf_ref[t] = o
        else:
            o_ref[0, t] = o
        return carry

    lax.fori_loop(0, tc, step, 0)

    if natural_out:
        nb = LANES // RWKV_HEADS
        for j in range(tc // 8):
            per_h = _chunk_transpose([obuf_ref[j * 8 + t] for t in range(8)], nb)
            ot = jnp.concatenate(per_h, axis=0)
            o_ref[j * 8:(j + 1) * 8] = ot.T.reshape(8, nb, RWKV_WIDTH)

    @pl.when(c == pl.num_programs(1) - 1)
    def _():
        sout_ref[0] = s_ref[...]


def _wkv_scan(scan_in, s0, rk_t, lnw_t, lnb_t, *, tc, natural_out):
    groups, T = scan_in.shape[:2]
    tile = pl.BlockSpec((1, HEAD_DIM, LANES), lambda g, c: (g, 0, 0))
    state = pl.BlockSpec((1, HEAD_DIM, HEAD_DIM, LANES), lambda g, c: (g, 0, 0, 0))
    if natural_out:
        nb = LANES // RWKV_HEADS
        o_spec = pl.BlockSpec((tc, nb, RWKV_WIDTH), lambda g, c: (c, 0, 0))
        o_shape = jax.ShapeDtypeStruct((T, nb, RWKV_WIDTH), f32)
        scratch = [pltpu.VMEM((HEAD_DIM, HEAD_DIM, LANES), f32), pltpu.VMEM((tc, HEAD_DIM, LANES), f32)]
    else:
        o_spec = pl.BlockSpec((1, tc, HEAD_DIM, LANES), lambda g, c: (g, c, 0, 0))
        o_shape = jax.ShapeDtypeStruct((groups, T, HEAD_DIM, LANES), f32)
        scratch = [pltpu.VMEM((HEAD_DIM, HEAD_DIM, LANES), f32)]
    kern = functools.partial(_wkv_scan_kernel, tc=tc, natural_out=natural_out)
    return pl.pallas_call(
        kern,
        grid=(groups, T // tc),
        in_specs=[
            pl.BlockSpec((1, tc, NQ, HEAD_DIM, LANES), lambda g, c: (g, c, 0, 0, 0)),
            state, tile, tile, tile,
        ],
        out_specs=[o_spec, state],
        out_shape=[o_shape, jax.ShapeDtypeStruct((groups, HEAD_DIM, HEAD_DIM, LANES), f32)],
        scratch_shapes=scratch,
        compiler_params=_cparams(2),
        name="wkv_scan",
    )(scan_in, s0, rk_t, lnw_t, lnb_t)


def _swa_attn_kernel(q_ref, kp_ref, kc_ref, vp_ref, vc_ref, sga_ref, sink_ref, o_ref, *,
                     bb, tq, first_block_has_no_prev):
    n = pl.program_id(1)
    row_p = lax.broadcasted_iota(jnp.int32, (tq, WINDOW), 0)
    col_p = lax.broadcasted_iota(jnp.int32, (tq, WINDOW), 1)
    mask_p = col_p > row_p
    if first_block_has_no_prev:
        mask_p = jnp.logical_and(mask_p, n > 0)
    row_c = lax.broadcasted_iota(jnp.int32, (tq, tq), 0)
    col_c = lax.broadcasted_iota(jnp.int32, (tq, tq), 1)
    mask_c = col_c <= row_c

    def one_batch(b, carry):
        outs = []
        for g in range(ATTN_KV_HEADS):
            lo, hi = g * HEAD_DIM, (g + 1) * HEAD_DIM
            kp = kp_ref[b, :, lo:hi].astype(bf16)
            kc = kc_ref[b, :, lo:hi].astype(bf16)
            vp = vp_ref[b, :, lo:hi].astype(bf16)
            vc = vc_ref[b, :, lo:hi].astype(bf16)
            for hh in range(ATTN_GROUP):
                h = g * ATTN_GROUP + hh
                qh = q_ref[b, :, h * HEAD_DIM:(h + 1) * HEAD_DIM].astype(bf16)
                sp = jnp.where(mask_p, _dot_nt(qh, kp), NEG_INF)
                sc = jnp.where(mask_c, _dot_nt(qh, kc), NEG_INF)
                sink = sink_ref[h:h + 1, 0:1]
                m = jnp.maximum(jnp.maximum(jnp.max(sp, axis=-1, keepdims=True),
                                            jnp.max(sc, axis=-1, keepdims=True)), sink)
                pp = jnp.exp(sp - m)
                pc = jnp.exp(sc - m)
                denom = (jnp.sum(pp, axis=-1, keepdims=True) + jnp.sum(pc, axis=-1, keepdims=True)
                         + jnp.exp(sink - m))
                pv = _dot(pp.astype(bf16), vp) + _dot(pc.astype(bf16), vc)
                outs.append(pv * (1.0 / denom))
        o_ref[b] = jnp.concatenate(outs, axis=-1) * sga_ref[b]
        return carry

    lax.fori_loop(0, bb, one_batch, 0)


def _swa_attn(q, k_prev, k_cur, v_prev, v_cur, sga, sink_t, *, bb, tq, prev_is_same_array):
    B, T, _ = q.shape
    nq = T // tq
    cur = lambda w: pl.BlockSpec((bb, tq, w), lambda bi, n: (bi, n, 0))
    if prev_is_same_array:
        prev = pl.BlockSpec((bb, WINDOW, KV_WIDTH), lambda bi, n: (bi, jnp.maximum(n - 1, 0), 0))
    else:
        prev = pl.BlockSpec((bb, WINDOW, KV_WIDTH), lambda bi, n: (bi, 0, 0))
    kern = functools.partial(_swa_attn_kernel, bb=bb, tq=tq, first_block_has_no_prev=prev_is_same_array)
    return pl.pallas_call(
        kern,
        grid=(B // bb, nq),
        in_specs=[cur(ATTN_WIDTH), prev, cur(KV_WIDTH), prev, cur(KV_WIDTH), cur(ATTN_WIDTH),
                  pl.BlockSpec((ATTN_Q_HEADS, LANES), lambda bi, n: (0, 0))],
        out_specs=cur(ATTN_WIDTH),
        out_shape=jax.ShapeDtypeStruct((B, T, ATTN_WIDTH), f32),
        compiler_params=_cparams(2),
        name="swa_attn",
    )(q, k_prev, k_cur, v_prev, v_cur, sga, sink_t)


def _out_mix_kernel(x_ref, or_ref, sgr_ref, ma_ref, p_ref, wo_ref, gp_ref, wpg_ref, wpp_ref, gf_ref,
                    y_ref):
    mr = (or_ref[...] * sgr_ref[...]).astype(bf16)
    ma = ma_ref[...].astype(bf16)
    h = (x_ref[...] + _dot(mr, wo_ref[:RWKV_WIDTH, :]) + _dot(ma, wo_ref[RWKV_WIDTH:, :]))
    gate = _sigmoid(_dot(_rmsnorm(h, gp_ref[...]).astype(bf16), wpg_ref[...]))
    h = h + gate * _dot(p_ref[...].astype(bf16), wpp_ref[...])
    y_ref[...] = _rmsnorm(h, gf_ref[...])


def _out_mix(x2d, o_r, sgr, ma, p2d, w_out, g_ple, w_pg, w_pp, g_final, *, tm):
    n_tok = x2d.shape[0]
    const = lambda shape: pl.BlockSpec(shape, lambda i: (0,) * len(shape))
    row = lambda w: pl.BlockSpec((tm, w), lambda i: (i, 0))
    return pl.pallas_call(
        _out_mix_kernel,
        grid=(n_tok // tm,),
        in_specs=[row(D_MODEL), row(RWKV_WIDTH), row(RWKV_WIDTH), row(ATTN_WIDTH), row(PLE_DIM),
                  const((D_MODEL, D_MODEL)), const((1, D_MODEL)), const((D_MODEL, D_MODEL)),
                  const((PLE_DIM, D_MODEL)), const((1, D_MODEL))],
        out_specs=row(D_MODEL),
        out_shape=jax.ShapeDtypeStruct((n_tok, D_MODEL), f32),
        compiler_params=_cparams(1),
        name="out_mix",
    )(x2d, o_r, sgr, ma, p2d, w_out, g_ple, w_pg, w_pp, g_final)


def _rope_tables(pos):
    half = ROPE_DIM // 2
    inv = ROPE_THETA ** (-jnp.arange(half, dtype=f32) / half)
    ang = pos.astype(f32)[:, None] * inv[None, :]
    cos, sin = jnp.cos(ang), jnp.sin(ang)
    n = pos.shape[0]
    ones = jnp.ones((n, HEAD_DIM - ROPE_DIM), f32)
    zeros = jnp.zeros((n, HEAD_DIM - ROPE_DIM), f32)
    zh = jnp.zeros((n, half), f32)
    c = jnp.concatenate([cos, cos, ones], axis=1)
    a = jnp.concatenate([-sin, zh, zeros], axis=1)
    b = jnp.concatenate([zh, sin, zeros], axis=1)
    rep = LANES // HEAD_DIM
    return tuple(jnp.tile(t, (1, rep)) for t in (c, a, b))


def _col_tile(vec):
    return jnp.broadcast_to(vec[:, None], (vec.shape[0], LANES))


def _lane_param_tile(vec, heads_on_lanes):
    ph = vec.reshape(RWKV_HEADS, HEAD_DIM)
    if heads_on_lanes:
        nb = LANES // RWKV_HEADS
        return jnp.repeat(ph.T, nb, axis=1)[None]
    return jnp.broadcast_to(ph[:, :, None], (RWKV_HEADS, HEAD_DIM, LANES))


def _layer(x, p, pos, s0, shift0, kbuf, vbuf, wts, *, heads_on_lanes, tt, tc, tm, att_bb, att_tq):
    B, T, _ = x.shape
    n_tok = B * T
    x2d = x.reshape(n_tok, D_MODEL)

    xt = jnp.swapaxes(x, 0, 1)
    if shift0 is None:
        shift0_t = jnp.zeros((SHIFT_DIM, LANES), f32)
    else:
        shift0_t = jnp.pad(shift0.T, ((0, 0), (LANES - B, 0)))
    scan_in, shift_t = _rwkv_proj(xt, wts["g_norm"], wts["wt_rwkv"], wts["mu_t"], wts["w0_t"],
                                  wts["a0_t"], wts["kk_t"], wts["ka_t"], wts["w2t"], wts["a2t"],
                                  shift0_t, tt=tt, heads_on_lanes=heads_on_lanes)
    shift_new = shift_t[:, LANES - B:].T if B < LANES else shift_t.T
    groups = scan_in.shape[0]
    if s0 is None:
        s0_t = jnp.zeros((groups, HEAD_DIM, HEAD_DIM, LANES), f32)
    else:
        s0_t = jnp.transpose(s0, (1, 3, 2, 0))
    rk_t = _lane_param_tile(wts["r_k"], heads_on_lanes)
    lnw_t = _lane_param_tile(wts["ln_w"], heads_on_lanes)
    lnb_t = _lane_param_tile(wts["ln_b"], heads_on_lanes)
    o_scan, s_fin = _wkv_scan(scan_in, s0_t, rk_t, lnw_t, lnb_t, tc=tc, natural_out=heads_on_lanes)
    if heads_on_lanes:
        o_r = jnp.swapaxes(o_scan, 0, 1).reshape(n_tok, RWKV_WIDTH)
        s_new = jnp.transpose(s_fin[0].reshape(HEAD_DIM, HEAD_DIM, RWKV_HEADS, B), (3, 2, 1, 0))
    else:
        o_r = jnp.transpose(o_scan, (3, 1, 0, 2)).reshape(n_tok, RWKV_WIDTH)
        s_new = jnp.transpose(s_fin, (3, 0, 2, 1))

    rope_c, rope_a, rope_b = _rope_tables(pos)
    if T < tm:
        reps = tm // T
        rope_c, rope_a, rope_b = (jnp.tile(t, (reps, 1)) for t in (rope_c, rope_a, rope_b))
    sgr, q, k, v, sga = _nat_proj(x2d, wts["g_norm"], wts["w_nat"], rope_c, rope_a, rope_b, tm=tm)
    q3 = q.reshape(B, T, ATTN_WIDTH)
    k3 = k.reshape(B, T, KV_WIDTH)
    v3 = v.reshape(B, T, KV_WIDTH)
    sga3 = sga.reshape(B, T, ATTN_WIDTH)
    if kbuf is None:
        ma = _swa_attn(q3, k3, k3, v3, v3, sga3, wts["sink_t"], bb=att_bb, tq=att_tq,
                       prev_is_same_array=True)
        k_new = k3[:, T - WINDOW:].reshape(B, WINDOW, ATTN_KV_HEADS, HEAD_DIM)
        v_new = v3[:, T - WINDOW:].reshape(B, WINDOW, ATTN_KV_HEADS, HEAD_DIM)
    else:
        kb = kbuf.reshape(B, WINDOW, KV_WIDTH)
        vb = vbuf.reshape(B, WINDOW, KV_WIDTH)
        ma = _swa_attn(q3, kb, k3, vb, v3, sga3, wts["sink_t"], bb=att_bb, tq=att_tq,
                       prev_is_same_array=False)
        k_new = jnp.concatenate([kb, k3], axis=1)[:, -WINDOW:].reshape(B, WINDOW, ATTN_KV_HEADS, HEAD_DIM)
        v_new = jnp.concatenate([vb, v3], axis=1)[:, -WINDOW:].reshape(B, WINDOW, ATTN_KV_HEADS, HEAD_DIM)

    y = _out_mix(x2d, o_r, sgr, ma.reshape(n_tok, ATTN_WIDTH), p.reshape(n_tok, PLE_DIM),
                 wts["w_out"], wts["g_ple"], wts["w_pg"], wts["w_pp"], wts["g_final"], tm=tm)
    return y.reshape(B, T, D_MODEL), s_new, shift_new, k_new, v_new


def kernel(x_prompt, x_sample, state_rwkv_wkv, state_rwkv_shift, cache_swa_k, cache_swa_v,
           p_prompt, p_sample, g_norm, w_in, mu_shift, w0, w2, a0, a2, k_k, k_a, r_k,
           ln_w, ln_b, sinks, w_out, g_ple, w_ple_gate, w_ple_proj, g_final):
    assert w_in.shape[0] == 1, "single layer"
    w_in0 = w_in[0]
    wts = {
        "g_norm": g_norm[0][None, :],
        "wt_rwkv": w_in0[:, :SHIFT_DIM].T.astype(bf16),
        "w_nat": w_in0[:, SHIFT_DIM:].astype(bf16),
        "mu_t": _col_tile(mu_shift[0]),
        "w0_t": _col_tile(w0[0]), "a0_t": _col_tile(a0[0]),
        "kk_t": _col_tile(k_k[0]), "ka_t": _col_tile(k_a[0]),
        "w2t": w2[0].T.astype(bf16), "a2t": a2[0].T.astype(bf16),
        "r_k": r_k[0], "ln_w": ln_w[0], "ln_b": ln_b[0],
        "sink_t": jnp.broadcast_to(sinks[0][:, None], (ATTN_Q_HEADS, LANES)),
        "w_out": w_out[0].astype(bf16), "g_ple": g_ple[0][None, :],
        "w_pg": w_ple_gate[0].astype(bf16), "w_pp": w_ple_proj[0].astype(bf16),
        "g_final": g_final[None, :],
    }
    Bp, Tp, _ = x_prompt.shape
    Bs, Ts, _ = x_sample.shape
    assert Bp * RWKV_HEADS == LANES and Bs == LANES and Tp % WINDOW == 0 and Ts % SUBLANES == 0

    yp, s1, sh1, k1, v1 = _layer(x_prompt, p_prompt[0], jnp.arange(Tp), None, None, None, None, wts,
                                 heads_on_lanes=True, tt=16, tc=16, tm=512, att_bb=1, att_tq=WINDOW)
    ys, s2, sh2, k2, v2 = _layer(x_sample, p_sample[0], PAST_LEN + jnp.arange(Ts),
                                 state_rwkv_wkv[0], state_rwkv_shift[0], cache_swa_k[0], cache_swa_v[0],
                                 wts, heads_on_lanes=False, tt=2, tc=Ts, tm=Bs * Ts, att_bb=8, att_tq=Ts)
    return (yp, ys, s1[None], sh1[None], k1[None], v1[None], s2[None], sh2[None], k2[None], v2[None])
```

```python
import functools
import math

import jax
import jax.numpy as jnp
from jax import lax
from jax.experimental import pallas as pl
from jax.experimental.pallas import tpu as pltpu

D_MODEL = 1024
HEAD_DIM = 64
RWKV_WIDTH = 512
RWKV_HEADS = 8
ATTN_WIDTH = 512
ATTN_Q_HEADS = 8
ATTN_KV_HEADS = 2
ATTN_GROUP = 4
KV_WIDTH = 128
LORA = 64
WINDOW = 128
ROPE_THETA = 500000.0
ROPE_DIM = 16
PLE_DIM = 256
NORM_EPS = 1e-6
GN_EPS = 64e-5
NEG_INF = -1e30
PAST_LEN = 16384
SHIFT_DIM = 3 * RWKV_WIDTH + 2 * LORA
NAT_DIM = RWKV_WIDTH + ATTN_WIDTH + 2 * KV_WIDTH + ATTN_WIDTH

LANES = 128
SUBLANES = 8
VMEM_LIMIT = 48 * 1024 * 1024
DECAY_SCALE = math.exp(-0.5)

Q_KK, Q_W, Q_B, Q_K, Q_R, Q_V = range(6)
NQ = 6

f32 = jnp.float32
bf16 = jnp.bfloat16


def _cparams(n_axes):
    return pltpu.CompilerParams(dimension_semantics=("arbitrary",) * n_axes,
                                vmem_limit_bytes=VMEM_LIMIT)


def _rmsnorm(x, g):
    ms = jnp.mean(x * x, axis=-1, keepdims=True)
    return x * lax.rsqrt(ms + NORM_EPS) * g


def _sigmoid(x):
    return 1.0 / (1.0 + jnp.exp(-x))


def _dot_nt(a, b):
    return lax.dot_general(a, b, (((1,), (1,)), ((), ())), preferred_element_type=f32)


def _dot(a, b):
    return jnp.dot(a, b, preferred_element_type=f32)


def _chunk_transpose(xs, chunk):
    lane = lax.broadcasted_iota(jnp.int32, xs[0].shape, 1)
    xs = list(xs)
    for d in (4, 2, 1):
        hi_lanes = (lane & (chunk * d)) != 0
        nxt = list(xs)
        for i in range(8):
            if i & d:
                continue
            lo, hi = xs[i], xs[i + d]
            nxt[i] = jnp.where(hi_lanes, pltpu.roll(hi, chunk * d, 1), lo)
            nxt[i + d] = jnp.where(hi_lanes, hi, pltpu.roll(lo, LANES - chunk * d, 1))
        xs = nxt
    return xs


def _rwkv_proj_kernel(x_ref, g_ref, wt_ref, mu_ref, w0_ref, a0_ref, kk_ref, ka_ref, w2t_ref,
                      a2t_ref, shift0_ref, out_ref, shift_ref, carry_ref, *, tt, nb, heads_on_lanes):
    i = pl.program_id(0)

    @pl.when(i == 0)
    def _():
        carry_ref[...] = shift0_ref[...]

    n = tt * nb
    x = x_ref[...].reshape(n, D_MODEL)
    u = _rmsnorm(x, g_ref[...]).astype(bf16)
    zt = _dot_nt(wt_ref[...], u)

    mu = mu_ref[...]
    lane = lax.broadcasted_iota(jnp.int32, (SHIFT_DIM, LANES), 1)
    nblk = n // LANES
    prev_blk = carry_ref[...]
    for j in range(nblk):
        z = zt[:, j * LANES:(j + 1) * LANES]
        if nb % LANES == 0:
            prev = prev_blk
        else:
            prev = jnp.where(lane < nb, pltpu.roll(prev_blk, nb, 1), pltpu.roll(z, nb, 1))
        prev_blk = z
        zs = z + mu * (prev - z)
        r = zs[0:RWKV_WIDTH]
        kx = zs[RWKV_WIDTH:2 * RWKV_WIDTH]
        v = zs[2 * RWKV_WIDTH:3 * RWKV_WIDTH]
        wd = zs[3 * RWKV_WIDTH:3 * RWKV_WIDTH + LORA]
        ad = zs[3 * RWKV_WIDTH + LORA:]
        w_pre = w0_ref[...] + _dot(w2t_ref[...], jnp.tanh(wd).astype(bf16))
        decay = jnp.exp(-DECAY_SCALE * _sigmoid(w_pre))
        a = _sigmoid(a0_ref[...] + _dot(a2t_ref[...], ad.astype(bf16)))
        kkr = kx * kk_ref[...]
        ss = jnp.sum((kkr * kkr).reshape(RWKV_HEADS, HEAD_DIM, LANES), axis=1, keepdims=True)
        inv = 1.0 / jnp.maximum(jnp.sqrt(ss), 1e-12)
        kk = (kkr.reshape(RWKV_HEADS, HEAD_DIM, LANES) * inv).reshape(RWKV_WIDTH, LANES)
        k = kx * (1.0 + (a - 1.0) * ka_ref[...])
        b = kk * a
        quantities = {Q_KK: kk, Q_W: decay, Q_B: b, Q_K: k, Q_R: r, Q_V: v}
        for q, val in quantities.items():
            per_head = [val[h * HEAD_DIM:(h + 1) * HEAD_DIM] for h in range(RWKV_HEADS)]
            if heads_on_lanes:
                per_t = _chunk_transpose(per_head, nb)
                for t in range(8):
                    out_ref[0, j * 8 + t, q] = per_t[t]
            else:
                for h in range(RWKV_HEADS):
                    out_ref[h, j, q] = per_head[h]
    carry_ref[...] = prev_blk
    shift_ref[...] = prev_blk


def _rwkv_proj(xt, g_norm, wt, mu_t, w0_t, a0_t, kk_t, ka_t, w2t, a2t, shift0_t, *, tt, heads_on_lanes):
    T, nb, _ = xt.shape
    groups = 1 if heads_on_lanes else RWKV_HEADS
    const = lambda shape: pl.BlockSpec(shape, lambda i: (0,) * len(shape))
    kern = functools.partial(_rwkv_proj_kernel, tt=tt, nb=nb, heads_on_lanes=heads_on_lanes)
    return pl.pallas_call(
        kern,
        grid=(T // tt,),
        in_specs=[
            pl.BlockSpec((tt, nb, D_MODEL), lambda i: (i, 0, 0)),
            const((1, D_MODEL)),
            const((SHIFT_DIM, D_MODEL)),
            const((SHIFT_DIM, LANES)),
            const((RWKV_WIDTH, LANES)), const((RWKV_WIDTH, LANES)),
            const((RWKV_WIDTH, LANES)), const((RWKV_WIDTH, LANES)),
            const((RWKV_WIDTH, LORA)), const((RWKV_WIDTH, LORA)),
            const((SHIFT_DIM, LANES)),
        ],
        out_specs=[
            pl.BlockSpec((groups, tt, NQ, HEAD_DIM, LANES), lambda i: (0, i, 0, 0, 0)),
            const((SHIFT_DIM, LANES)),
        ],
        out_shape=[
            jax.ShapeDtypeStruct((groups, T, NQ, HEAD_DIM, LANES), f32),
            jax.ShapeDtypeStruct((SHIFT_DIM, LANES), f32),
        ],
        scratch_shapes=[pltpu.VMEM((SHIFT_DIM, LANES), f32)],
        compiler_params=_cparams(1),
        name="rwkv_proj",
    )(xt, g_norm, wt, mu_t, w0_t, a0_t, kk_t, ka_t, w2t, a2t, shift0_t)


def _rope(x, c, a, b):
    return x * c + pltpu.roll(x, LANES - ROPE_DIM // 2, 1) * a + pltpu.roll(x, ROPE_DIM // 2, 1) * b


def _nat_proj_kernel(x_ref, g_ref, w_ref, rc_ref, ra_ref, rb_ref,
                     sgr_ref, q_ref, k_ref, v_ref, sga_ref):
    u = _rmsnorm(x_ref[...], g_ref[...]).astype(bf16)
    z = _dot(u, w_ref[...])
    o_q = RWKV_WIDTH
    o_k = o_q + ATTN_WIDTH
    o_v = o_k + KV_WIDTH
    o_g = o_v + KV_WIDTH
    gr = z[:, :o_q]
    sgr_ref[...] = gr * _sigmoid(gr)
    rc, ra, rb = rc_ref[...], ra_ref[...], rb_ref[...]
    for j in range(ATTN_WIDTH // LANES):
        qj = z[:, o_q + j * LANES:o_q + (j + 1) * LANES]
        q_ref[:, j * LANES:(j + 1) * LANES] = _rope(qj, rc, ra, rb) * (HEAD_DIM ** -0.5)
    k_ref[...] = _rope(z[:, o_k:o_v], rc, ra, rb)
    v_ref[...] = z[:, o_v:o_g]
    ga = z[:, o_g:]
    sga_ref[...] = ga * _sigmoid(ga)


def _nat_proj(x2d, g_norm, w_nat, rope_c, rope_a, rope_b, *, tm):
    n_tok = x2d.shape[0]
    n_tab = rope_c.shape[0] // tm
    const = lambda shape: pl.BlockSpec(shape, lambda i: (0,) * len(shape))
    row = lambda w: pl.BlockSpec((tm, w), lambda i: (i, 0))
    tab = pl.BlockSpec((tm, LANES), lambda i: (i % n_tab, 0))
    return pl.pallas_call(
        _nat_proj_kernel,
        grid=(n_tok // tm,),
        in_specs=[row(D_MODEL), const((1, D_MODEL)), const((D_MODEL, NAT_DIM)), tab, tab, tab],
        out_specs=[row(RWKV_WIDTH), row(ATTN_WIDTH), row(KV_WIDTH), row(KV_WIDTH), row(ATTN_WIDTH)],
        out_shape=[jax.ShapeDtypeStruct((n_tok, w), f32)
                   for w in (RWKV_WIDTH, ATTN_WIDTH, KV_WIDTH, KV_WIDTH, ATTN_WIDTH)],
        compiler_params=_cparams(1),
        name="nat_proj",
    )(x2d, g_norm, w_nat, rope_c, rope_a, rope_b)


def _row_bcast(ref, idx, k):
    return jnp.broadcast_to(ref[idx + (pl.ds(k, 1), slice(None))], (HEAD_DIM, LANES))


def _wkv_scan_kernel(in_ref, s0_ref, rk_ref, lnw_ref, lnb_ref, o_ref, sout_ref, s_ref, *rest,
                     tc, natural_out):
    c = pl.program_id(1)

    @pl.when(c == 0)
    def _():
        s_ref[...] = s0_ref[0]

    obuf_ref = rest[0] if natural_out else None
    rk = rk_ref[0]
    lnw = lnw_ref[0]
    lnb = lnb_ref[0]

    sa0 = jnp.zeros((HEAD_DIM, LANES), f32)
    for k in range(HEAD_DIM):
        sa0 = sa0 + s_ref[k] * _row_bcast(in_ref, (0, 0, Q_KK), k)

    def step(t, sa):
        vv = in_ref[0, t, Q_V]
        t_next = jnp.minimum(t + 1, tc - 1)
        y = jnp.zeros((HEAD_DIM, LANES), f32)
        sa_next = jnp.zeros((HEAD_DIM, LANES), f32)
        for k in range(HEAD_DIM):
            s_new = (s_ref[k] * _row_bcast(in_ref, (0, t, Q_W), k)
                     - sa * _row_bcast(in_ref, (0, t, Q_B), k)
                     + vv * _row_bcast(in_ref, (0, t, Q_K), k))
            s_ref[k] = s_new
            y = y + s_new * _row_bcast(in_ref, (0, t, Q_R), k)
            sa_next = sa_next + s_new * _row_bcast(in_ref, (0, t_next, Q_KK), k)
        mean = jnp.mean(y, axis=0, keepdims=True)
        d = y - mean
        var = jnp.mean(d * d, axis=0, keepdims=True)
        yn = d * lax.rsqrt(var + GN_EPS) * lnw + lnb
        rkk = jnp.sum(in_ref[0, t, Q_R] * in_ref[0, t, Q_K] * rk, axis=0, keepdims=True)
        o = yn + rkk * vv
        if natural_out:
            obuf_ref[t] = o
        else:
            o_ref[0, t] = o
        return sa_next

    lax.fori_loop(0, tc, step, sa0)

    if natural_out:
        nb = LANES // RWKV_HEADS
        for j in range(tc // 8):
            per_h = _chunk_transpose([obuf_ref[j * 8 + t] for t in range(8)], nb)
            ot = jnp.concatenate(per_h, axis=0)
            o_ref[j * 8:(j + 1) * 8] = ot.T.reshape(8, nb, RWKV_WIDTH)

    @pl.when(c == pl.num_programs(1) - 1)
    def _():
        sout_ref[0] = s_ref[...]


def _wkv_scan(scan_in, s0, rk_t, lnw_t, lnb_t, *, tc, natural_out):
    groups, T = scan_in.shape[:2]
    tile = pl.BlockSpec((1, HEAD_DIM, LANES), lambda g, c: (g, 0, 0))
    state = pl.BlockSpec((1, HEAD_DIM, HEAD_DIM, LANES), lambda g, c: (g, 0, 0, 0))
    if natural_out:
        nb = LANES // RWKV_HEADS
        o_spec = pl.BlockSpec((tc, nb, RWKV_WIDTH), lambda g, c: (c, 0, 0))
        o_shape = jax.ShapeDtypeStruct((T, nb, RWKV_WIDTH), f32)
        scratch = [pltpu.VMEM((HEAD_DIM, HEAD_DIM, LANES), f32), pltpu.VMEM((tc, HEAD_DIM, LANES), f32)]
    else:
        o_spec = pl.BlockSpec((1, tc, HEAD_DIM, LANES), lambda g, c: (g, c, 0, 0))
        o_shape = jax.ShapeDtypeStruct((groups, T, HEAD_DIM, LANES), f32)
        scratch = [pltpu.VMEM((HEAD_DIM, HEAD_DIM, LANES), f32)]
    kern = functools.partial(_wkv_scan_kernel, tc=tc, natural_out=natural_out)
    return pl.pallas_call(
        kern,
        grid=(groups, T // tc),
        in_specs=[
            pl.BlockSpec((1, tc, NQ, HEAD_DIM, LANES), lambda g, c: (g, c, 0, 0, 0)),
            state, tile, tile, tile,
        ],
        out_specs=[o_spec, state],
        out_shape=[o_shape, jax.ShapeDtypeStruct((groups, HEAD_DIM, HEAD_DIM, LANES), f32)],
        scratch_shapes=scratch,
        compiler_params=_cparams(2),
        name="wkv_scan",
    )(scan_in, s0, rk_t, lnw_t, lnb_t)


def _swa_attn_kernel(q_ref, kp_ref, kc_ref, vp_ref, vc_ref, sga_ref, sink_ref, o_ref, *,
                     bb, tq, first_block_has_no_prev):
    n = pl.program_id(1)
    nr = bb * tq
    half = LANES // 2
    tq_bits = tq.bit_length() - 1

    def key_mask(n_keys_per_batch, is_prev):
        rows = lax.broadcasted_iota(jnp.int32, (nr, bb * n_keys_per_batch), 0)
        cols = lax.broadcasted_iota(jnp.int32, (nr, bb * n_keys_per_batch), 1)
        i = rows & (tq - 1)
        j = cols & (n_keys_per_batch - 1)
        ok = (j > i) if is_prev else (j <= i)
        if bb > 1:
            same = (rows >> tq_bits) == (cols >> (n_keys_per_batch.bit_length() - 1))
            ok = jnp.logical_and(same, ok)
        if is_prev and first_block_has_no_prev:
            ok = jnp.logical_and(ok, n > 0)
        return jnp.tile(ok, (ATTN_GROUP, 1))

    mask_p = key_mask(WINDOW, True)
    mask_c = key_mask(tq, False)
    lane_q = lax.broadcasted_iota(jnp.int32, (nr, LANES), 1)
    lane_kp = lax.broadcasted_iota(jnp.int32, (bb * WINDOW, LANES), 1)

    kp = kp_ref[...].reshape(bb * WINDOW, LANES).astype(bf16)
    kc = kc_ref[...].reshape(nr, LANES).astype(bf16)
    vp = vp_ref[...].reshape(bb * WINDOW, LANES)
    vc = vc_ref[...].reshape(nr, LANES)

    for g in range(ATTN_KV_HEADS):
        in_g = (lambda lane: lane < half) if g == 0 else (lambda lane: lane >= half)
        qs, sinks = [], []
        for hh in range(ATTN_GROUP):
            h = g * ATTN_GROUP + hh
            x = q_ref[:, :, (h // 2) * LANES:(h // 2 + 1) * LANES].reshape(nr, LANES)
            if h % 2 != g:
                x = pltpu.roll(x, half, 1)
            qs.append(jnp.where(in_g(lane_q), x, 0.0).astype(bf16))
            sinks.append(jnp.broadcast_to(sink_ref[h:h + 1, :], (nr, LANES)))
        qg = jnp.concatenate(qs, axis=0)
        sink = jnp.concatenate(sinks, axis=0)[:, 0:1]
        sp = jnp.where(mask_p, _dot_nt(qg, kp), NEG_INF)
        sc = jnp.where(mask_c, _dot_nt(qg, kc), NEG_INF)
        m = jnp.maximum(jnp.maximum(jnp.max(sp, axis=-1, keepdims=True),
                                    jnp.max(sc, axis=-1, keepdims=True)), sink)
        pp = jnp.exp(sp - m).astype(bf16)
        pc = jnp.exp(sc - m).astype(bf16)
        e_sink = jnp.exp(sink - m)
        vpg = jnp.where(in_g(lane_kp), vp, 1.0).astype(bf16)
        vcg = jnp.where(in_g(lane_q), vc, 1.0).astype(bf16)
        pv = _dot(pp, vpg) + _dot(pc, vcg)
        swapped = pltpu.roll(pv, half, 1)
        for pair in range(ATTN_GROUP // 2):
            halves = []
            for hh in (2 * pair, 2 * pair + 1):
                blk = slice(hh * nr, (hh + 1) * nr)
                num, den = (pv[blk], swapped[blk]) if hh % 2 == g else (swapped[blk], pv[blk])
                halves.append(num * (1.0 / (den + e_sink[blk])))
            col = g * (ATTN_GROUP // 2) + pair
            cs = slice(col * LANES, (col + 1) * LANES)
            out = jnp.where(lane_q < half, halves[0], halves[1])
            o_ref[:, :, cs] = out.reshape(bb, tq, LANES) * sga_ref[:, :, cs]


def _swa_prompt_kernel(q_ref, kp_ref, kc_ref, vp_ref, vc_ref, sga_ref, sink_ref, o_ref):
    n = pl.program_id(1)
    tq = WINDOW
    half = LANES // 2
    keys = lax.broadcasted_iota(jnp.int32, (2 * WINDOW, tq), 0)
    qi = lax.broadcasted_iota(jnp.int32, (2 * WINDOW, tq), 1)
    ok = jnp.logical_or(jnp.logical_and(jnp.logical_and(keys < WINDOW, keys > qi), n > 0),
                        jnp.logical_and(keys >= WINDOW, keys - WINDOW <= qi))
    mask_t = jnp.tile(ok, (1, ATTN_GROUP))
    lane_q = lax.broadcasted_iota(jnp.int32, (tq, LANES), 1)
    lane_k = lax.broadcasted_iota(jnp.int32, (2 * WINDOW, LANES), 1)

    k_all = jnp.concatenate([kp_ref[0], kc_ref[0]], axis=0).astype(bf16)
    v_all = jnp.concatenate([vp_ref[0], vc_ref[0]], axis=0)

    for g in range(ATTN_KV_HEADS):
        in_g = (lambda lane: lane < half) if g == 0 else (lambda lane: lane >= half)
        qs, sinks = [], []
        for hh in range(ATTN_GROUP):
            h = g * ATTN_GROUP + hh
            x = q_ref[0, :, (h // 2) * LANES:(h // 2 + 1) * LANES]
            if h % 2 != g:
                x = pltpu.roll(x, half, 1)
            qs.append(jnp.where(in_g(lane_q), x, 0.0).astype(bf16))
            sinks.append(sink_ref[h:h + 1, :])
        qg = jnp.concatenate(qs, axis=0)
        sink = jnp.concatenate(sinks, axis=1)
        st = jnp.where(mask_t, _dot_nt(k_all, qg), NEG_INF)
        m = jnp.maximum(jnp.max(st, axis=0, keepdims=True), sink)
        p = jnp.exp(st - m).astype(bf16)
        e_sink = jnp.exp(sink - m)
        vg = jnp.where(in_g(lane_k), v_all, 1.0).astype(bf16)
        ot = lax.dot_general(vg, p, (((0,), (0,)), ((), ())), preferred_element_type=f32)
        lo, hi = ot[:half], ot[half:]
        num, den = (lo, hi) if g == 0 else (hi, lo)
        norm_t = num * (1.0 / (den + e_sink))
        for pair in range(ATTN_GROUP // 2):
            blk_t = jnp.concatenate([norm_t[:, (2 * pair) * tq:(2 * pair + 1) * tq],
                                     norm_t[:, (2 * pair + 1) * tq:(2 * pair + 2) * tq]], axis=0)
            col = g * (ATTN_GROUP // 2) + pair
            cs = slice(col * LANES, (col + 1) * LANES)
            o_ref[0, :, cs] = blk_t.T * sga_ref[0, :, cs]


def _swa_attn(q, k_prev, k_cur, v_prev, v_cur, sga, sink_t, *, bb, tq, prev_is_same_array):
    B, T, _ = q.shape
    nq = T // tq
    cur = lambda w: pl.BlockSpec((bb, tq, w), lambda bi, n: (bi, n, 0))
    if prev_is_same_array:
        prev = pl.BlockSpec((bb, WINDOW, KV_WIDTH), lambda bi, n: (bi, jnp.maximum(n - 1, 0), 0))
    else:
        prev = pl.BlockSpec((bb, WINDOW, KV_WIDTH), lambda bi, n: (bi, 0, 0))
    if prev_is_same_array:
        assert bb == 1 and tq == WINDOW
        kern = _swa_prompt_kernel
    else:
        kern = functools.partial(_swa_attn_kernel, bb=bb, tq=tq, first_block_has_no_prev=False)
    return pl.pallas_call(
        kern,
        grid=(B // bb, nq),
        in_specs=[cur(ATTN_WIDTH), prev, cur(KV_WIDTH), prev, cur(KV_WIDTH), cur(ATTN_WIDTH),
                  pl.BlockSpec((ATTN_Q_HEADS, LANES), lambda bi, n: (0, 0))],
        out_specs=cur(ATTN_WIDTH),
        out_shape=jax.ShapeDtypeStruct((B, T, ATTN_WIDTH), f32),
        compiler_params=_cparams(2),
        name="swa_attn",
    )(q, k_prev, k_cur, v_prev, v_cur, sga, sink_t)


def _out_mix_kernel(x_ref, or_ref, sgr_ref, ma_ref, p_ref, wo_ref, gp_ref, wpg_ref, wpp_ref, gf_ref,
                    y_ref):
    mr = (or_ref[...] * sgr_ref[...]).astype(bf16)
    ma = ma_ref[...].astype(bf16)
    h = (x_ref[...] + _dot(mr, wo_ref[:RWKV_WIDTH, :]) + _dot(ma, wo_ref[RWKV_WIDTH:, :]))
    gate = _sigmoid(_dot(_rmsnorm(h, gp_ref[...]).astype(bf16), wpg_ref[...]))
    h = h + gate * _dot(p_ref[...].astype(bf16), wpp_ref[...])
    y_ref[...] = _rmsnorm(h, gf_ref[...])


def _out_mix(x2d, o_r, sgr, ma, p2d, w_out, g_ple, w_pg, w_pp, g_final, *, tm):
    n_tok = x2d.shape[0]
    const = lambda shape: pl.BlockSpec(shape, lambda i: (0,) * len(shape))
    row = lambda w: pl.BlockSpec((tm, w), lambda i: (i, 0))
    return pl.pallas_call(
        _out_mix_kernel,
        grid=(n_tok // tm,),
        in_specs=[row(D_MODEL), row(RWKV_WIDTH), row(RWKV_WIDTH), row(ATTN_WIDTH), row(PLE_DIM),
                  const((D_MODEL, D_MODEL)), const((1, D_MODEL)), const((D_MODEL, D_MODEL)),
                  const((PLE_DIM, D_MODEL)), const((1, D_MODEL))],
        out_specs=row(D_MODEL),
        out_shape=jax.ShapeDtypeStruct((n_tok, D_MODEL), f32),
        compiler_params=_cparams(1),
        name="out_mix",
    )(x2d, o_r, sgr, ma, p2d, w_out, g_ple, w_pg, w_pp, g_final)


def _rope_tables(pos):
    half = ROPE_DIM // 2
    inv = ROPE_THETA ** (-jnp.arange(half, dtype=f32) / half)
    ang = pos.astype(f32)[:, None] * inv[None, :]
    cos, sin = jnp.cos(ang), jnp.sin(ang)
    n = pos.shape[0]
    ones = jnp.ones((n, HEAD_DIM - ROPE_DIM), f32)
    zeros = jnp.zeros((n, HEAD_DIM - ROPE_DIM), f32)
    zh = jnp.zeros((n, half), f32)
    c = jnp.concatenate([cos, cos, ones], axis=1)
    a = jnp.concatenate([-sin, zh, zeros], axis=1)
    b = jnp.concatenate([zh, sin, zeros], axis=1)
    rep = LANES // HEAD_DIM
    return tuple(jnp.tile(t, (1, rep)) for t in (c, a, b))


def _col_tile(vec):
    return jnp.broadcast_to(vec[:, None], (vec.shape[0], LANES))


def _lane_param_tile(vec, heads_on_lanes):
    ph = vec.reshape(RWKV_HEADS, HEAD_DIM)
    if heads_on_lanes:
        nb = LANES // RWKV_HEADS
        return jnp.repeat(ph.T, nb, axis=1)[None]
    return jnp.broadcast_to(ph[:, :, None], (RWKV_HEADS, HEAD_DIM, LANES))


def _layer(x, p, pos, s0, shift0, kbuf, vbuf, wts, *, heads_on_lanes, tt, tc, tm, att_bb, att_tq):
    B, T, _ = x.shape
    n_tok = B * T
    x2d = x.reshape(n_tok, D_MODEL)

    xt = jnp.swapaxes(x, 0, 1)
    if shift0 is None:
        shift0_t = jnp.zeros((SHIFT_DIM, LANES), f32)
    else:
        shift0_t = jnp.pad(shift0.T, ((0, 0), (LANES - B, 0)))
    scan_in, shift_t = _rwkv_proj(xt, wts["g_norm"], wts["wt_rwkv"], wts["mu_t"], wts["w0_t"],
                                  wts["a0_t"], wts["kk_t"], wts["ka_t"], wts["w2t"], wts["a2t"],
                                  shift0_t, tt=tt, heads_on_lanes=heads_on_lanes)
    shift_new = shift_t[:, LANES - B:].T if B < LANES else shift_t.T
    groups = scan_in.shape[0]
    if s0 is None:
        s0_t = jnp.zeros((groups, HEAD_DIM, HEAD_DIM, LANES), f32)
    else:
        s0_t = jnp.transpose(s0, (1, 3, 2, 0))
    rk_t = _lane_param_tile(wts["r_k"], heads_on_lanes)
    lnw_t = _lane_param_tile(wts["ln_w"], heads_on_lanes)
    lnb_t = _lane_param_tile(wts["ln_b"], heads_on_lanes)
    o_scan, s_fin = _wkv_scan(scan_in, s0_t, rk_t, lnw_t, lnb_t, tc=tc, natural_out=heads_on_lanes)
    if heads_on_lanes:
        o_r = jnp.swapaxes(o_scan, 0, 1).reshape(n_tok, RWKV_WIDTH)
        s_new = jnp.transpose(s_fin[0].reshape(HEAD_DIM, HEAD_DIM, RWKV_HEADS, B), (3, 2, 1, 0))
    else:
        o_r = jnp.transpose(o_scan, (3, 1, 0, 2)).reshape(n_tok, RWKV_WIDTH)
        s_new = jnp.transpose(s_fin, (3, 0, 2, 1))

    rope_c, rope_a, rope_b = _rope_tables(pos)
    if T < tm:
        reps = tm // T
        rope_c, rope_a, rope_b = (jnp.tile(t, (reps, 1)) for t in (rope_c, rope_a, rope_b))
    sgr, q, k, v, sga = _nat_proj(x2d, wts["g_norm"], wts["w_nat"], rope_c, rope_a, rope_b, tm=tm)
    q3 = q.reshape(B, T, ATTN_WIDTH)
    k3 = k.reshape(B, T, KV_WIDTH)
    v3 = v.reshape(B, T, KV_WIDTH)
    sga3 = sga.reshape(B, T, ATTN_WIDTH)
    if kbuf is None:
        ma = _swa_attn(q3, k3, k3, v3, v3, sga3, wts["sink_t"], bb=att_bb, tq=att_tq,
                       prev_is_same_array=True)
        k_new = k3[:, T - WINDOW:].reshape(B, WINDOW, ATTN_KV_HEADS, HEAD_DIM)
        v_new = v3[:, T - WINDOW:].reshape(B, WINDOW, ATTN_KV_HEADS, HEAD_DIM)
    else:
        kb = kbuf.reshape(B, WINDOW, KV_WIDTH)
        vb = vbuf.reshape(B, WINDOW, KV_WIDTH)
        ma = _swa_attn(q3, kb, k3, vb, v3, sga3, wts["sink_t"], bb=att_bb, tq=att_tq,
                       prev_is_same_array=False)
        k_new = jnp.concatenate([kb, k3], axis=1)[:, -WINDOW:].reshape(B, WINDOW, ATTN_KV_HEADS, HEAD_DIM)
        v_new = jnp.concatenate([vb, v3], axis=1)[:, -WINDOW:].reshape(B, WINDOW, ATTN_KV_HEADS, HEAD_DIM)

    y = _out_mix(x2d, o_r, sgr, ma.reshape(n_tok, ATTN_WIDTH), p.reshape(n_tok, PLE_DIM),
                 wts["w_out"], wts["g_ple"], wts["w_pg"], wts["w_pp"], wts["g_final"], tm=tm)
    return y.reshape(B, T, D_MODEL), s_new, shift_new, k_new, v_new


def kernel(x_prompt, x_sample, state_rwkv_wkv, state_rwkv_shift, cache_swa_k, cache_swa_v,
           p_prompt, p_sample, g_norm, w_in, mu_shift, w0, w2, a0, a2, k_k, k_a, r_k,
           ln_w, ln_b, sinks, w_out, g_ple, w_ple_gate, w_ple_proj, g_final):
    assert w_in.shape[0] == 1, "single layer"
    w_in0 = w_in[0]
    wts = {
        "g_norm": g_norm[0][None, :],
        "wt_rwkv": w_in0[:, :SHIFT_DIM].T.astype(bf16),
        "w_nat": w_in0[:, SHIFT_DIM:].astype(bf16),
        "mu_t": _col_tile(mu_shift[0]),
        "w0_t": _col_tile(w0[0]), "a0_t": _col_tile(a0[0]),
        "kk_t": _col_tile(k_k[0]), "ka_t": _col_tile(k_a[0]),
        "w2t": w2[0].T.astype(bf16), "a2t": a2[0].T.astype(bf16),
        "r_k": r_k[0], "ln_w": ln_w[0], "ln_b": ln_b[0],
        "sink_t": jnp.broadcast_to(sinks[0][:, None], (ATTN_Q_HEADS, LANES)),
        "w_out": w_out[0].astype(bf16), "g_ple": g_ple[0][None, :],
        "w_pg": w_ple_gate[0].astype(bf16), "w_pp": w_ple_proj[0].astype(bf16),
        "g_final": g_final[None, :],
    }
    Bp, Tp, _ = x_prompt.shape
    Bs, Ts, _ = x_sample.shape
    assert Bp * RWKV_HEADS == LANES and Bs == LANES and Tp % WINDOW == 0 and Ts % SUBLANES == 0

    yp, s1, sh1, k1, v1 = _layer(x_prompt, p_prompt[0], jnp.arange(Tp), None, None, None, None, wts,
                                 heads_on_lanes=True, tt=16, tc=32, tm=512, att_bb=1, att_tq=WINDOW)
    ys, s2, sh2, k2, v2 = _layer(x_sample, p_sample[0], PAST_LEN + jnp.arange(Ts),
                                 state_rwkv_wkv[0], state_rwkv_shift[0], cache_swa_k[0], cache_swa_v[0],
                                 wts, heads_on_lanes=False, tt=2, tc=Ts, tm=Bs * Ts, att_bb=8, att_tq=Ts)
    return (yp, ys, s1[None], sh1[None], k1[None], v1[None], s2[None], sh2[None], k2[None], v2[None])
```

```python
import functools
import math

import jax
import jax.numpy as jnp
from jax import lax
from jax.experimental import pallas as pl
from jax.experimental.pallas import tpu as pltpu

D_MODEL = 1024
HEAD_DIM = 64
RWKV_WIDTH = 512
RWKV_HEADS = 8
ATTN_WIDTH = 512
ATTN_Q_HEADS = 8
ATTN_KV_HEADS = 2
ATTN_GROUP = 4
KV_WIDTH = 128
LORA = 64
WINDOW = 128
ROPE_THETA = 500000.0
ROPE_DIM = 16
PLE_DIM = 256
NORM_EPS = 1e-6
GN_EPS = 64e-5
NEG_INF = -1e30
PAST_LEN = 16384
SHIFT_DIM = 3 * RWKV_WIDTH + 2 * LORA
NAT_DIM = RWKV_WIDTH + ATTN_WIDTH + 2 * KV_WIDTH + ATTN_WIDTH

LANES = 128
SUBLANES = 8
VMEM_LIMIT = 48 * 1024 * 1024
DECAY_SCALE = math.exp(-0.5)
GROUP_T = 8

Q_KK, Q_W, Q_B, Q_K, Q_R, Q_V = range(6)
NQ = 6

f32 = jnp.float32
bf16 = jnp.bfloat16


def _cparams(n_axes):
    return pltpu.CompilerParams(dimension_semantics=("arbitrary",) * n_axes,
                                vmem_limit_bytes=VMEM_LIMIT)


def _rmsnorm(x, g):
    ms = jnp.mean(x * x, axis=-1, keepdims=True)
    return x * lax.rsqrt(ms + NORM_EPS) * g


def _sigmoid(x):
    return 1.0 / (1.0 + jnp.exp(-x))


def _dot_nt(a, b):
    return lax.dot_general(a, b, (((1,), (1,)), ((), ())), preferred_element_type=f32)


def _dot(a, b):
    return jnp.dot(a, b, preferred_element_type=f32)


def _chunk_transpose(xs, chunk):
    lane = lax.broadcasted_iota(jnp.int32, xs[0].shape, 1)
    xs = list(xs)
    for d in (4, 2, 1):
        hi_lanes = (lane & (chunk * d)) != 0
        nxt = list(xs)
        for i in range(8):
            if i & d:
                continue
            lo, hi = xs[i], xs[i + d]
            nxt[i] = jnp.where(hi_lanes, pltpu.roll(hi, chunk * d, 1), lo)
            nxt[i + d] = jnp.where(hi_lanes, hi, pltpu.roll(lo, LANES - chunk * d, 1))
        xs = nxt
    return xs


def _rwkv_proj_kernel(x_ref, g_ref, wt_ref, mu_ref, w0_ref, a0_ref, kk_ref, ka_ref, w2t_ref,
                      a2t_ref, shift0_ref, out_ref, shift_ref, carry_ref, *rest, tt, nb, heads_on_lanes):
    i = pl.program_id(0)

    @pl.when(i == 0)
    def _():
        carry_ref[...] = shift0_ref[...]

    n = tt * nb
    if heads_on_lanes:
        u_ref = rest[0]
        n_col = D_MODEL // LANES
        for bi in range(nb):
            ub = _rmsnorm(x_ref[bi], g_ref[...])
            for ci in range(n_col):
                u_ref[ci, pl.ds(bi, tt, stride=nb), :] = ub[:, ci * LANES:(ci + 1) * LANES]
        u = jnp.concatenate([u_ref[ci] for ci in range(n_col)], axis=1).astype(bf16)
    else:
        u = _rmsnorm(x_ref[...].reshape(n, D_MODEL), g_ref[...]).astype(bf16)
    zt = _dot_nt(wt_ref[...], u)

    mu = mu_ref[...]
    lane = lax.broadcasted_iota(jnp.int32, (SHIFT_DIM, LANES), 1)
    nblk = n // LANES
    prev_rot = carry_ref[...]
    for j in range(nblk):
        z = zt[:, j * LANES:(j + 1) * LANES]
        if nb % LANES == 0:
            prev, z_rot = prev_rot, z
        else:
            z_rot = pltpu.roll(z, nb, 1)
            prev = jnp.where(lane < nb, prev_rot, z_rot)
        prev_rot = z_rot
        zs = z + mu * (prev - z)
        r = zs[0:RWKV_WIDTH]
        kx = zs[RWKV_WIDTH:2 * RWKV_WIDTH]
        v = zs[2 * RWKV_WIDTH:3 * RWKV_WIDTH]
        wd = zs[3 * RWKV_WIDTH:3 * RWKV_WIDTH + LORA]
        ad = zs[3 * RWKV_WIDTH + LORA:]
        w_pre = w0_ref[...] + _dot(w2t_ref[...], jnp.tanh(wd).astype(bf16))
        decay = jnp.exp(-DECAY_SCALE * _sigmoid(w_pre))
        a = _sigmoid(a0_ref[...] + _dot(a2t_ref[...], ad.astype(bf16)))
        kkr = kx * kk_ref[...]
        ss = jnp.sum((kkr * kkr).reshape(RWKV_HEADS, HEAD_DIM, LANES), axis=1, keepdims=True)
        inv = 1.0 / jnp.maximum(jnp.sqrt(ss), 1e-12)
        kk = (kkr.reshape(RWKV_HEADS, HEAD_DIM, LANES) * inv).reshape(RWKV_WIDTH, LANES)
        k = kx * (1.0 + (a - 1.0) * ka_ref[...])
        b = kk * a
        quantities = {Q_KK: kk, Q_W: decay, Q_B: b, Q_K: k, Q_R: r, Q_V: v}
        for q, val in quantities.items():
            per_head = [val[h * HEAD_DIM:(h + 1) * HEAD_DIM] for h in range(RWKV_HEADS)]
            if heads_on_lanes:
                per_t = _chunk_transpose(per_head, nb)
                for t in range(GROUP_T):
                    out_ref[0, j * GROUP_T + t, q] = per_t[t]
            else:
                for h in range(RWKV_HEADS):
                    out_ref[h, j, q] = per_head[h]
    carry_ref[...] = prev_rot
    shift_ref[...] = z


def _rwkv_proj(x, g_norm, wt, mu_t, w0_t, a0_t, kk_t, ka_t, w2t, a2t, shift0_t, *, tt, heads_on_lanes):
    const = lambda shape: pl.BlockSpec(shape, lambda i: (0,) * len(shape))
    if heads_on_lanes:
        nb, T, _ = x.shape
        x_spec = pl.BlockSpec((nb, tt, D_MODEL), lambda i: (0, i, 0))
        o_spec = pl.BlockSpec((1, tt, NQ, HEAD_DIM, LANES), lambda i: (0, i, 0, 0, 0))
        o_shape = jax.ShapeDtypeStruct((1, T, NQ, HEAD_DIM, LANES), f32)
        scratch = [pltpu.VMEM((SHIFT_DIM, LANES), f32), pltpu.VMEM((D_MODEL // LANES, tt * nb, LANES), f32)]
    else:
        T, nb, _ = x.shape
        x_spec = pl.BlockSpec((tt, nb, D_MODEL), lambda i: (i, 0, 0))
        o_spec = pl.BlockSpec((RWKV_HEADS, tt, NQ, HEAD_DIM, LANES), lambda i: (0, i, 0, 0, 0))
        o_shape = jax.ShapeDtypeStruct((RWKV_HEADS, T, NQ, HEAD_DIM, LANES), f32)
        scratch = [pltpu.VMEM((SHIFT_DIM, LANES), f32)]
    kern = functools.partial(_rwkv_proj_kernel, tt=tt, nb=nb, heads_on_lanes=heads_on_lanes)
    return pl.pallas_call(
        kern,
        grid=(T // tt,),
        in_specs=[
            x_spec,
            const((1, D_MODEL)),
            const((SHIFT_DIM, D_MODEL)),
            const((SHIFT_DIM, LANES)),
            const((RWKV_WIDTH, LANES)), const((RWKV_WIDTH, LANES)),
            const((RWKV_WIDTH, LANES)), const((RWKV_WIDTH, LANES)),
            const((RWKV_WIDTH, LORA)), const((RWKV_WIDTH, LORA)),
            const((SHIFT_DIM, LANES)),
        ],
        out_specs=[o_spec, const((SHIFT_DIM, LANES))],
        out_shape=[o_shape, jax.ShapeDtypeStruct((SHIFT_DIM, LANES), f32)],
        scratch_shapes=scratch,
        compiler_params=_cparams(1),
        name="rwkv_proj",
    )(x, g_norm, wt, mu_t, w0_t, a0_t, kk_t, ka_t, w2t, a2t, shift0_t)


def _rope(x, c, a, b):
    return x * c + pltpu.roll(x, LANES - ROPE_DIM // 2, 1) * a + pltpu.roll(x, ROPE_DIM // 2, 1) * b


def _nat_proj_kernel(x_ref, g_ref, w_ref, rc_ref, ra_ref, rb_ref,
                     sgr_ref, q_ref, k_ref, v_ref, sga_ref):
    u = _rmsnorm(x_ref[...], g_ref[...]).astype(bf16)
    z = _dot(u, w_ref[...])
    o_q = RWKV_WIDTH
    o_k = o_q + ATTN_WIDTH
    o_v = o_k + KV_WIDTH
    o_g = o_v + KV_WIDTH
    gr = z[:, :o_q]
    sgr_ref[...] = gr * _sigmoid(gr)
    rc, ra, rb = rc_ref[...], ra_ref[...], rb_ref[...]
    for j in range(ATTN_WIDTH // LANES):
        qj = z[:, o_q + j * LANES:o_q + (j + 1) * LANES]
        q_ref[:, j * LANES:(j + 1) * LANES] = (_rope(qj, rc, ra, rb) * (HEAD_DIM ** -0.5)).astype(q_ref.dtype)
    k_ref[...] = _rope(z[:, o_k:o_v], rc, ra, rb)
    v_ref[...] = z[:, o_v:o_g]
    ga = z[:, o_g:]
    sga_ref[...] = ga * _sigmoid(ga)


def _nat_proj(x2d, g_norm, w_nat, rope_c, rope_a, rope_b, *, tm, q_dtype):
    n_tok = x2d.shape[0]
    n_tab = rope_c.shape[0] // tm
    const = lambda shape: pl.BlockSpec(shape, lambda i: (0,) * len(shape))
    row = lambda w: pl.BlockSpec((tm, w), lambda i: (i, 0))
    tab = pl.BlockSpec((tm, LANES), lambda i: (i % n_tab, 0))
    return pl.pallas_call(
        _nat_proj_kernel,
        grid=(n_tok // tm,),
        in_specs=[row(D_MODEL), const((1, D_MODEL)), const((D_MODEL, NAT_DIM)), tab, tab, tab],
        out_specs=[row(RWKV_WIDTH), row(ATTN_WIDTH), row(KV_WIDTH), row(KV_WIDTH), row(ATTN_WIDTH)],
        out_shape=[jax.ShapeDtypeStruct((n_tok, w), dt) for w, dt in
                   ((RWKV_WIDTH, f32), (ATTN_WIDTH, q_dtype), (KV_WIDTH, f32), (KV_WIDTH, f32),
                    (ATTN_WIDTH, f32))],
        compiler_params=_cparams(1),
        name="nat_proj",
    )(x2d, g_norm, w_nat, rope_c, rope_a, rope_b)


def _row_bcast(ref, idx, k):
    return jnp.broadcast_to(ref[idx + (pl.ds(k, 1), slice(None))], (HEAD_DIM, LANES))


def _wkv_step(s_ref, ref, at, ref_next, at_next, sa, rk, lnw, lnb):
    vv = ref[at + (Q_V,)]
    y = jnp.zeros((HEAD_DIM, LANES), f32)
    sa_next = jnp.zeros((HEAD_DIM, LANES), f32)
    for k in range(HEAD_DIM):
        s_new = (s_ref[k] * _row_bcast(ref, at + (Q_W,), k)
                 - sa * _row_bcast(ref, at + (Q_B,), k)
                 + vv * _row_bcast(ref, at + (Q_K,), k))
        s_ref[k] = s_new
        y = y + s_new * _row_bcast(ref, at + (Q_R,), k)
        sa_next = sa_next + s_new * _row_bcast(ref_next, at_next + (Q_KK,), k)
    mean = jnp.mean(y, axis=0, keepdims=True)
    d = y - mean
    var = jnp.mean(d * d, axis=0, keepdims=True)
    yn = d * lax.rsqrt(var + GN_EPS) * lnw + lnb
    rkk = jnp.sum(ref[at + (Q_R,)] * ref[at + (Q_K,)] * rk, axis=0, keepdims=True)
    return yn + rkk * vv, sa_next


def _wkv_scan_kernel(in_ref, s0_ref, rk_ref, lnw_ref, lnb_ref, o_ref, sout_ref, s_ref, *rest,
                     tc, natural_out):
    c = pl.program_id(1)

    @pl.when(c == 0)
    def _():
        s_ref[...] = s0_ref[0]

    obuf_ref = rest[0] if natural_out else None
    rk, lnw, lnb = rk_ref[0], lnw_ref[0], lnb_ref[0]
    sa0 = jnp.zeros((HEAD_DIM, LANES), f32)
    for k in range(HEAD_DIM):
        sa0 = sa0 + s_ref[k] * _row_bcast(in_ref, (0, 0, Q_KK), k)

    def step(t, sa):
        t_next = jnp.minimum(t + 1, tc - 1)
        o, sa_next = _wkv_step(s_ref, in_ref, (0, t), in_ref, (0, t_next), sa, rk, lnw, lnb)
        if natural_out:
            obuf_ref[t] = o
        else:
            o_ref[0, t] = o
        return sa_next

    lax.fori_loop(0, tc, step, sa0)

    if natural_out:
        nb = LANES // RWKV_HEADS
        for j in range(tc // GROUP_T):
            per_h = _chunk_transpose([obuf_ref[j * GROUP_T + t] for t in range(GROUP_T)], nb)
            ot = jnp.concatenate(per_h, axis=0)
            o_ref[j * GROUP_T:(j + 1) * GROUP_T] = ot.T.reshape(GROUP_T, nb, RWKV_WIDTH)

    @pl.when(c == pl.num_programs(1) - 1)
    def _():
        sout_ref[0] = s_ref[...]


def _wkv_scan(scan_in, s0, rk_t, lnw_t, lnb_t, *, tc, natural_out):
    groups, T = scan_in.shape[:2]
    tile = pl.BlockSpec((1, HEAD_DIM, LANES), lambda g, c: (g, 0, 0))
    state = pl.BlockSpec((1, HEAD_DIM, HEAD_DIM, LANES), lambda g, c: (g, 0, 0, 0))
    scratch = [pltpu.VMEM((HEAD_DIM, HEAD_DIM, LANES), f32)]
    if natural_out:
        nb = LANES // RWKV_HEADS
        o_spec = pl.BlockSpec((tc, nb, RWKV_WIDTH), lambda g, c: (c, 0, 0))
        o_shape = jax.ShapeDtypeStruct((T, nb, RWKV_WIDTH), f32)
        scratch.append(pltpu.VMEM((tc, HEAD_DIM, LANES), f32))
    else:
        o_spec = pl.BlockSpec((1, tc, HEAD_DIM, LANES), lambda g, c: (g, c, 0, 0))
        o_shape = jax.ShapeDtypeStruct((groups, T, HEAD_DIM, LANES), f32)
    return pl.pallas_call(
        functools.partial(_wkv_scan_kernel, tc=tc, natural_out=natural_out),
        grid=(groups, T // tc),
        in_specs=[
            pl.BlockSpec((1, tc, NQ, HEAD_DIM, LANES), lambda g, c: (g, c, 0, 0, 0)),
            state, tile, tile, tile,
        ],
        out_specs=[o_spec, state],
        out_shape=[o_shape, jax.ShapeDtypeStruct((groups, HEAD_DIM, HEAD_DIM, LANES), f32)],
        scratch_shapes=scratch,
        compiler_params=_cparams(2),
        name="wkv_scan",
    )(scan_in, s0, rk_t, lnw_t, lnb_t)


def _swa_attn_kernel(q_ref, kp_ref, kc_ref, vp_ref, vc_ref, sga_ref, sink_ref, o_ref, *,
                     bb, tq, first_block_has_no_prev):
    n = pl.program_id(1)
    nr = bb * tq
    half = LANES // 2
    tq_bits = tq.bit_length() - 1

    def key_mask(n_keys_per_batch, is_prev):
        rows = lax.broadcasted_iota(jnp.int32, (nr, bb * n_keys_per_batch), 0)
        cols = lax.broadcasted_iota(jnp.int32, (nr, bb * n_keys_per_batch), 1)
        i = rows & (tq - 1)
        j = cols & (n_keys_per_batch - 1)
        ok = (j > i) if is_prev else (j <= i)
        if bb > 1:
            same = (rows >> tq_bits) == (cols >> (n_keys_per_batch.bit_length() - 1))
            ok = jnp.logical_and(same, ok)
        if is_prev and first_block_has_no_prev:
            ok = jnp.logical_and(ok, n > 0)
        return jnp.tile(ok, (ATTN_GROUP, 1))

    mask_p = key_mask(WINDOW, True)
    mask_c = key_mask(tq, False)
    lane_q = lax.broadcasted_iota(jnp.int32, (nr, LANES), 1)
    lane_kp = lax.broadcasted_iota(jnp.int32, (bb * WINDOW, LANES), 1)

    kp = kp_ref[...].reshape(bb * WINDOW, LANES).astype(bf16)
    kc = kc_ref[...].reshape(nr, LANES).astype(bf16)
    vp = vp_ref[...].reshape(bb * WINDOW, LANES)
    vc = vc_ref[...].reshape(nr, LANES)

    res = []
    for g in range(ATTN_KV_HEADS):
        in_g = (lambda lane: lane < half) if g == 0 else (lambda lane: lane >= half)
        qs, sinks = [], []
        for hh in range(ATTN_GROUP):
            x = q_ref[:, :, hh * LANES:(hh + 1) * LANES].reshape(nr, LANES)
            qs.append(jnp.where(in_g(lane_q), x, 0.0).astype(bf16))
            sinks.append(jnp.broadcast_to(sink_ref[g * ATTN_GROUP + hh:g * ATTN_GROUP + hh + 1, :], (nr, LANES)))
        qg = jnp.concatenate(qs, axis=0)
        sink = jnp.concatenate(sinks, axis=0)[:, 0:1]
        sp = jnp.where(mask_p, _dot_nt(qg, kp), NEG_INF)
        sc = jnp.where(mask_c, _dot_nt(qg, kc), NEG_INF)
        m = jnp.maximum(jnp.maximum(jnp.max(sp, axis=-1, keepdims=True),
                                    jnp.max(sc, axis=-1, keepdims=True)), sink)
        pp = jnp.exp(sp - m).astype(bf16)
        pc = jnp.exp(sc - m).astype(bf16)
        e_sink = jnp.exp(sink - m)
        vpg = jnp.where(in_g(lane_kp), vp, 1.0).astype(bf16)
        vcg = jnp.where(in_g(lane_q), vc, 1.0).astype(bf16)
        pv = _dot(pp, vpg) + _dot(pc, vcg)
        res.append(pv * (1.0 / (pltpu.roll(pv, half, 1) + e_sink)))
    for hh in range(ATTN_GROUP):
        blk = slice(hh * nr, (hh + 1) * nr)
        cs = slice(hh * LANES, (hh + 1) * LANES)
        out = jnp.where(lane_q < half, res[0][blk], res[1][blk])
        o_ref[:, :, cs] = (out.reshape(bb, tq, LANES) * sga_ref[:, :, cs]).astype(o_ref.dtype)


def _swa_prompt_kernel(q_ref, kp_ref, kc_ref, vp_ref, vc_ref, sga_ref, sink_ref, o_ref):
    n = pl.program_id(1)
    tq = WINDOW
    half = LANES // 2
    keys = lax.broadcasted_iota(jnp.int32, (2 * WINDOW, tq), 0)
    qi = lax.broadcasted_iota(jnp.int32, (2 * WINDOW, tq), 1)
    ok = jnp.logical_or(jnp.logical_and(jnp.logical_and(keys < WINDOW, keys > qi), n > 0),
                        jnp.logical_and(keys >= WINDOW, keys - WINDOW <= qi))
    mask_t = jnp.tile(ok, (1, ATTN_GROUP))
    lane_q = lax.broadcasted_iota(jnp.int32, (tq, LANES), 1)
    lane_k = lax.broadcasted_iota(jnp.int32, (2 * WINDOW, LANES), 1)

    k_all = jnp.concatenate([kp_ref[0], kc_ref[0]], axis=0).astype(bf16)
    v_all = jnp.concatenate([vp_ref[0], vc_ref[0]], axis=0)

    norm_t = []
    for g in range(ATTN_KV_HEADS):
        in_g = (lambda lane: lane < half) if g == 0 else (lambda lane: lane >= half)
        qs, sinks = [], []
        for hh in range(ATTN_GROUP):
            x = q_ref[0, :, hh * LANES:(hh + 1) * LANES]
            qs.append(jnp.where(in_g(lane_q), x, jnp.zeros_like(x)).astype(bf16))
            sinks.append(sink_ref[g * ATTN_GROUP + hh:g * ATTN_GROUP + hh + 1, :])
        qg = jnp.concatenate(qs, axis=0)
        sink = jnp.concatenate(sinks, axis=1)
        st = jnp.where(mask_t, _dot_nt(k_all, qg), NEG_INF)
        m = jnp.maximum(jnp.max(st, axis=0, keepdims=True), sink)
        p = jnp.exp(st - m).astype(bf16)
        e_sink = jnp.exp(sink - m)
        vg = jnp.where(in_g(lane_k), v_all, 1.0).astype(bf16)
        ot = lax.dot_general(vg, p, (((0,), (0,)), ((), ())), preferred_element_type=f32)
        lo, hi = ot[:half], ot[half:]
        num, den = (lo, hi) if g == 0 else (hi, lo)
        norm_t.append(num * (1.0 / (den + e_sink)))
    for hh in range(ATTN_GROUP):
        cs = slice(hh * LANES, (hh + 1) * LANES)
        blk_t = jnp.concatenate([norm_t[0][:, hh * tq:(hh + 1) * tq],
                                 norm_t[1][:, hh * tq:(hh + 1) * tq]], axis=0)
        o_ref[0, :, cs] = (blk_t.T * sga_ref[0, :, cs]).astype(o_ref.dtype)


def _swa_attn(q, k_prev, k_cur, v_prev, v_cur, sga, sink_t, *, bb, tq, prev_is_same_array, out_dtype):
    B, T, _ = q.shape
    nq = T // tq
    cur = lambda w: pl.BlockSpec((bb, tq, w), lambda bi, n: (bi, n, 0))
    if prev_is_same_array:
        prev = pl.BlockSpec((bb, WINDOW, KV_WIDTH), lambda bi, n: (bi, jnp.maximum(n - 1, 0), 0))
    else:
        prev = pl.BlockSpec((bb, WINDOW, KV_WIDTH), lambda bi, n: (bi, 0, 0))
    if prev_is_same_array:
        assert bb == 1 and tq == WINDOW
        kern = _swa_prompt_kernel
    else:
        kern = functools.partial(_swa_attn_kernel, bb=bb, tq=tq, first_block_has_no_prev=False)
    return pl.pallas_call(
        kern,
        grid=(B // bb, nq),
        in_specs=[cur(ATTN_WIDTH), prev, cur(KV_WIDTH), prev, cur(KV_WIDTH), cur(ATTN_WIDTH),
                  pl.BlockSpec((ATTN_Q_HEADS, LANES), lambda bi, n: (0, 0))],
        out_specs=cur(ATTN_WIDTH),
        out_shape=jax.ShapeDtypeStruct((B, T, ATTN_WIDTH), out_dtype),
        compiler_params=_cparams(2),
        name="swa_attn",
    )(q, k_prev, k_cur, v_prev, v_cur, sga, sink_t)


def _out_mix_kernel(x_ref, or_ref, sgr_ref, ma_ref, p_ref, wo_ref, gp_ref, wpg_ref, wpp_ref, gf_ref,
                    y_ref):
    mr = (or_ref[...] * sgr_ref[...]).astype(bf16)
    ma = ma_ref[...].astype(bf16)
    h = (x_ref[...] + _dot(mr, wo_ref[:RWKV_WIDTH, :]) + _dot(ma, wo_ref[RWKV_WIDTH:, :]))
    gate = _sigmoid(_dot(_rmsnorm(h, gp_ref[...]).astype(bf16), wpg_ref[...]))
    h = h + gate * _dot(p_ref[...].astype(bf16), wpp_ref[...])
    y_ref[...] = _rmsnorm(h, gf_ref[...])


def _out_mix(x2d, o_r, sgr, ma, p2d, w_out, g_ple, w_pg, w_pp, g_final, *, tm):
    n_tok = x2d.shape[0]
    const = lambda shape: pl.BlockSpec(shape, lambda i: (0,) * len(shape))
    row = lambda w: pl.BlockSpec((tm, w), lambda i: (i, 0))
    return pl.pallas_call(
        _out_mix_kernel,
        grid=(n_tok // tm,),
        in_specs=[row(D_MODEL), row(RWKV_WIDTH), row(RWKV_WIDTH), row(ATTN_WIDTH), row(PLE_DIM),
                  const((D_MODEL, D_MODEL)), const((1, D_MODEL)), const((D_MODEL, D_MODEL)),
                  const((PLE_DIM, D_MODEL)), const((1, D_MODEL))],
        out_specs=row(D_MODEL),
        out_shape=jax.ShapeDtypeStruct((n_tok, D_MODEL), f32),
        compiler_params=_cparams(1),
        name="out_mix",
    )(x2d, o_r, sgr, ma, p2d, w_out, g_ple, w_pg, w_pp, g_final)


def _rope_tables(pos):
    half = ROPE_DIM // 2
    inv = ROPE_THETA ** (-jnp.arange(half, dtype=f32) / half)
    ang = pos.astype(f32)[:, None] * inv[None, :]
    cos, sin = jnp.cos(ang), jnp.sin(ang)
    n = pos.shape[0]
    ones = jnp.ones((n, HEAD_DIM - ROPE_DIM), f32)
    zeros = jnp.zeros((n, HEAD_DIM - ROPE_DIM), f32)
    zh = jnp.zeros((n, half), f32)
    c = jnp.concatenate([cos, cos, ones], axis=1)
    a = jnp.concatenate([-sin, zh, zeros], axis=1)
    b = jnp.concatenate([zh, sin, zeros], axis=1)
    rep = LANES // HEAD_DIM
    return tuple(jnp.tile(t, (1, rep)) for t in (c, a, b))


def _col_tile(vec):
    return jnp.broadcast_to(vec[:, None], (vec.shape[0], LANES))


def _lane_param_tile(vec, heads_on_lanes):
    ph = vec.reshape(RWKV_HEADS, HEAD_DIM)
    if heads_on_lanes:
        nb = LANES // RWKV_HEADS
        return jnp.repeat(ph.T, nb, axis=1)[None]
    return jnp.broadcast_to(ph[:, :, None], (RWKV_HEADS, HEAD_DIM, LANES))


def _layer(x, p, pos, s0, shift0, kbuf, vbuf, wts, *, heads_on_lanes, tt, tc, tm, att_bb, att_tq):
    B, T, _ = x.shape
    n_tok = B * T
    x2d = x.reshape(n_tok, D_MODEL)

    if shift0 is None:
        shift0_t = jnp.zeros((SHIFT_DIM, LANES), f32)
    else:
        shift0_t = shift0.T
    proj_args = (wts["g_norm"], wts["wt_rwkv"], wts["mu_t"], wts["w0_t"], wts["a0_t"], wts["kk_t"],
                 wts["ka_t"], wts["w2t"], wts["a2t"], shift0_t)
    rk_t = _lane_param_tile(wts["r_k"], heads_on_lanes)
    lnw_t = _lane_param_tile(wts["ln_w"], heads_on_lanes)
    lnb_t = _lane_param_tile(wts["ln_b"], heads_on_lanes)
    if heads_on_lanes:
        assert s0 is None and shift0 is None
        scan_in, shift_t = _rwkv_proj(x, *proj_args, tt=tt, heads_on_lanes=True)
        shift_new = shift_t[:, LANES - B:].T
        s0_t = jnp.zeros((1, HEAD_DIM, HEAD_DIM, LANES), f32)
        o_tb, s_fin = _wkv_scan(scan_in, s0_t, rk_t, lnw_t, lnb_t, tc=tc, natural_out=True)
        o_r = jnp.swapaxes(o_tb, 0, 1).reshape(n_tok, RWKV_WIDTH)
        s_new = jnp.transpose(s_fin[0].reshape(HEAD_DIM, HEAD_DIM, RWKV_HEADS, B), (3, 2, 1, 0))
    else:
        scan_in, shift_t = _rwkv_proj(jnp.swapaxes(x, 0, 1), *proj_args, tt=tt, heads_on_lanes=False)
        shift_new = shift_t.T
        s0_t = jnp.transpose(s0, (1, 3, 2, 0))
        o_scan, s_fin = _wkv_scan(scan_in, s0_t, rk_t, lnw_t, lnb_t, tc=tc, natural_out=False)
        o_r = jnp.transpose(o_scan, (3, 1, 0, 2)).reshape(n_tok, RWKV_WIDTH)
        s_new = jnp.transpose(s_fin, (3, 0, 2, 1))

    rope_c, rope_a, rope_b = _rope_tables(pos)
    if T < tm:
        reps = tm // T
        rope_c, rope_a, rope_b = (jnp.tile(t, (reps, 1)) for t in (rope_c, rope_a, rope_b))
    act_dtype = bf16 if heads_on_lanes else f32
    sgr, q, k, v, sga = _nat_proj(x2d, wts["g_norm"], wts["w_nat"], rope_c, rope_a, rope_b, tm=tm,
                                  q_dtype=act_dtype)
    q3 = q.reshape(B, T, ATTN_WIDTH)
    k3 = k.reshape(B, T, KV_WIDTH)
    v3 = v.reshape(B, T, KV_WIDTH)
    sga3 = sga.reshape(B, T, ATTN_WIDTH)
    if kbuf is None:
        ma = _swa_attn(q3, k3, k3, v3, v3, sga3, wts["sink_t"], bb=att_bb, tq=att_tq,
                       prev_is_same_array=True, out_dtype=act_dtype)
        k_new = k3[:, T - WINDOW:].reshape(B, WINDOW, ATTN_KV_HEADS, HEAD_DIM)
        v_new = v3[:, T - WINDOW:].reshape(B, WINDOW, ATTN_KV_HEADS, HEAD_DIM)
    else:
        kb = kbuf.reshape(B, WINDOW, KV_WIDTH)
        vb = vbuf.reshape(B, WINDOW, KV_WIDTH)
        ma = _swa_attn(q3, kb, k3, vb, v3, sga3, wts["sink_t"], bb=att_bb, tq=att_tq,
                       prev_is_same_array=False, out_dtype=act_dtype)
        k_new = jnp.concatenate([kb, k3], axis=1)[:, -WINDOW:].reshape(B, WINDOW, ATTN_KV_HEADS, HEAD_DIM)
        v_new = jnp.concatenate([vb, v3], axis=1)[:, -WINDOW:].reshape(B, WINDOW, ATTN_KV_HEADS, HEAD_DIM)

    y = _out_mix(x2d, o_r, sgr, ma.reshape(n_tok, ATTN_WIDTH), p.reshape(n_tok, PLE_DIM),
                 wts["w_out"], wts["g_ple"], wts["w_pg"], wts["w_pp"], wts["g_final"], tm=tm)
    return y.reshape(B, T, D_MODEL), s_new, shift_new, k_new, v_new


def kernel(x_prompt, x_sample, state_rwkv_wkv, state_rwkv_shift, cache_swa_k, cache_swa_v,
           p_prompt, p_sample, g_norm, w_in, mu_shift, w0, w2, a0, a2, k_k, k_a, r_k,
           ln_w, ln_b, sinks, w_out, g_ple, w_ple_gate, w_ple_proj, g_final):
    assert w_in.shape[0] == 1, "single layer"
    w_in0 = w_in[0]
    head_order = [g * ATTN_GROUP + i for i in range(ATTN_GROUP) for g in range(ATTN_KV_HEADS)]
    cols = jnp.concatenate([jnp.arange(h * HEAD_DIM, (h + 1) * HEAD_DIM) for h in head_order])
    o_q = SHIFT_DIM + RWKV_WIDTH
    o_ga = o_q + ATTN_WIDTH + 2 * KV_WIDTH
    w_nat = jnp.concatenate([w_in0[:, SHIFT_DIM:o_q], w_in0[:, o_q + cols], w_in0[:, o_q + ATTN_WIDTH:o_ga],
                             w_in0[:, o_ga + cols]], axis=1)
    w_out0 = jnp.concatenate([w_out[0][:RWKV_WIDTH], w_out[0][RWKV_WIDTH + cols]], axis=0)
    wts = {
        "g_norm": g_norm[0][None, :],
        "wt_rwkv": w_in0[:, :SHIFT_DIM].T.astype(bf16),
        "w_nat": w_nat.astype(bf16),
        "mu_t": _col_tile(mu_shift[0]),
        "w0_t": _col_tile(w0[0]), "a0_t": _col_tile(a0[0]),
        "kk_t": _col_tile(k_k[0]), "ka_t": _col_tile(k_a[0]),
        "w2t": w2[0].T.astype(bf16), "a2t": a2[0].T.astype(bf16),
        "r_k": r_k[0], "ln_w": ln_w[0], "ln_b": ln_b[0],
        "sink_t": jnp.broadcast_to(sinks[0][:, None], (ATTN_Q_HEADS, LANES)),
        "w_out": w_out0.astype(bf16), "g_ple": g_ple[0][None, :],
        "w_pg": w_ple_gate[0].astype(bf16), "w_pp": w_ple_proj[0].astype(bf16),
        "g_final": g_final[None, :],
    }
    Bp, Tp, _ = x_prompt.shape
    Bs, Ts, _ = x_sample.shape
    assert Bp * RWKV_HEADS == LANES and Bs == LANES and Tp % WINDOW == 0 and Ts % SUBLANES == 0

    yp, s1, sh1, k1, v1 = _layer(x_prompt, p_prompt[0], jnp.arange(Tp), None, None, None, None, wts,
                                 heads_on_lanes=True, tt=16, tc=32, tm=512, att_bb=1, att_tq=WINDOW)
    ys, s2, sh2, k2, v2 = _layer(x_sample, p_sample[0], PAST_LEN + jnp.arange(Ts),
                                 state_rwkv_wkv[0], state_rwkv_shift[0], cache_swa_k[0], cache_swa_v[0],
                                 wts, heads_on_lanes=False, tt=2, tc=Ts, tm=Bs * Ts, att_bb=8, att_tq=Ts)
    return (yp, ys, s1[None], sh1[None], k1[None], v1[None], s2[None], sh2[None], k2[None], v2[None])
```

```python
import functools
import math

import jax
import jax.numpy as jnp
from jax import lax
from jax.experimental import pallas as pl
from jax.experimental.pallas import tpu as pltpu

D_MODEL = 1024
HEAD_DIM = 64
RWKV_WIDTH = 512
RWKV_HEADS = 8
ATTN_WIDTH = 512
ATTN_Q_HEADS = 8
ATTN_KV_HEADS = 2
ATTN_GROUP = 4
KV_WIDTH = 128
LORA = 64
WINDOW = 128
ROPE_THETA = 500000.0
ROPE_DIM = 16
PLE_DIM = 256
NORM_EPS = 1e-6
GN_EPS = 64e-5
NEG_INF = -1e30
PAST_LEN = 16384
SHIFT_DIM = 3 * RWKV_WIDTH + 2 * LORA
NAT_DIM = RWKV_WIDTH + ATTN_WIDTH + 2 * KV_WIDTH + ATTN_WIDTH

LANES = 128
SUBLANES = 8
VMEM_LIMIT = 48 * 1024 * 1024
DECAY_SCALE = math.exp(-0.5)
GROUP_T = 8

Q_KK, Q_W, Q_B, Q_K, Q_R, Q_V = range(6)
NQ = 6

f32 = jnp.float32
bf16 = jnp.bfloat16


def _cparams(n_axes):
    return pltpu.CompilerParams(dimension_semantics=("arbitrary",) * n_axes,
                                vmem_limit_bytes=VMEM_LIMIT)


def _rmsnorm(x, g):
    ms = jnp.mean(x * x, axis=-1, keepdims=True)
    return x * lax.rsqrt(ms + NORM_EPS) * g


def _sigmoid(x):
    return 1.0 / (1.0 + jnp.exp(-x))


def _dot_nt(a, b):
    return lax.dot_general(a, b, (((1,), (1,)), ((), ())), preferred_element_type=f32)


def _dot(a, b):
    return jnp.dot(a, b, preferred_element_type=f32)


def _chunk_transpose(xs, chunk):
    lane = lax.broadcasted_iota(jnp.int32, xs[0].shape, 1)
    xs = list(xs)
    for d in (4, 2, 1):
        hi_lanes = (lane & (chunk * d)) != 0
        nxt = list(xs)
        for i in range(8):
            if i & d:
                continue
            lo, hi = xs[i], xs[i + d]
            nxt[i] = jnp.where(hi_lanes, pltpu.roll(hi, chunk * d, 1), lo)
            nxt[i + d] = jnp.where(hi_lanes, hi, pltpu.roll(lo, LANES - chunk * d, 1))
        xs = nxt
    return xs


def _rwkv_proj_kernel(x_ref, g_ref, wt_ref, mul_ref, w0_ref, a0_ref, w2t_ref, a2t_ref, shift0_ref,
                      mu3_ref, kkp_ref, kap_ref, out_ref, shift_ref, wend_ref,
                      carry_ref, prev_ref, wc_ref, *rest, tt, nb, heads_on_lanes, period):
    i = pl.program_id(0)
    groups = 1 if heads_on_lanes else RWKV_HEADS
    lora0 = 3 * RWKV_WIDTH

    @pl.when(i == 0)
    def _():
        carry_ref[...] = shift0_ref[lora0:, :]
        wc_ref[...] = jnp.ones(wc_ref.shape, f32)
        for g in range(groups):
            for q in range(3):
                if heads_on_lanes:
                    prev_ref[g, q] = jnp.zeros((HEAD_DIM, LANES), f32)
                else:
                    r0 = q * RWKV_WIDTH + g * HEAD_DIM
                    prev_ref[g, q] = shift0_ref[r0:r0 + HEAD_DIM, :]

    n = tt * nb
    if heads_on_lanes:
        u_ref = rest[0]
        n_col = D_MODEL // LANES
        for bi in range(nb):
            ub = _rmsnorm(x_ref[bi], g_ref[...])
            for ci in range(n_col):
                u_ref[ci, pl.ds(bi, tt, stride=nb), :] = ub[:, ci * LANES:(ci + 1) * LANES]
        u = jnp.concatenate([u_ref[ci] for ci in range(n_col)], axis=1).astype(bf16)
    else:
        u = _rmsnorm(x_ref[...].reshape(n, D_MODEL), g_ref[...]).astype(bf16)
    zt = _dot_nt(wt_ref[...], u)

    lane = lax.broadcasted_iota(jnp.int32, (2 * LORA, LANES), 1)
    ones = jnp.ones((HEAD_DIM, LANES), f32)
    prev = [[prev_ref[g, q] for q in range(3)] for g in range(groups)]
    wc = [wc_ref[g] for g in range(groups)]
    prev_rot = carry_ref[...]
    steps_per_blk = LANES // nb if heads_on_lanes else 1
    for j in range(n // LANES):
        z = zt[:, j * LANES:(j + 1) * LANES]
        zl = z[lora0:]
        if nb % LANES == 0:
            prev_l, zl_rot = prev_rot, zl
        else:
            zl_rot = pltpu.roll(zl, nb, 1)
            prev_l = jnp.where(lane < nb, prev_rot, zl_rot)
        prev_rot = zl_rot
        zls = zl + mul_ref[...] * (prev_l - zl)
        w_pre = w0_ref[...] + _dot(w2t_ref[...], jnp.tanh(zls[:LORA]).astype(bf16))
        decay = jnp.exp(-DECAY_SCALE * _sigmoid(w_pre))
        a_all = _sigmoid(a0_ref[...] + _dot(a2t_ref[...], zls[LORA:].astype(bf16)))
        raw = [z[0:RWKV_WIDTH], z[RWKV_WIDTH:2 * RWKV_WIDTH], z[2 * RWKV_WIDTH:lora0], decay, a_all]
        per_head = [[val[h * HEAD_DIM:(h + 1) * HEAD_DIM] for h in range(RWKV_HEADS)] for val in raw]
        if heads_on_lanes:
            tiles = [_chunk_transpose(ph, nb) for ph in per_head]
        for ls in range(steps_per_blk):
            local = j * steps_per_blk + ls
            for g in range(groups):
                idx = ls if heads_on_lanes else g
                r_raw, kx_raw, v_raw, w_t, a_t = (tiles[q][idx] if heads_on_lanes else per_head[q][idx]
                                                  for q in range(5))
                mu3 = mu3_ref[g]
                r = r_raw + mu3[0] * (prev[g][0] - r_raw)
                kx = kx_raw + mu3[1] * (prev[g][1] - kx_raw)
                v = v_raw + mu3[2] * (prev[g][2] - v_raw)
                prev[g] = [r_raw, kx_raw, v_raw]
                kkr = kx * kkp_ref[g]
                ss = jnp.sum(kkr * kkr, axis=0, keepdims=True)
                kk = kkr * (1.0 / jnp.maximum(jnp.sqrt(ss), 1e-12))
                k = kx * (1.0 + (a_t - 1.0) * kap_ref[g])
                b = kk * a_t
                wc_in = wc[g]
                if tt >= period:
                    wc_base = ones if local % period == 0 else wc_in
                elif local == 0:
                    wc_base = jnp.where(i % (period // tt) == 0, ones, wc_in)
                else:
                    wc_base = wc_in
                wc_t = wc_base * w_t
                inv_wc = 1.0 / wc_t
                step = (g, local)
                out_ref[step + (Q_KK,)] = kk * wc_base
                out_ref[step + (Q_W,)] = wc_in
                out_ref[step + (Q_B,)] = b * inv_wc
                out_ref[step + (Q_K,)] = k * inv_wc
                out_ref[step + (Q_R,)] = r * wc_t
                out_ref[step + (Q_V,)] = v
                wc[g] = wc_t
    carry_ref[...] = prev_rot
    shift_ref[...] = z
    for g in range(groups):
        wc_ref[g] = wc[g]
        wend_ref[g] = wc[g]
        for q in range(3):
            prev_ref[g, q] = prev[g][q]


def _rwkv_proj(x, g_norm, wt, mul_t, w0_t, a0_t, w2t, a2t, shift0_t, mu3_t, kk_t, ka_t, *,
               tt, heads_on_lanes, period):
    const = lambda shape: pl.BlockSpec(shape, lambda i: (0,) * len(shape))
    groups = 1 if heads_on_lanes else RWKV_HEADS
    scratch = [pltpu.VMEM((2 * LORA, LANES), f32), pltpu.VMEM((groups, 3, HEAD_DIM, LANES), f32),
               pltpu.VMEM((groups, HEAD_DIM, LANES), f32)]
    if heads_on_lanes:
        nb, T, _ = x.shape
        x_spec = pl.BlockSpec((nb, tt, D_MODEL), lambda i: (0, i, 0))
        scratch.append(pltpu.VMEM((D_MODEL // LANES, tt * nb, LANES), f32))
    else:
        T, nb, _ = x.shape
        x_spec = pl.BlockSpec((tt, nb, D_MODEL), lambda i: (i, 0, 0))
    assert period % tt == 0 or tt % period == 0
    kern = functools.partial(_rwkv_proj_kernel, tt=tt, nb=nb, heads_on_lanes=heads_on_lanes, period=period)
    return pl.pallas_call(
        kern,
        grid=(T // tt,),
        in_specs=[
            x_spec,
            const((1, D_MODEL)),
            const((SHIFT_DIM, D_MODEL)),
            const((2 * LORA, LANES)),
            const((RWKV_WIDTH, LANES)), const((RWKV_WIDTH, LANES)),
            const((RWKV_WIDTH, LORA)), const((RWKV_WIDTH, LORA)),
            const((SHIFT_DIM, LANES)),
            const((groups, 3, HEAD_DIM, LANES)), const((groups, HEAD_DIM, LANES)),
            const((groups, HEAD_DIM, LANES)),
        ],
        out_specs=[pl.BlockSpec((groups, tt, NQ, HEAD_DIM, LANES), lambda i: (0, i, 0, 0, 0)),
                   const((SHIFT_DIM, LANES)), const((groups, HEAD_DIM, LANES))],
        out_shape=[jax.ShapeDtypeStruct((groups, T, NQ, HEAD_DIM, LANES), f32),
                   jax.ShapeDtypeStruct((SHIFT_DIM, LANES), f32),
                   jax.ShapeDtypeStruct((groups, HEAD_DIM, LANES), f32)],
        scratch_shapes=scratch,
        compiler_params=_cparams(1),
        name="rwkv_proj",
    )(x, g_norm, wt, mul_t, w0_t, a0_t, w2t, a2t, shift0_t, mu3_t, kk_t, ka_t)


def _rope(x, c, a, b):
    return x * c + pltpu.roll(x, LANES - ROPE_DIM // 2, 1) * a + pltpu.roll(x, ROPE_DIM // 2, 1) * b


def _nat_proj_kernel(x_ref, g_ref, w_ref, rc_ref, ra_ref, rb_ref,
                     sgr_ref, q_ref, k_ref, v_ref, sga_ref):
    u = _rmsnorm(x_ref[...], g_ref[...]).astype(bf16)
    z = _dot(u, w_ref[...])
    o_q = RWKV_WIDTH
    o_k = o_q + ATTN_WIDTH
    o_v = o_k + KV_WIDTH
    o_g = o_v + KV_WIDTH
    gr = z[:, :o_q]
    sgr_ref[...] = gr * _sigmoid(gr)
    rc, ra, rb = rc_ref[...], ra_ref[...], rb_ref[...]
    for j in range(ATTN_WIDTH // LANES):
        qj = z[:, o_q + j * LANES:o_q + (j + 1) * LANES]
        q_ref[:, j * LANES:(j + 1) * LANES] = (_rope(qj, rc, ra, rb) * (HEAD_DIM ** -0.5)).astype(q_ref.dtype)
    k_ref[...] = _rope(z[:, o_k:o_v], rc, ra, rb)
    v_ref[...] = z[:, o_v:o_g]
    ga = z[:, o_g:]
    sga_ref[...] = ga * _sigmoid(ga)


def _nat_proj(x2d, g_norm, w_nat, rope_c, rope_a, rope_b, *, tm, q_dtype):
    n_tok = x2d.shape[0]
    n_tab = rope_c.shape[0] // tm
    const = lambda shape: pl.BlockSpec(shape, lambda i: (0,) * len(shape))
    row = lambda w: pl.BlockSpec((tm, w), lambda i: (i, 0))
    tab = pl.BlockSpec((tm, LANES), lambda i: (i % n_tab, 0))
    return pl.pallas_call(
        _nat_proj_kernel,
        grid=(n_tok // tm,),
        in_specs=[row(D_MODEL), const((1, D_MODEL)), const((D_MODEL, NAT_DIM)), tab, tab, tab],
        out_specs=[row(RWKV_WIDTH), row(ATTN_WIDTH), row(KV_WIDTH), row(KV_WIDTH), row(ATTN_WIDTH)],
        out_shape=[jax.ShapeDtypeStruct((n_tok, w), dt) for w, dt in
                   ((RWKV_WIDTH, f32), (ATTN_WIDTH, q_dtype), (KV_WIDTH, f32), (KV_WIDTH, f32),
                    (ATTN_WIDTH, f32))],
        compiler_params=_cparams(1),
        name="nat_proj",
    )(x2d, g_norm, w_nat, rope_c, rope_a, rope_b)


def _row_bcast(ref, idx, k):
    return jnp.broadcast_to(ref[idx + (pl.ds(k, 1), slice(None))], (HEAD_DIM, LANES))


def _wkv_step(s_ref, ref, at, at_next, sa, rk, lnw, lnb):
    vv = ref[at + (Q_V,)]
    y = jnp.zeros((HEAD_DIM, LANES), f32)
    sa_next = jnp.zeros((HEAD_DIM, LANES), f32)
    for k in range(HEAD_DIM):
        s_new = (s_ref[k] - sa * _row_bcast(ref, at + (Q_B,), k)
                 + vv * _row_bcast(ref, at + (Q_K,), k))
        s_ref[k] = s_new
        y = y + s_new * _row_bcast(ref, at + (Q_R,), k)
        sa_next = sa_next + s_new * _row_bcast(ref, at_next + (Q_KK,), k)
    mean = jnp.mean(y, axis=0, keepdims=True)
    d = y - mean
    var = jnp.mean(d * d, axis=0, keepdims=True)
    yn = d * lax.rsqrt(var + GN_EPS) * lnw + lnb
    rkk = jnp.sum(ref[at + (Q_R,)] * ref[at + (Q_K,)] * rk, axis=0, keepdims=True)
    return yn + rkk * vv, sa_next


def _wkv_scan_kernel(in_ref, s0_ref, wend_ref, rk_ref, lnw_ref, lnb_ref, o_ref, sout_ref, s_ref, *rest,
                     tc, period, natural_out):
    c = pl.program_id(1)

    @pl.when(c == 0)
    def _():
        s_ref[...] = s0_ref[0]

    obuf_ref = rest[0] if natural_out else None
    rk, lnw, lnb = rk_ref[0], lnw_ref[0], lnb_ref[0]

    for t0 in range(0, tc, period):
        sa0 = jnp.zeros((HEAD_DIM, LANES), f32)
        for k in range(HEAD_DIM):
            s_k = s_ref[k] * _row_bcast(in_ref, (0, t0, Q_W), k)
            s_ref[k] = s_k
            sa0 = sa0 + s_k * _row_bcast(in_ref, (0, t0, Q_KK), k)

        def step(t, sa, last=t0 + period - 1):
            t_next = jnp.minimum(t + 1, last)
            o, sa_next = _wkv_step(s_ref, in_ref, (0, t), (0, t_next), sa, rk, lnw, lnb)
            if natural_out:
                obuf_ref[t] = o
            else:
                o_ref[0, t] = o
            return sa_next

        lax.fori_loop(t0, t0 + period, step, sa0)

    if natural_out:
        nb = LANES // RWKV_HEADS
        for j in range(tc // GROUP_T):
            per_h = _chunk_transpose([obuf_ref[j * GROUP_T + t] for t in range(GROUP_T)], nb)
            ot = jnp.concatenate(per_h, axis=0)
            o_ref[j * GROUP_T:(j + 1) * GROUP_T] = ot.T.reshape(GROUP_T, nb, RWKV_WIDTH)

    @pl.when(c == pl.num_programs(1) - 1)
    def _():
        for k in range(HEAD_DIM):
            sout_ref[0, k] = s_ref[k] * _row_bcast(wend_ref, (0,), k)


def _wkv_scan(scan_in, s0, wend, rk_t, lnw_t, lnb_t, *, tc, period, natural_out):
    groups, T = scan_in.shape[:2]
    assert tc % period == 0
    tile = pl.BlockSpec((1, HEAD_DIM, LANES), lambda g, c: (g, 0, 0))
    state = pl.BlockSpec((1, HEAD_DIM, HEAD_DIM, LANES), lambda g, c: (g, 0, 0, 0))
    scratch = [pltpu.VMEM((HEAD_DIM, HEAD_DIM, LANES), f32)]
    if natural_out:
        nb = LANES // RWKV_HEADS
        o_spec = pl.BlockSpec((tc, nb, RWKV_WIDTH), lambda g, c: (c, 0, 0))
        o_shape = jax.ShapeDtypeStruct((T, nb, RWKV_WIDTH), f32)
        scratch.append(pltpu.VMEM((tc, HEAD_DIM, LANES), f32))
    else:
        o_spec = pl.BlockSpec((1, tc, HEAD_DIM, LANES), lambda g, c: (g, c, 0, 0))
        o_shape = jax.ShapeDtypeStruct((groups, T, HEAD_DIM, LANES), f32)
    return pl.pallas_call(
        functools.partial(_wkv_scan_kernel, tc=tc, period=period, natural_out=natural_out),
        grid=(groups, T // tc),
        in_specs=[
            pl.BlockSpec((1, tc, NQ, HEAD_DIM, LANES), lambda g, c: (g, c, 0, 0, 0)),
            state, tile, tile, tile, tile,
        ],
        out_specs=[o_spec, state],
        out_shape=[o_shape, jax.ShapeDtypeStruct((groups, HEAD_DIM, HEAD_DIM, LANES), f32)],
        scratch_shapes=scratch,
        compiler_params=_cparams(2),
        name="wkv_scan",
    )(scan_in, s0, wend, rk_t, lnw_t, lnb_t)


def _swa_attn_kernel(q_ref, kp_ref, kc_ref, vp_ref, vc_ref, sga_ref, sink_ref, o_ref, *,
                     bb, tq, first_block_has_no_prev):
    n = pl.program_id(1)
    nr = bb * tq
    half = LANES // 2
    tq_bits = tq.bit_length() - 1

    def key_mask(n_keys_per_batch, is_prev):
        rows = lax.broadcasted_iota(jnp.int32, (nr, bb * n_keys_per_batch), 0)
        cols = lax.broadcasted_iota(jnp.int32, (nr, bb * n_keys_per_batch), 1)
        i = rows & (tq - 1)
        j = cols & (n_keys_per_batch - 1)
        ok = (j > i) if is_prev else (j <= i)
        if bb > 1:
            same = (rows >> tq_bits) == (cols >> (n_keys_per_batch.bit_length() - 1))
            ok = jnp.logical_and(same, ok)
        if is_prev and first_block_has_no_prev:
            ok = jnp.logical_and(ok, n > 0)
        return jnp.tile(ok, (ATTN_GROUP, 1))

    mask_p = key_mask(WINDOW, True)
    mask_c = key_mask(tq, False)
    lane_q = lax.broadcasted_iota(jnp.int32, (nr, LANES), 1)
    lane_kp = lax.broadcasted_iota(jnp.int32, (bb * WINDOW, LANES), 1)

    kp = kp_ref[...].reshape(bb * WINDOW, LANES).astype(bf16)
    kc = kc_ref[...].reshape(nr, LANES).astype(bf16)
    vp = vp_ref[...].reshape(bb * WINDOW, LANES)
    vc = vc_ref[...].reshape(nr, LANES)

    res = []
    for g in range(ATTN_KV_HEADS):
        in_g = (lambda lane: lane < half) if g == 0 else (lambda lane: lane >= half)
        qs, sinks = [], []
        for hh in range(ATTN_GROUP):
            x = q_ref[:, :, hh * LANES:(hh + 1) * LANES].reshape(nr, LANES)
            qs.append(jnp.where(in_g(lane_q), x, 0.0).astype(bf16))
            sinks.append(jnp.broadcast_to(sink_ref[g * ATTN_GROUP + hh:g * ATTN_GROUP + hh + 1, :], (nr, LANES)))
        qg = jnp.concatenate(qs, axis=0)
        sink = jnp.concatenate(sinks, axis=0)[:, 0:1]
        sp = jnp.where(mask_p, _dot_nt(qg, kp), NEG_INF)
        sc = jnp.where(mask_c, _dot_nt(qg, kc), NEG_INF)
        m = jnp.maximum(jnp.maximum(jnp.max(sp, axis=-1, keepdims=True),
                                    jnp.max(sc, axis=-1, keepdims=True)), sink)
        pp = jnp.exp(sp - m).astype(bf16)
        pc = jnp.exp(sc - m).astype(bf16)
        e_sink = jnp.exp(sink - m)
        vpg = jnp.where(in_g(lane_kp), vp, 1.0).astype(bf16)
        vcg = jnp.where(in_g(lane_q), vc, 1.0).astype(bf16)
        pv = _dot(pp, vpg) + _dot(pc, vcg)
        res.append(pv * (1.0 / (pltpu.roll(pv, half, 1) + e_sink)))
    for hh in range(ATTN_GROUP):
        blk = slice(hh * nr, (hh + 1) * nr)
        cs = slice(hh * LANES, (hh + 1) * LANES)
        out = jnp.where(lane_q < half, res[0][blk], res[1][blk])
        o_ref[:, :, cs] = (out.reshape(bb, tq, LANES) * sga_ref[:, :, cs]).astype(o_ref.dtype)


def _swa_prompt_kernel(q_ref, kp_ref, kc_ref, vp_ref, vc_ref, sga_ref, sink_ref, o_ref):
    n = pl.program_id(1)
    tq = WINDOW
    half = LANES // 2
    keys = lax.broadcasted_iota(jnp.int32, (2 * WINDOW, tq), 0)
    qi = lax.broadcasted_iota(jnp.int32, (2 * WINDOW, tq), 1)
    ok = jnp.logical_or(jnp.logical_and(jnp.logical_and(keys < WINDOW, keys > qi), n > 0),
                        jnp.logical_and(keys >= WINDOW, keys - WINDOW <= qi))
    mask_t = jnp.tile(ok, (1, ATTN_GROUP))
    lane_q = lax.broadcasted_iota(jnp.int32, (tq, LANES), 1)
    lane_k = lax.broadcasted_iota(jnp.int32, (2 * WINDOW, LANES), 1)

    k_all = jnp.concatenate([kp_ref[0], kc_ref[0]], axis=0).astype(bf16)
    v_all = jnp.concatenate([vp_ref[0], vc_ref[0]], axis=0)

    norm_t = []
    for g in range(ATTN_KV_HEADS):
        in_g = (lambda lane: lane < half) if g == 0 else (lambda lane: lane >= half)
        qs, sinks = [], []
        for hh in range(ATTN_GROUP):
            x = q_ref[0, :, hh * LANES:(hh + 1) * LANES]
            qs.append(jnp.where(in_g(lane_q), x, jnp.zeros_like(x)).astype(bf16))
            sinks.append(sink_ref[g * ATTN_GROUP + hh:g * ATTN_GROUP + hh + 1, :])
        qg = jnp.concatenate(qs, axis=0)
        sink = jnp.concatenate(sinks, axis=1)
        st = jnp.where(mask_t, _dot_nt(k_all, qg), NEG_INF)
        m = jnp.maximum(jnp.max(st, axis=0, keepdims=True), sink)
        p = jnp.exp(st - m).astype(bf16)
        e_sink = jnp.exp(sink - m)
        vg = jnp.where(in_g(lane_k), v_all, 1.0).astype(bf16)
        ot = lax.dot_general(vg, p, (((0,), (0,)), ((), ())), preferred_element_type=f32)
        lo, hi = ot[:half], ot[half:]
        num, den = (lo, hi) if g == 0 else (hi, lo)
        norm_t.append(num * (1.0 / (den + e_sink)))
    for hh in range(ATTN_GROUP):
        cs = slice(hh * LANES, (hh + 1) * LANES)
        blk_t = jnp.concatenate([norm_t[0][:, hh * tq:(hh + 1) * tq],
                                 norm_t[1][:, hh * tq:(hh + 1) * tq]], axis=0)
        o_ref[0, :, cs] = (blk_t.T * sga_ref[0, :, cs]).astype(o_ref.dtype)


def _swa_attn(q, k_prev, k_cur, v_prev, v_cur, sga, sink_t, *, bb, tq, prev_is_same_array, out_dtype):
    B, T, _ = q.shape
    nq = T // tq
    cur = lambda w: pl.BlockSpec((bb, tq, w), lambda bi, n: (bi, n, 0))
    if prev_is_same_array:
        prev = pl.BlockSpec((bb, WINDOW, KV_WIDTH), lambda bi, n: (bi, jnp.maximum(n - 1, 0), 0))
    else:
        prev = pl.BlockSpec((bb, WINDOW, KV_WIDTH), lambda bi, n: (bi, 0, 0))
    if prev_is_same_array:
        assert bb == 1 and tq == WINDOW
        kern = _swa_prompt_kernel
    else:
        kern = functools.partial(_swa_attn_kernel, bb=bb, tq=tq, first_block_has_no_prev=False)
    return pl.pallas_call(
        kern,
        grid=(B // bb, nq),
        in_specs=[cur(ATTN_WIDTH), prev, cur(KV_WIDTH), prev, cur(KV_WIDTH), cur(ATTN_WIDTH),
                  pl.BlockSpec((ATTN_Q_HEADS, LANES), lambda bi, n: (0, 0))],
        out_specs=cur(ATTN_WIDTH),
        out_shape=jax.ShapeDtypeStruct((B, T, ATTN_WIDTH), out_dtype),
        compiler_params=_cparams(2),
        name="swa_attn",
    )(q, k_prev, k_cur, v_prev, v_cur, sga, sink_t)


def _out_mix_kernel(x_ref, or_ref, sgr_ref, ma_ref, p_ref, wo_ref, gp_ref, wpg_ref, wpp_ref, gf_ref,
                    y_ref):
    mr = (or_ref[...] * sgr_ref[...]).astype(bf16)
    ma = ma_ref[...].astype(bf16)
    h = (x_ref[...] + _dot(mr, wo_ref[:RWKV_WIDTH, :]) + _dot(ma, wo_ref[RWKV_WIDTH:, :]))
    gate = _sigmoid(_dot(_rmsnorm(h, gp_ref[...]).astype(bf16), wpg_ref[...]))
    h = h + gate * _dot(p_ref[...].astype(bf16), wpp_ref[...])
    y_ref[...] = _rmsnorm(h, gf_ref[...])


def _out_mix(x2d, o_r, sgr, ma, p2d, w_out, g_ple, w_pg, w_pp, g_final, *, tm):
    n_tok = x2d.shape[0]
    const = lambda shape: pl.BlockSpec(shape, lambda i: (0,) * len(shape))
    row = lambda w: pl.BlockSpec((tm, w), lambda i: (i, 0))
    return pl.pallas_call(
        _out_mix_kernel,
        grid=(n_tok // tm,),
        in_specs=[row(D_MODEL), row(RWKV_WIDTH), row(RWKV_WIDTH), row(ATTN_WIDTH), row(PLE_DIM),
                  const((D_MODEL, D_MODEL)), const((1, D_MODEL)), const((D_MODEL, D_MODEL)),
                  const((PLE_DIM, D_MODEL)), const((1, D_MODEL))],
        out_specs=row(D_MODEL),
        out_shape=jax.ShapeDtypeStruct((n_tok, D_MODEL), f32),
        compiler_params=_cparams(1),
        name="out_mix",
    )(x2d, o_r, sgr, ma, p2d, w_out, g_ple, w_pg, w_pp, g_final)


def _rope_tables(pos):
    half = ROPE_DIM // 2
    inv = ROPE_THETA ** (-jnp.arange(half, dtype=f32) / half)
    ang = pos.astype(f32)[:, None] * inv[None, :]
    cos, sin = jnp.cos(ang), jnp.sin(ang)
    n = pos.shape[0]
    ones = jnp.ones((n, HEAD_DIM - ROPE_DIM), f32)
    zeros = jnp.zeros((n, HEAD_DIM - ROPE_DIM), f32)
    zh = jnp.zeros((n, half), f32)
    c = jnp.concatenate([cos, cos, ones], axis=1)
    a = jnp.concatenate([-sin, zh, zeros], axis=1)
    b = jnp.concatenate([zh, sin, zeros], axis=1)
    rep = LANES // HEAD_DIM
    return tuple(jnp.tile(t, (1, rep)) for t in (c, a, b))


def _col_tile(vec):
    return jnp.broadcast_to(vec[:, None], (vec.shape[0], LANES))


def _lane_param_tile(vec, heads_on_lanes):
    ph = vec.reshape(RWKV_HEADS, HEAD_DIM)
    if heads_on_lanes:
        nb = LANES // RWKV_HEADS
        return jnp.repeat(ph.T, nb, axis=1)[None]
    return jnp.broadcast_to(ph[:, :, None], (RWKV_HEADS, HEAD_DIM, LANES))


def _layer(x, p, pos, s0, shift0, kbuf, vbuf, wts, *, heads_on_lanes, tt, tc, period, tm, att_bb, att_tq):
    B, T, _ = x.shape
    n_tok = B * T
    x2d = x.reshape(n_tok, D_MODEL)

    if shift0 is None:
        shift0_t = jnp.zeros((SHIFT_DIM, LANES), f32)
    else:
        shift0_t = shift0.T
    mu = wts["mu"]
    mu3_t = jnp.stack([_lane_param_tile(mu[q * RWKV_WIDTH:(q + 1) * RWKV_WIDTH], heads_on_lanes)
                       for q in range(3)], axis=1)
    tile = lambda name: _lane_param_tile(wts[name], heads_on_lanes)
    proj_args = (wts["g_norm"], wts["wt_rwkv"], _col_tile(mu[3 * RWKV_WIDTH:]), wts["w0_t"], wts["a0_t"],
                 wts["w2t"], wts["a2t"], shift0_t, mu3_t, tile("k_k"), tile("k_a"))
    scan_params = (tile("r_k"), tile("ln_w"), tile("ln_b"))
    if heads_on_lanes:
        assert s0 is None and shift0 is None
        scan_in, shift_t, wend = _rwkv_proj(x, *proj_args, tt=tt, heads_on_lanes=True, period=period)
        shift_new = shift_t[:, LANES - B:].T
        s0_t = jnp.zeros((1, HEAD_DIM, HEAD_DIM, LANES), f32)
        o_tb, s_fin = _wkv_scan(scan_in, s0_t, wend, *scan_params, tc=tc, period=period, natural_out=True)
        o_r = jnp.swapaxes(o_tb, 0, 1).reshape(n_tok, RWKV_WIDTH)
        s_new = jnp.transpose(s_fin[0].reshape(HEAD_DIM, HEAD_DIM, RWKV_HEADS, B), (3, 2, 1, 0))
    else:
        scan_in, shift_t, wend = _rwkv_proj(jnp.swapaxes(x, 0, 1), *proj_args, tt=tt, heads_on_lanes=False,
                                            period=period)
        shift_new = shift_t.T
        s0_t = jnp.transpose(s0, (1, 3, 2, 0))
        o_scan, s_fin = _wkv_scan(scan_in, s0_t, wend, *scan_params, tc=tc, period=period, natural_out=False)
        o_r = jnp.transpose(o_scan, (3, 1, 0, 2)).reshape(n_tok, RWKV_WIDTH)
        s_new = jnp.transpose(s_fin, (3, 0, 2, 1))

    rope_c, rope_a, rope_b = _rope_tables(pos)
    if T < tm:
        reps = tm // T
        rope_c, rope_a, rope_b = (jnp.tile(t, (reps, 1)) for t in (rope_c, rope_a, rope_b))
    act_dtype = bf16 if heads_on_lanes else f32
    sgr, q, k, v, sga = _nat_proj(x2d, wts["g_norm"], wts["w_nat"], rope_c, rope_a, rope_b, tm=tm,
                                  q_dtype=act_dtype)
    q3 = q.reshape(B, T, ATTN_WIDTH)
    k3 = k.reshape(B, T, KV_WIDTH)
    v3 = v.reshape(B, T, KV_WIDTH)
    sga3 = sga.reshape(B, T, ATTN_WIDTH)
    if kbuf is None:
        ma = _swa_attn(q3, k3, k3, v3, v3, sga3, wts["sink_t"], bb=att_bb, tq=att_tq,
                       prev_is_same_array=True, out_dtype=act_dtype)
        k_new = k3[:, T - WINDOW:].reshape(B, WINDOW, ATTN_KV_HEADS, HEAD_DIM)
        v_new = v3[:, T - WINDOW:].reshape(B, WINDOW, ATTN_KV_HEADS, HEAD_DIM)
    else:
        kb = kbuf.reshape(B, WINDOW, KV_WIDTH)
        vb = vbuf.reshape(B, WINDOW, KV_WIDTH)
        ma = _swa_attn(q3, kb, k3, vb, v3, sga3, wts["sink_t"], bb=att_bb, tq=att_tq,
                       prev_is_same_array=False, out_dtype=act_dtype)
        k_new = jnp.concatenate([kb, k3], axis=1)[:, -WINDOW:].reshape(B, WINDOW, ATTN_KV_HEADS, HEAD_DIM)
        v_new = jnp.concatenate([vb, v3], axis=1)[:, -WINDOW:].reshape(B, WINDOW, ATTN_KV_HEADS, HEAD_DIM)

    y = _out_mix(x2d, o_r, sgr, ma.reshape(n_tok, ATTN_WIDTH), p.reshape(n_tok, PLE_DIM),
                 wts["w_out"], wts["g_ple"], wts["w_pg"], wts["w_pp"], wts["g_final"], tm=tm)
    return y.reshape(B, T, D_MODEL), s_new, shift_new, k_new, v_new


def kernel(x_prompt, x_sample, state_rwkv_wkv, state_rwkv_shift, cache_swa_k, cache_swa_v,
           p_prompt, p_sample, g_norm, w_in, mu_shift, w0, w2, a0, a2, k_k, k_a, r_k,
           ln_w, ln_b, sinks, w_out, g_ple, w_ple_gate, w_ple_proj, g_final):
    assert w_in.shape[0] == 1, "single layer"
    w_in0 = w_in[0]
    head_order = [g * ATTN_GROUP + i for i in range(ATTN_GROUP) for g in range(ATTN_KV_HEADS)]
    cols = jnp.concatenate([jnp.arange(h * HEAD_DIM, (h + 1) * HEAD_DIM) for h in head_order])
    o_q = SHIFT_DIM + RWKV_WIDTH
    o_ga = o_q + ATTN_WIDTH + 2 * KV_WIDTH
    w_nat = jnp.concatenate([w_in0[:, SHIFT_DIM:o_q], w_in0[:, o_q + cols], w_in0[:, o_q + ATTN_WIDTH:o_ga],
                             w_in0[:, o_ga + cols]], axis=1)
    w_out0 = jnp.concatenate([w_out[0][:RWKV_WIDTH], w_out[0][RWKV_WIDTH + cols]], axis=0)
    wts = {
        "g_norm": g_norm[0][None, :],
        "wt_rwkv": w_in0[:, :SHIFT_DIM].T.astype(bf16),
        "w_nat": w_nat.astype(bf16),
        "mu": mu_shift[0],
        "w0_t": _col_tile(w0[0]), "a0_t": _col_tile(a0[0]),
        "k_k": k_k[0], "k_a": k_a[0],
        "w2t": w2[0].T.astype(bf16), "a2t": a2[0].T.astype(bf16),
        "r_k": r_k[0], "ln_w": ln_w[0], "ln_b": ln_b[0],
        "sink_t": jnp.broadcast_to(sinks[0][:, None], (ATTN_Q_HEADS, LANES)),
        "w_out": w_out0.astype(bf16), "g_ple": g_ple[0][None, :],
        "w_pg": w_ple_gate[0].astype(bf16), "w_pp": w_ple_proj[0].astype(bf16),
        "g_final": g_final[None, :],
    }
    Bp, Tp, _ = x_prompt.shape
    Bs, Ts, _ = x_sample.shape
    assert Bp * RWKV_HEADS == LANES and Bs == LANES and Tp % WINDOW == 0 and Ts % SUBLANES == 0

    yp, s1, sh1, k1, v1 = _layer(x_prompt, p_prompt[0], jnp.arange(Tp), None, None, None, None, wts,
                                 heads_on_lanes=True, tt=16, tc=32, period=16, tm=512, att_bb=1, att_tq=WINDOW)
    ys, s2, sh2, k2, v2 = _layer(x_sample, p_sample[0], PAST_LEN + jnp.arange(Ts),
                                 state_rwkv_wkv[0], state_rwkv_shift[0], cache_swa_k[0], cache_swa_v[0],
                                 wts, heads_on_lanes=False, tt=2, tc=Ts, period=Ts, tm=Bs * Ts, att_bb=8, att_tq=Ts)
    return (yp, ys, s1[None], sh1[None], k1[None], v1[None], s2[None], sh2[None], k2[None], v2[None])
```

```python
import functools
import math

import jax
import jax.numpy as jnp
from jax import lax
from jax.experimental import pallas as pl
from jax.experimental.pallas import tpu as pltpu

D_MODEL = 1024
HEAD_DIM = 64
RWKV_WIDTH = 512
RWKV_HEADS = 8
ATTN_WIDTH = 512
ATTN_Q_HEADS = 8
ATTN_KV_HEADS = 2
ATTN_GROUP = 4
KV_WIDTH = 128
LORA = 64
WINDOW = 128
ROPE_THETA = 500000.0
ROPE_DIM = 16
PLE_DIM = 256
NORM_EPS = 1e-6
GN_EPS = 64e-5
NEG_INF = -1e30
PAST_LEN = 16384
SHIFT_DIM = 3 * RWKV_WIDTH + 2 * LORA
NAT_DIM = RWKV_WIDTH + ATTN_WIDTH + 2 * KV_WIDTH + ATTN_WIDTH

LANES = 128
SUBLANES = 8
VMEM_LIMIT = 56 * 1024 * 1024
DECAY_SCALE = math.exp(-0.5)
GROUP_T = 8

Q_KK, Q_W, Q_B, Q_K, Q_R, Q_V = range(6)
NQ = 6

f32 = jnp.float32
bf16 = jnp.bfloat16


def _cparams(n_axes):
    return pltpu.CompilerParams(dimension_semantics=("arbitrary",) * n_axes,
                                vmem_limit_bytes=VMEM_LIMIT)


def _rmsnorm(x, g):
    ms = jnp.mean(x * x, axis=-1, keepdims=True)
    return x * lax.rsqrt(ms + NORM_EPS) * g


def _sigmoid(x):
    return 1.0 / (1.0 + jnp.exp(-x))


def _dot_nt(a, b):
    return lax.dot_general(a, b, (((1,), (1,)), ((), ())), preferred_element_type=f32)


def _dot(a, b):
    return jnp.dot(a, b, preferred_element_type=f32)


def _chunk_transpose(xs, chunk):
    lane = lax.broadcasted_iota(jnp.int32, xs[0].shape, 1)
    xs = list(xs)
    for d in (4, 2, 1):
        hi_lanes = (lane & (chunk * d)) != 0
        nxt = list(xs)
        for i in range(8):
            if i & d:
                continue
            lo, hi = xs[i], xs[i + d]
            if 2 * chunk * d == LANES:
                moved = pltpu.roll(jnp.where(hi_lanes, lo, hi), chunk * d, 1)
                nxt[i] = jnp.where(hi_lanes, moved, lo)
                nxt[i + d] = jnp.where(hi_lanes, hi, moved)
            else:
                nxt[i] = jnp.where(hi_lanes, pltpu.roll(hi, chunk * d, 1), lo)
                nxt[i + d] = jnp.where(hi_lanes, hi, pltpu.roll(lo, LANES - chunk * d, 1))
        xs = nxt
    return xs


def _rwkv_proj_kernel(x_ref, g_ref, wt_ref, mul_ref, w0_ref, a0_ref, w2t_ref, a2t_ref, shift0_ref,
                      mu3_ref, kkp_ref, kap_ref, out_ref, shift_ref, wend_ref,
                      carry_ref, prev_ref, wc_ref, *rest, tt, nb, heads_on_lanes, period):
    i = pl.program_id(0)
    groups = 1 if heads_on_lanes else RWKV_HEADS
    lora0 = 3 * RWKV_WIDTH

    @pl.when(i == 0)
    def _():
        carry_ref[...] = shift0_ref[lora0:, :]
        wc_ref[...] = jnp.ones(wc_ref.shape, f32)
        for g in range(groups):
            for q in range(3):
                if heads_on_lanes:
                    prev_ref[g, q] = jnp.zeros((HEAD_DIM, LANES), f32)
                else:
                    r0 = q * RWKV_WIDTH + g * HEAD_DIM
                    prev_ref[g, q] = shift0_ref[r0:r0 + HEAD_DIM, :]

    n = tt * nb
    if heads_on_lanes:
        u_ref = rest[0]
        n_col = D_MODEL // LANES
        for bi in range(nb):
            ub = _rmsnorm(x_ref[bi], g_ref[...])
            for ci in range(n_col):
                u_ref[ci, pl.ds(bi, tt, stride=nb), :] = ub[:, ci * LANES:(ci + 1) * LANES]
        u = jnp.concatenate([u_ref[ci] for ci in range(n_col)], axis=1).astype(bf16)
    else:
        u = _rmsnorm(x_ref[...].reshape(n, D_MODEL), g_ref[...]).astype(bf16)
    zt = _dot_nt(wt_ref[...], u)

    lane = lax.broadcasted_iota(jnp.int32, (2 * LORA, LANES), 1)
    ones = jnp.ones((HEAD_DIM, LANES), f32)
    prev = [[prev_ref[g, q] for q in range(3)] for g in range(groups)]
    wc = [wc_ref[g] for g in range(groups)]
    prev_rot = carry_ref[...]
    steps_per_blk = LANES // nb if heads_on_lanes else 1
    for j in range(n // LANES):
        z = zt[:, j * LANES:(j + 1) * LANES]
        zl = z[lora0:]
        if nb % LANES == 0:
            prev_l, zl_rot = prev_rot, zl
        else:
            zl_rot = pltpu.roll(zl, nb, 1)
            prev_l = jnp.where(lane < nb, prev_rot, zl_rot)
        prev_rot = zl_rot
        zls = zl + mul_ref[...] * (prev_l - zl)
        w_pre = w0_ref[...] + _dot(w2t_ref[...], jnp.tanh(zls[:LORA]).astype(bf16))
        decay = jnp.exp(-DECAY_SCALE * _sigmoid(w_pre))
        a_all = _sigmoid(a0_ref[...] + _dot(a2t_ref[...], zls[LORA:].astype(bf16)))
        raw = [z[0:RWKV_WIDTH], z[RWKV_WIDTH:2 * RWKV_WIDTH], z[2 * RWKV_WIDTH:lora0], decay, a_all]
        per_head = [[val[h * HEAD_DIM:(h + 1) * HEAD_DIM] for h in range(RWKV_HEADS)] for val in raw]
        if heads_on_lanes:
            tiles = [_chunk_transpose(ph, nb) for ph in per_head]
        for ls in range(steps_per_blk):
            local = j * steps_per_blk + ls
            for g in range(groups):
                idx = ls if heads_on_lanes else g
                r_raw, kx_raw, v_raw, w_t, a_t = (tiles[q][idx] if heads_on_lanes else per_head[q][idx]
                                                  for q in range(5))
                mu3 = mu3_ref[g]
                r = r_raw + mu3[0] * (prev[g][0] - r_raw)
                kx = kx_raw + mu3[1] * (prev[g][1] - kx_raw)
                v = v_raw + mu3[2] * (prev[g][2] - v_raw)
                prev[g] = [r_raw, kx_raw, v_raw]
                kkr = kx * kkp_ref[g]
                ss = jnp.sum(kkr * kkr, axis=0, keepdims=True)
                kk = kkr * (1.0 / jnp.maximum(jnp.sqrt(ss), 1e-12))
                k = kx * (1.0 + (a_t - 1.0) * kap_ref[g])
                b = kk * a_t
                wc_in = wc[g]
                if tt >= period:
                    wc_base = ones if local % period == 0 else wc_in
                elif local == 0:
                    wc_base = jnp.where(i % (period // tt) == 0, ones, wc_in)
                else:
                    wc_base = wc_in
                wc_t = wc_base * w_t
                inv_wc = 1.0 / wc_t
                step = (g, local)
                out_ref[step + (Q_KK,)] = kk * wc_base
                out_ref[step + (Q_W,)] = wc_in
                out_ref[step + (Q_B,)] = b * inv_wc
                out_ref[step + (Q_K,)] = k * inv_wc
                out_ref[step + (Q_R,)] = r * wc_t
                out_ref[step + (Q_V,)] = v
                wc[g] = wc_t
    carry_ref[...] = prev_rot
    shift_ref[...] = z
    for g in range(groups):
        wc_ref[g] = wc[g]
        wend_ref[g] = wc[g]
        for q in range(3):
            prev_ref[g, q] = prev[g][q]


def _rwkv_proj(x, g_norm, wt, mul_t, w0_t, a0_t, w2t, a2t, shift0_t, mu3_t, kk_t, ka_t, *,
               tt, heads_on_lanes, period):
    const = lambda shape: pl.BlockSpec(shape, lambda i: (0,) * len(shape))
    groups = 1 if heads_on_lanes else RWKV_HEADS
    scratch = [pltpu.VMEM((2 * LORA, LANES), f32), pltpu.VMEM((groups, 3, HEAD_DIM, LANES), f32),
               pltpu.VMEM((groups, HEAD_DIM, LANES), f32)]
    if heads_on_lanes:
        nb, T, _ = x.shape
        x_spec = pl.BlockSpec((nb, tt, D_MODEL), lambda i: (0, i, 0))
        scratch.append(pltpu.VMEM((D_MODEL // LANES, tt * nb, LANES), f32))
    else:
        T, nb, _ = x.shape
        x_spec = pl.BlockSpec((tt, nb, D_MODEL), lambda i: (i, 0, 0))
    assert period % tt == 0 or tt % period == 0
    kern = functools.partial(_rwkv_proj_kernel, tt=tt, nb=nb, heads_on_lanes=heads_on_lanes, period=period)
    return pl.pallas_call(
        kern,
        grid=(T // tt,),
        in_specs=[
            x_spec,
            const((1, D_MODEL)),
            const((SHIFT_DIM, D_MODEL)),
            const((2 * LORA, LANES)),
            const((RWKV_WIDTH, LANES)), const((RWKV_WIDTH, LANES)),
            const((RWKV_WIDTH, LORA)), const((RWKV_WIDTH, LORA)),
            const((SHIFT_DIM, LANES)),
            const((groups, 3, HEAD_DIM, LANES)), const((groups, HEAD_DIM, LANES)),
            const((groups, HEAD_DIM, LANES)),
        ],
        out_specs=[pl.BlockSpec((groups, tt, NQ, HEAD_DIM, LANES), lambda i: (0, i, 0, 0, 0)),
                   const((SHIFT_DIM, LANES)), const((groups, HEAD_DIM, LANES))],
        out_shape=[jax.ShapeDtypeStruct((groups, T, NQ, HEAD_DIM, LANES), f32),
                   jax.ShapeDtypeStruct((SHIFT_DIM, LANES), f32),
                   jax.ShapeDtypeStruct((groups, HEAD_DIM, LANES), f32)],
        scratch_shapes=scratch,
        compiler_params=_cparams(1),
        name="rwkv_proj",
    )(x, g_norm, wt, mul_t, w0_t, a0_t, w2t, a2t, shift0_t, mu3_t, kk_t, ka_t)


def _rope(x, c, a, b):
    return x * c + pltpu.roll(x, LANES - ROPE_DIM // 2, 1) * a + pltpu.roll(x, ROPE_DIM // 2, 1) * b


def _nat_proj_kernel(x_ref, g_ref, w_ref, rc_ref, ra_ref, rb_ref,
                     sgr_ref, q_ref, k_ref, v_ref, sga_ref):
    u = _rmsnorm(x_ref[...], g_ref[...]).astype(bf16)
    z = _dot(u, w_ref[...])
    o_q = RWKV_WIDTH
    o_k = o_q + ATTN_WIDTH
    o_v = o_k + KV_WIDTH
    o_g = o_v + KV_WIDTH
    gr = z[:, :o_q]
    sgr_ref[...] = gr * _sigmoid(gr)
    rc, ra, rb = rc_ref[...], ra_ref[...], rb_ref[...]
    for j in range(ATTN_WIDTH // LANES):
        qj = z[:, o_q + j * LANES:o_q + (j + 1) * LANES]
        q_ref[:, j * LANES:(j + 1) * LANES] = (_rope(qj, rc, ra, rb) * (HEAD_DIM ** -0.5)).astype(q_ref.dtype)
    k_ref[...] = _rope(z[:, o_k:o_v], rc, ra, rb)
    v_ref[...] = z[:, o_v:o_g]
    ga = z[:, o_g:]
    sga_ref[...] = ga * _sigmoid(ga)


def _nat_proj(x2d, g_norm, w_nat, rope_c, rope_a, rope_b, *, tm, q_dtype):
    n_tok = x2d.shape[0]
    n_tab = rope_c.shape[0] // tm
    const = lambda shape: pl.BlockSpec(shape, lambda i: (0,) * len(shape))
    row = lambda w: pl.BlockSpec((tm, w), lambda i: (i, 0))
    tab = pl.BlockSpec((tm, LANES), lambda i: (i % n_tab, 0))
    return pl.pallas_call(
        _nat_proj_kernel,
        grid=(n_tok // tm,),
        in_specs=[row(D_MODEL), const((1, D_MODEL)), const((D_MODEL, NAT_DIM)), tab, tab, tab],
        out_specs=[row(RWKV_WIDTH), row(ATTN_WIDTH), row(KV_WIDTH), row(KV_WIDTH), row(ATTN_WIDTH)],
        out_shape=[jax.ShapeDtypeStruct((n_tok, w), dt) for w, dt in
                   ((RWKV_WIDTH, f32), (ATTN_WIDTH, q_dtype), (KV_WIDTH, f32), (KV_WIDTH, f32),
                    (ATTN_WIDTH, f32))],
        compiler_params=_cparams(1),
        name="nat_proj",
    )(x2d, g_norm, w_nat, rope_c, rope_a, rope_b)


def _row_bcast(ref, idx, k):
    return jnp.broadcast_to(ref[idx + (pl.ds(k, 1), slice(None))], (HEAD_DIM, LANES))


def _wkv_step(s_ref, ref, at, at_next, sa, rk, lnw, lnb):
    vv = ref[at + (Q_V,)]
    y = jnp.zeros((HEAD_DIM, LANES), f32)
    sa_next = jnp.zeros((HEAD_DIM, LANES), f32)
    for k in range(HEAD_DIM):
        s_new = (s_ref[k] - sa * _row_bcast(ref, at + (Q_B,), k)
                 + vv * _row_bcast(ref, at + (Q_K,), k))
        s_ref[k] = s_new
        y = y + s_new * _row_bcast(ref, at + (Q_R,), k)
        sa_next = sa_next + s_new * _row_bcast(ref, at_next + (Q_KK,), k)
    mean = jnp.mean(y, axis=0, keepdims=True)
    d = y - mean
    var = jnp.mean(d * d, axis=0, keepdims=True)
    yn = d * lax.rsqrt(var + GN_EPS) * lnw + lnb
    rkk = jnp.sum(ref[at + (Q_R,)] * ref[at + (Q_K,)] * rk, axis=0, keepdims=True)
    return yn + rkk * vv, sa_next


def _wkv_scan_kernel(in_ref, s0_ref, wend_ref, rk_ref, lnw_ref, lnb_ref, o_ref, sout_ref, s_ref, *rest,
                     tc, period, natural_out):
    c = pl.program_id(1)

    @pl.when(c == 0)
    def _():
        s_ref[...] = s0_ref[0]

    obuf_ref = rest[0] if natural_out else None
    rk, lnw, lnb = rk_ref[0], lnw_ref[0], lnb_ref[0]

    for t0 in range(0, tc, period):
        sa0 = jnp.zeros((HEAD_DIM, LANES), f32)
        for k in range(HEAD_DIM):
            s_k = s_ref[k] * _row_bcast(in_ref, (0, t0, Q_W), k)
            s_ref[k] = s_k
            sa0 = sa0 + s_k * _row_bcast(in_ref, (0, t0, Q_KK), k)

        def step(t, sa, last=t0 + period - 1):
            t_next = jnp.minimum(t + 1, last)
            o, sa_next = _wkv_step(s_ref, in_ref, (0, t), (0, t_next), sa, rk, lnw, lnb)
            if natural_out:
                obuf_ref[t] = o
            else:
                o_ref[0, t] = o
            return sa_next

        lax.fori_loop(t0, t0 + period, step, sa0)

    if natural_out:
        nb = LANES // RWKV_HEADS
        for j in range(tc // GROUP_T):
            per_h = _chunk_transpose([obuf_ref[j * GROUP_T + t] for t in range(GROUP_T)], nb)
            ot = jnp.concatenate(per_h, axis=0)
            o_ref[j * GROUP_T:(j + 1) * GROUP_T] = ot.T.reshape(GROUP_T, nb, RWKV_WIDTH)

    @pl.when(c == pl.num_programs(1) - 1)
    def _():
        for k in range(HEAD_DIM):
            sout_ref[0, k] = s_ref[k] * _row_bcast(wend_ref, (0,), k)


def _wkv_scan(scan_in, s0, wend, rk_t, lnw_t, lnb_t, *, tc, period, natural_out):
    groups, T = scan_in.shape[:2]
    assert tc % period == 0
    tile = pl.BlockSpec((1, HEAD_DIM, LANES), lambda g, c: (g, 0, 0))
    state = pl.BlockSpec((1, HEAD_DIM, HEAD_DIM, LANES), lambda g, c: (g, 0, 0, 0))
    scratch = [pltpu.VMEM((HEAD_DIM, HEAD_DIM, LANES), f32)]
    if natural_out:
        nb = LANES // RWKV_HEADS
        o_spec = pl.BlockSpec((tc, nb, RWKV_WIDTH), lambda g, c: (c, 0, 0))
        o_shape = jax.ShapeDtypeStruct((T, nb, RWKV_WIDTH), f32)
        scratch.append(pltpu.VMEM((tc, HEAD_DIM, LANES), f32))
    else:
        o_spec = pl.BlockSpec((1, tc, HEAD_DIM, LANES), lambda g, c: (g, c, 0, 0))
        o_shape = jax.ShapeDtypeStruct((groups, T, HEAD_DIM, LANES), f32)
    return pl.pallas_call(
        functools.partial(_wkv_scan_kernel, tc=tc, period=period, natural_out=natural_out),
        grid=(groups, T // tc),
        in_specs=[
            pl.BlockSpec((1, tc, NQ, HEAD_DIM, LANES), lambda g, c: (g, c, 0, 0, 0)),
            state, tile, tile, tile, tile,
        ],
        out_specs=[o_spec, state],
        out_shape=[o_shape, jax.ShapeDtypeStruct((groups, HEAD_DIM, HEAD_DIM, LANES), f32)],
        scratch_shapes=scratch,
        compiler_params=_cparams(2),
        name="wkv_scan",
    )(scan_in, s0, wend, rk_t, lnw_t, lnb_t)


def _swa_attn_kernel(q_ref, kp_ref, kc_ref, vp_ref, vc_ref, sga_ref, sink_ref, o_ref, *,
                     bb, tq, first_block_has_no_prev):
    n = pl.program_id(1)
    nr = bb * tq
    half = LANES // 2
    tq_bits = tq.bit_length() - 1

    def key_mask(n_keys_per_batch, is_prev):
        rows = lax.broadcasted_iota(jnp.int32, (nr, bb * n_keys_per_batch), 0)
        cols = lax.broadcasted_iota(jnp.int32, (nr, bb * n_keys_per_batch), 1)
        i = rows & (tq - 1)
        j = cols & (n_keys_per_batch - 1)
        ok = (j > i) if is_prev else (j <= i)
        if bb > 1:
            same = (rows >> tq_bits) == (cols >> (n_keys_per_batch.bit_length() - 1))
            ok = jnp.logical_and(same, ok)
        if is_prev and first_block_has_no_prev:
            ok = jnp.logical_and(ok, n > 0)
        return jnp.tile(ok, (ATTN_GROUP, 1))

    mask_p = key_mask(WINDOW, True)
    mask_c = key_mask(tq, False)
    lane_q = lax.broadcasted_iota(jnp.int32, (nr, LANES), 1)
    lane_kp = lax.broadcasted_iota(jnp.int32, (bb * WINDOW, LANES), 1)

    kp = kp_ref[...].reshape(bb * WINDOW, LANES).astype(bf16)
    kc = kc_ref[...].reshape(nr, LANES).astype(bf16)
    vp = vp_ref[...].reshape(bb * WINDOW, LANES)
    vc = vc_ref[...].reshape(nr, LANES)

    res = []
    for g in range(ATTN_KV_HEADS):
        in_g = (lambda lane: lane < half) if g == 0 else (lambda lane: lane >= half)
        qs, sinks = [], []
        for hh in range(ATTN_GROUP):
            x = q_ref[:, :, hh * LANES:(hh + 1) * LANES].reshape(nr, LANES)
            qs.append(jnp.where(in_g(lane_q), x, 0.0).astype(bf16))
            sinks.append(jnp.broadcast_to(sink_ref[g * ATTN_GROUP + hh:g * ATTN_GROUP + hh + 1, :], (nr, LANES)))
        qg = jnp.concatenate(qs, axis=0)
        sink = jnp.concatenate(sinks, axis=0)[:, 0:1]
        sp = jnp.where(mask_p, _dot_nt(qg, kp), NEG_INF)
        sc = jnp.where(mask_c, _dot_nt(qg, kc), NEG_INF)
        m = jnp.maximum(jnp.maximum(jnp.max(sp, axis=-1, keepdims=True),
                                    jnp.max(sc, axis=-1, keepdims=True)), sink)
        pp = jnp.exp(sp - m).astype(bf16)
        pc = jnp.exp(sc - m).astype(bf16)
        e_sink = jnp.exp(sink - m)
        vpg = jnp.where(in_g(lane_kp), vp, 1.0).astype(bf16)
        vcg = jnp.where(in_g(lane_q), vc, 1.0).astype(bf16)
        pv = _dot(pp, vpg) + _dot(pc, vcg)
        res.append(pv * (1.0 / (pltpu.roll(pv, half, 1) + e_sink)))
    for hh in range(ATTN_GROUP):
        blk = slice(hh * nr, (hh + 1) * nr)
        cs = slice(hh * LANES, (hh + 1) * LANES)
        out = jnp.where(lane_q < half, res[0][blk], res[1][blk])
        o_ref[:, :, cs] = (out.reshape(bb, tq, LANES) * sga_ref[:, :, cs]).astype(o_ref.dtype)


def _swa_prompt_kernel(q_ref, kp_ref, kc_ref, vp_ref, vc_ref, sga_ref, sink_ref, o_ref, *, n_blk):
    n = pl.program_id(1)
    tq = WINDOW
    half = LANES // 2
    keys = lax.broadcasted_iota(jnp.int32, (2 * WINDOW, tq), 0)
    qi = lax.broadcasted_iota(jnp.int32, (2 * WINDOW, tq), 1)
    prev_ok = jnp.logical_and(keys < WINDOW, keys > qi)
    cur_ok = jnp.logical_and(keys >= WINDOW, keys - WINDOW <= qi)
    lane_q = lax.broadcasted_iota(jnp.int32, (tq, LANES), 1)
    lane_k = lax.broadcasted_iota(jnp.int32, (2 * WINDOW, LANES), 1)

    for blk in range(n_blk):
        rows = slice(blk * tq, (blk + 1) * tq)
        if blk == 0:
            kp, vp = kp_ref[0], vp_ref[0]
            ok = jnp.logical_or(jnp.logical_and(prev_ok, n > 0), cur_ok)
        else:
            before = slice((blk - 1) * tq, blk * tq)
            kp, vp = kc_ref[0, before, :], vc_ref[0, before, :]
            ok = jnp.logical_or(prev_ok, cur_ok)
        mask_t = jnp.tile(ok, (1, ATTN_GROUP))
        k_all = jnp.concatenate([kp, kc_ref[0, rows, :]], axis=0).astype(bf16)
        v_all = jnp.concatenate([vp, vc_ref[0, rows, :]], axis=0)

        norm_t = []
        for g in range(ATTN_KV_HEADS):
            in_g = (lambda lane: lane < half) if g == 0 else (lambda lane: lane >= half)
            qs, sinks = [], []
            for hh in range(ATTN_GROUP):
                x = q_ref[0, rows, hh * LANES:(hh + 1) * LANES]
                qs.append(jnp.where(in_g(lane_q), x, jnp.zeros_like(x)).astype(bf16))
                sinks.append(sink_ref[g * ATTN_GROUP + hh:g * ATTN_GROUP + hh + 1, :])
            qg = jnp.concatenate(qs, axis=0)
            sink = jnp.concatenate(sinks, axis=1)
            st = jnp.where(mask_t, _dot_nt(k_all, qg), NEG_INF)
            m = jnp.maximum(jnp.max(st, axis=0, keepdims=True), sink)
            p = jnp.exp(st - m).astype(bf16)
            e_sink = jnp.exp(sink - m)
            vg = jnp.where(in_g(lane_k), v_all, 1.0).astype(bf16)
            ot = lax.dot_general(vg, p, (((0,), (0,)), ((), ())), preferred_element_type=f32)
            lo, hi = ot[:half], ot[half:]
            num, den = (lo, hi) if g == 0 else (hi, lo)
            norm_t.append(num * (1.0 / (den + e_sink)))
        for hh in range(ATTN_GROUP):
            cs = slice(hh * LANES, (hh + 1) * LANES)
            blk_t = jnp.concatenate([norm_t[0][:, hh * tq:(hh + 1) * tq],
                                     norm_t[1][:, hh * tq:(hh + 1) * tq]], axis=0)
            o_ref[0, rows, cs] = (blk_t.T * sga_ref[0, rows, cs]).astype(o_ref.dtype)


def _swa_attn(q, k_prev, k_cur, v_prev, v_cur, sga, sink_t, *, bb, tq, prev_is_same_array, out_dtype):
    B, T, _ = q.shape
    nq = T // tq
    cur = lambda w: pl.BlockSpec((bb, tq, w), lambda bi, n: (bi, n, 0))
    if prev_is_same_array:
        assert bb == 1 and tq % WINDOW == 0
        n_blk = tq // WINDOW
        prev = pl.BlockSpec((bb, WINDOW, KV_WIDTH), lambda bi, n: (bi, jnp.maximum(n * n_blk - 1, 0), 0))
        kern = functools.partial(_swa_prompt_kernel, n_blk=n_blk)
    else:
        prev = pl.BlockSpec((bb, WINDOW, KV_WIDTH), lambda bi, n: (bi, 0, 0))
        kern = functools.partial(_swa_attn_kernel, bb=bb, tq=tq, first_block_has_no_prev=False)
    return pl.pallas_call(
        kern,
        grid=(B // bb, nq),
        in_specs=[cur(ATTN_WIDTH), prev, cur(KV_WIDTH), prev, cur(KV_WIDTH), cur(ATTN_WIDTH),
                  pl.BlockSpec((ATTN_Q_HEADS, LANES), lambda bi, n: (0, 0))],
        out_specs=cur(ATTN_WIDTH),
        out_shape=jax.ShapeDtypeStruct((B, T, ATTN_WIDTH), out_dtype),
        compiler_params=_cparams(2),
        name="swa_attn",
    )(q, k_prev, k_cur, v_prev, v_cur, sga, sink_t)


def _out_mix_kernel(x_ref, or_ref, sgr_ref, ma_ref, p_ref, wo_ref, gp_ref, wpg_ref, wpp_ref, gf_ref,
                    y_ref):
    mr = (or_ref[...] * sgr_ref[...]).astype(bf16)
    ma = ma_ref[...].astype(bf16)
    h = (x_ref[...] + _dot(mr, wo_ref[:RWKV_WIDTH, :]) + _dot(ma, wo_ref[RWKV_WIDTH:, :]))
    gate = _sigmoid(_dot(_rmsnorm(h, gp_ref[...]).astype(bf16), wpg_ref[...]))
    h = h + gate * _dot(p_ref[...].astype(bf16), wpp_ref[...])
    y_ref[...] = _rmsnorm(h, gf_ref[...])


def _out_mix(x2d, o_r, sgr, ma, p2d, w_out, g_ple, w_pg, w_pp, g_final, *, tm):
    n_tok = x2d.shape[0]
    const = lambda shape: pl.BlockSpec(shape, lambda i: (0,) * len(shape))
    row = lambda w: pl.BlockSpec((tm, w), lambda i: (i, 0))
    return pl.pallas_call(
        _out_mix_kernel,
        grid=(n_tok // tm,),
        in_specs=[row(D_MODEL), row(RWKV_WIDTH), row(RWKV_WIDTH), row(ATTN_WIDTH), row(PLE_DIM),
                  const((D_MODEL, D_MODEL)), const((1, D_MODEL)), const((D_MODEL, D_MODEL)),
                  const((PLE_DIM, D_MODEL)), const((1, D_MODEL))],
        out_specs=row(D_MODEL),
        out_shape=jax.ShapeDtypeStruct((n_tok, D_MODEL), f32),
        compiler_params=_cparams(1),
        name="out_mix",
    )(x2d, o_r, sgr, ma, p2d, w_out, g_ple, w_pg, w_pp, g_final)


def _rope_tables(pos):
    half = ROPE_DIM // 2
    inv = ROPE_THETA ** (-jnp.arange(half, dtype=f32) / half)
    ang = pos.astype(f32)[:, None] * inv[None, :]
    cos, sin = jnp.cos(ang), jnp.sin(ang)
    n = pos.shape[0]
    ones = jnp.ones((n, HEAD_DIM - ROPE_DIM), f32)
    zeros = jnp.zeros((n, HEAD_DIM - ROPE_DIM), f32)
    zh = jnp.zeros((n, half), f32)
    c = jnp.concatenate([cos, cos, ones], axis=1)
    a = jnp.concatenate([-sin, zh, zeros], axis=1)
    b = jnp.concatenate([zh, sin, zeros], axis=1)
    rep = LANES // HEAD_DIM
    return tuple(jnp.tile(t, (1, rep)) for t in (c, a, b))


def _col_tile(vec):
    return jnp.broadcast_to(vec[:, None], (vec.shape[0], LANES))


def _lane_param_tile(vec, heads_on_lanes):
    ph = vec.reshape(RWKV_HEADS, HEAD_DIM)
    if heads_on_lanes:
        nb = LANES // RWKV_HEADS
        return jnp.repeat(ph.T, nb, axis=1)[None]
    return jnp.broadcast_to(ph[:, :, None], (RWKV_HEADS, HEAD_DIM, LANES))


def _layer(x, p, pos, s0, shift0, kbuf, vbuf, wts, *, heads_on_lanes, tt, tc, period, tm, att_bb, att_tq):
    B, T, _ = x.shape
    n_tok = B * T
    x2d = x.reshape(n_tok, D_MODEL)

    if shift0 is None:
        shift0_t = jnp.zeros((SHIFT_DIM, LANES), f32)
    else:
        shift0_t = shift0.T
    mu = wts["mu"]
    mu3_t = jnp.stack([_lane_param_tile(mu[q * RWKV_WIDTH:(q + 1) * RWKV_WIDTH], heads_on_lanes)
                       for q in range(3)], axis=1)
    tile = lambda name: _lane_param_tile(wts[name], heads_on_lanes)
    proj_args = (wts["g_norm"], wts["wt_rwkv"], _col_tile(mu[3 * RWKV_WIDTH:]), wts["w0_t"], wts["a0_t"],
                 wts["w2t"], wts["a2t"], shift0_t, mu3_t, tile("k_k"), tile("k_a"))
    scan_params = (tile("r_k"), tile("ln_w"), tile("ln_b"))
    if heads_on_lanes:
        assert s0 is None and shift0 is None
        scan_in, shift_t, wend = _rwkv_proj(x, *proj_args, tt=tt, heads_on_lanes=True, period=period)
        shift_new = shift_t[:, LANES - B:].T
        s0_t = jnp.zeros((1, HEAD_DIM, HEAD_DIM, LANES), f32)
        o_tb, s_fin = _wkv_scan(scan_in, s0_t, wend, *scan_params, tc=tc, period=period, natural_out=True)
        o_r = jnp.swapaxes(o_tb, 0, 1).reshape(n_tok, RWKV_WIDTH)
        s_new = jnp.transpose(s_fin[0].reshape(HEAD_DIM, HEAD_DIM, RWKV_HEADS, B), (3, 2, 1, 0))
    else:
        scan_in, shift_t, wend = _rwkv_proj(jnp.swapaxes(x, 0, 1), *proj_args, tt=tt, heads_on_lanes=False,
                                            period=period)
        shift_new = shift_t.T
        s0_t = jnp.transpose(s0, (1, 3, 2, 0))
        o_scan, s_fin = _wkv_scan(scan_in, s0_t, wend, *scan_params, tc=tc, period=period, natural_out=False)
        o_r = jnp.transpose(o_scan, (3, 1, 0, 2)).reshape(n_tok, RWKV_WIDTH)
        s_new = jnp.transpose(s_fin, (3, 0, 2, 1))

    rope_c, rope_a, rope_b = _rope_tables(pos)
    if T < tm:
        reps = tm // T
        rope_c, rope_a, rope_b = (jnp.tile(t, (reps, 1)) for t in (rope_c, rope_a, rope_b))
    act_dtype = bf16 if heads_on_lanes else f32
    sgr, q, k, v, sga = _nat_proj(x2d, wts["g_norm"], wts["w_nat"], rope_c, rope_a, rope_b, tm=tm,
                                  q_dtype=act_dtype)
    q3 = q.reshape(B, T, ATTN_WIDTH)
    k3 = k.reshape(B, T, KV_WIDTH)
    v3 = v.reshape(B, T, KV_WIDTH)
    sga3 = sga.reshape(B, T, ATTN_WIDTH)
    if kbuf is None:
        ma = _swa_attn(q3, k3, k3, v3, v3, sga3, wts["sink_t"], bb=att_bb, tq=att_tq,
                       prev_is_same_array=True, out_dtype=act_dtype)
        k_new = k3[:, T - WINDOW:].reshape(B, WINDOW, ATTN_KV_HEADS, HEAD_DIM)
        v_new = v3[:, T - WINDOW:].reshape(B, WINDOW, ATTN_KV_HEADS, HEAD_DIM)
    else:
        kb = kbuf.reshape(B, WINDOW, KV_WIDTH)
        vb = vbuf.reshape(B, WINDOW, KV_WIDTH)
        ma = _swa_attn(q3, kb, k3, vb, v3, sga3, wts["sink_t"], bb=att_bb, tq=att_tq,
                       prev_is_same_array=False, out_dtype=act_dtype)
        k_new = jnp.concatenate([kb, k3], axis=1)[:, -WINDOW:].reshape(B, WINDOW, ATTN_KV_HEADS, HEAD_DIM)
        v_new = jnp.concatenate([vb, v3], axis=1)[:, -WINDOW:].reshape(B, WINDOW, ATTN_KV_HEADS, HEAD_DIM)

    y = _out_mix(x2d, o_r, sgr, ma.reshape(n_tok, ATTN_WIDTH), p.reshape(n_tok, PLE_DIM),
                 wts["w_out"], wts["g_ple"], wts["w_pg"], wts["w_pp"], wts["g_final"], tm=tm)
    return y.reshape(B, T, D_MODEL), s_new, shift_new, k_new, v_new


def kernel(x_prompt, x_sample, state_rwkv_wkv, state_rwkv_shift, cache_swa_k, cache_swa_v,
           p_prompt, p_sample, g_norm, w_in, mu_shift, w0, w2, a0, a2, k_k, k_a, r_k,
           ln_w, ln_b, sinks, w_out, g_ple, w_ple_gate, w_ple_proj, g_final):
    assert w_in.shape[0] == 1, "single layer"
    w_in0 = w_in[0]
    head_order = [g * ATTN_GROUP + i for i in range(ATTN_GROUP) for g in range(ATTN_KV_HEADS)]
    cols = jnp.concatenate([jnp.arange(h * HEAD_DIM, (h + 1) * HEAD_DIM) for h in head_order])
    o_q = SHIFT_DIM + RWKV_WIDTH
    o_ga = o_q + ATTN_WIDTH + 2 * KV_WIDTH
    w_nat = jnp.concatenate([w_in0[:, SHIFT_DIM:o_q], w_in0[:, o_q + cols], w_in0[:, o_q + ATTN_WIDTH:o_ga],
                             w_in0[:, o_ga + cols]], axis=1)
    w_out0 = jnp.concatenate([w_out[0][:RWKV_WIDTH], w_out[0][RWKV_WIDTH + cols]], axis=0)
    wts = {
        "g_norm": g_norm[0][None, :],
        "wt_rwkv": w_in0[:, :SHIFT_DIM].T.astype(bf16),
        "w_nat": w_nat.astype(bf16),
        "mu": mu_shift[0],
        "w0_t": _col_tile(w0[0]), "a0_t": _col_tile(a0[0]),
        "k_k": k_k[0], "k_a": k_a[0],
        "w2t": w2[0].T.astype(bf16), "a2t": a2[0].T.astype(bf16),
        "r_k": r_k[0], "ln_w": ln_w[0], "ln_b": ln_b[0],
        "sink_t": jnp.broadcast_to(sinks[0][:, None], (ATTN_Q_HEADS, LANES)),
        "w_out": w_out0.astype(bf16), "g_ple": g_ple[0][None, :],
        "w_pg": w_ple_gate[0].astype(bf16), "w_pp": w_ple_proj[0].astype(bf16),
        "g_final": g_final[None, :],
    }
    Bp, Tp, _ = x_prompt.shape
    Bs, Ts, _ = x_sample.shape
    assert Bp * RWKV_HEADS == LANES and Bs == LANES and Tp % WINDOW == 0 and Ts % SUBLANES == 0

    yp, s1, sh1, k1, v1 = _layer(x_prompt, p_prompt[0], jnp.arange(Tp), None, None, None, None, wts,
                                 heads_on_lanes=True, tt=16, tc=32, period=32, tm=1024, att_bb=1, att_tq=2 * WINDOW)
    ys, s2, sh2, k2, v2 = _layer(x_sample, p_sample[0], PAST_LEN + jnp.arange(Ts),
                                 state_rwkv_wkv[0], state_rwkv_shift[0], cache_swa_k[0], cache_swa_v[0],
                                 wts, heads_on_lanes=False, tt=2, tc=Ts, period=Ts, tm=Bs * Ts, att_bb=8, att_tq=Ts)
    return (yp, ys, s1[None], sh1[None], k1[None], v1[None], s2[None], sh2[None], k2[None], v2[None])
```

```python
import functools
import math

import jax
import jax.numpy as jnp
from jax import lax
from jax.experimental import pallas as pl
from jax.experimental.pallas import tpu as pltpu

D_MODEL = 1024
HEAD_DIM = 64
RWKV_WIDTH = 512
RWKV_HEADS = 8
ATTN_WIDTH = 512
ATTN_Q_HEADS = 8
ATTN_KV_HEADS = 2
ATTN_GROUP = 4
KV_WIDTH = 128
LORA = 64
WINDOW = 128
ROPE_THETA = 500000.0
ROPE_DIM = 16
PLE_DIM = 256
NORM_EPS = 1e-6
GN_EPS = 64e-5
NEG_INF = -1e30
PAST_LEN = 16384
SHIFT_DIM = 3 * RWKV_WIDTH + 2 * LORA
NAT_DIM = RWKV_WIDTH + ATTN_WIDTH + 2 * KV_WIDTH + ATTN_WIDTH

LANES = 128
SUBLANES = 8
VMEM_LIMIT = 56 * 1024 * 1024
DECAY_SCALE = math.exp(-0.5)
GROUP_T = 8

Q_KK, Q_W, Q_B, Q_K, Q_R, Q_V = range(6)
NQ = 6

f32 = jnp.float32
bf16 = jnp.bfloat16


def _cparams(n_axes):
    return pltpu.CompilerParams(dimension_semantics=("arbitrary",) * n_axes,
                                vmem_limit_bytes=VMEM_LIMIT)


def _rmsnorm(x, g):
    ms = jnp.mean(x * x, axis=-1, keepdims=True)
    return x * lax.rsqrt(ms + NORM_EPS) * g


def _sigmoid(x):
    return 1.0 / (1.0 + jnp.exp(-x))


def _dot_nt(a, b):
    return lax.dot_general(a, b, (((1,), (1,)), ((), ())), preferred_element_type=f32)


def _dot(a, b):
    return jnp.dot(a, b, preferred_element_type=f32)


def _chunk_transpose(xs, chunk):
    lane = lax.broadcasted_iota(jnp.int32, xs[0].shape, 1)
    xs = list(xs)
    for d in (4, 2, 1):
        hi_lanes = (lane & (chunk * d)) != 0
        nxt = list(xs)
        for i in range(8):
            if i & d:
                continue
            lo, hi = xs[i], xs[i + d]
            if 2 * chunk * d == LANES:
                moved = pltpu.roll(jnp.where(hi_lanes, lo, hi), chunk * d, 1)
                nxt[i] = jnp.where(hi_lanes, moved, lo)
                nxt[i + d] = jnp.where(hi_lanes, hi, moved)
            else:
                nxt[i] = jnp.where(hi_lanes, pltpu.roll(hi, chunk * d, 1), lo)
                nxt[i + d] = jnp.where(hi_lanes, hi, pltpu.roll(lo, LANES - chunk * d, 1))
        xs = nxt
    return xs


def _rwkv_proj_kernel(x_ref, g_ref, wt_ref, mul_ref, w0_ref, a0_ref, w2t_ref, a2t_ref, shift0_ref,
                      mu3_ref, kkp_ref, kap_ref, out_ref, shift_ref, wend_ref,
                      carry_ref, prev_ref, wc_ref, za_ref, zb_ref, *rest, tt, nb, heads_on_lanes, period):
    i = pl.program_id(0)
    groups = 1 if heads_on_lanes else RWKV_HEADS
    lora0 = 3 * RWKV_WIDTH
    n = tt * nb
    first = i == 0

    def carry0():
        return shift0_ref[lora0:, :]

    def prev0(g, q):
        if heads_on_lanes:
            return jnp.zeros((HEAD_DIM, LANES), f32)
        r0 = q * RWKV_WIDTH + g * HEAD_DIM
        return shift0_ref[r0:r0 + HEAD_DIM, :]

    @pl.when(first)
    def _():
        zb_ref[...] = jnp.zeros(zb_ref.shape, f32)
        carry_ref[...] = carry0()
        wc_ref[...] = jnp.ones(wc_ref.shape, f32)
        for g in range(groups):
            for q in range(3):
                prev_ref[g, q] = prev0(g, q)

    def project(z_ref):
        if heads_on_lanes:
            u_ref = rest[0]
            n_col = D_MODEL // LANES
            for bi in range(nb):
                ub = _rmsnorm(x_ref[bi], g_ref[...])
                for ci in range(n_col):
                    u_ref[ci, pl.ds(bi, tt, stride=nb), :] = ub[:, ci * LANES:(ci + 1) * LANES]
            u = jnp.concatenate([u_ref[ci] for ci in range(n_col)], axis=1).astype(bf16)
        else:
            u = _rmsnorm(x_ref[...].reshape(n, D_MODEL), g_ref[...]).astype(bf16)
        z_ref[...] = _dot_nt(wt_ref[...], u)

    def post(z_ref):
        tile = i - 1
        lane = lax.broadcasted_iota(jnp.int32, (2 * LORA, LANES), 1)
        ones = jnp.ones((HEAD_DIM, LANES), f32)
        prev = [[prev_ref[g, q] for q in range(3)] for g in range(groups)]
        wc = [wc_ref[g] for g in range(groups)]
        prev_rot = carry_ref[...]
        steps_per_blk = LANES // nb if heads_on_lanes else 1
        for j in range(n // LANES):
            z = z_ref[:, j * LANES:(j + 1) * LANES]
            zl = z[lora0:]
            if nb % LANES == 0:
                prev_l, zl_rot = prev_rot, zl
            else:
                zl_rot = pltpu.roll(zl, nb, 1)
                prev_l = jnp.where(lane < nb, prev_rot, zl_rot)
            prev_rot = zl_rot
            zls = zl + mul_ref[...] * (prev_l - zl)
            w_pre = w0_ref[...] + _dot(w2t_ref[...], jnp.tanh(zls[:LORA]).astype(bf16))
            decay = jnp.exp(-DECAY_SCALE * _sigmoid(w_pre))
            a_all = _sigmoid(a0_ref[...] + _dot(a2t_ref[...], zls[LORA:].astype(bf16)))
            raw = [z[0:RWKV_WIDTH], z[RWKV_WIDTH:2 * RWKV_WIDTH], z[2 * RWKV_WIDTH:lora0], decay, a_all]
            per_head = [[val[h * HEAD_DIM:(h + 1) * HEAD_DIM] for h in range(RWKV_HEADS)] for val in raw]
            if heads_on_lanes:
                tiles = [_chunk_transpose(ph, nb) for ph in per_head]
            for ls in range(steps_per_blk):
                local = j * steps_per_blk + ls
                for g in range(groups):
                    idx = ls if heads_on_lanes else g
                    r_raw, kx_raw, v_raw, w_t, a_t = (tiles[q][idx] if heads_on_lanes else per_head[q][idx]
                                                      for q in range(5))
                    mu3 = mu3_ref[g]
                    r = r_raw + mu3[0] * (prev[g][0] - r_raw)
                    kx = kx_raw + mu3[1] * (prev[g][1] - kx_raw)
                    v = v_raw + mu3[2] * (prev[g][2] - v_raw)
                    prev[g] = [r_raw, kx_raw, v_raw]
                    kkr = kx * kkp_ref[g]
                    ss = jnp.sum(kkr * kkr, axis=0, keepdims=True)
                    kk = kkr * (1.0 / jnp.maximum(jnp.sqrt(ss), 1e-12))
                    k = kx * (1.0 + (a_t - 1.0) * kap_ref[g])
                    b = kk * a_t
                    wc_in = wc[g]
                    if tt >= period:
                        wc_base = ones if local % period == 0 else wc_in
                    elif local == 0:
                        wc_base = jnp.where(tile % (period // tt) == 0, ones, wc_in)
                    else:
                        wc_base = wc_in
                    wc_t = wc_base * w_t
                    inv_wc = 1.0 / wc_t
                    step = (g, local)
                    out_ref[step + (Q_KK,)] = kk * wc_base
                    out_ref[step + (Q_W,)] = wc_in
                    out_ref[step + (Q_B,)] = b * inv_wc
                    out_ref[step + (Q_K,)] = k * inv_wc
                    out_ref[step + (Q_R,)] = r * wc_t
                    out_ref[step + (Q_V,)] = v
                    wc[g] = wc_t
        carry_ref[...] = jnp.where(first, carry0(), prev_rot)
        shift_ref[...] = z
        for g in range(groups):
            wc_g = jnp.where(first, ones, wc[g])
            wc_ref[g] = wc_g
            wend_ref[g] = wc_g
            for q in range(3):
                prev_ref[g, q] = jnp.where(first, prev0(g, q), prev[g][q])

    @pl.when(i % 2 == 0)
    def _():
        project(za_ref)
        post(zb_ref)

    @pl.when(i % 2 == 1)
    def _():
        project(zb_ref)
        post(za_ref)


def _rwkv_proj(x, g_norm, wt, mul_t, w0_t, a0_t, w2t, a2t, shift0_t, mu3_t, kk_t, ka_t, *,
               tt, heads_on_lanes, period):
    const = lambda shape: pl.BlockSpec(shape, lambda i: (0,) * len(shape))
    groups = 1 if heads_on_lanes else RWKV_HEADS
    if heads_on_lanes:
        nb, T, _ = x.shape
    else:
        T, nb, _ = x.shape
    n_tiles = T // tt
    proj_tile = lambda i: jnp.minimum(i, n_tiles - 1)
    post_tile = lambda i: jnp.maximum(i - 1, 0)
    scratch = [pltpu.VMEM((2 * LORA, LANES), f32), pltpu.VMEM((groups, 3, HEAD_DIM, LANES), f32),
               pltpu.VMEM((groups, HEAD_DIM, LANES), f32),
               pltpu.VMEM((SHIFT_DIM, tt * nb), f32), pltpu.VMEM((SHIFT_DIM, tt * nb), f32)]
    if heads_on_lanes:
        x_spec = pl.BlockSpec((nb, tt, D_MODEL), lambda i: (0, proj_tile(i), 0))
        scratch.append(pltpu.VMEM((D_MODEL // LANES, tt * nb, LANES), f32))
    else:
        x_spec = pl.BlockSpec((tt, nb, D_MODEL), lambda i: (proj_tile(i), 0, 0))
    assert period % tt == 0 or tt % period == 0
    kern = functools.partial(_rwkv_proj_kernel, tt=tt, nb=nb, heads_on_lanes=heads_on_lanes, period=period)
    return pl.pallas_call(
        kern,
        grid=(n_tiles + 1,),
        in_specs=[
            x_spec,
            const((1, D_MODEL)),
            const((SHIFT_DIM, D_MODEL)),
            const((2 * LORA, LANES)),
            const((RWKV_WIDTH, LANES)), const((RWKV_WIDTH, LANES)),
            const((RWKV_WIDTH, LORA)), const((RWKV_WIDTH, LORA)),
            const((SHIFT_DIM, LANES)),
            const((groups, 3, HEAD_DIM, LANES)), const((groups, HEAD_DIM, LANES)),
            const((groups, HEAD_DIM, LANES)),
        ],
        out_specs=[pl.BlockSpec((groups, tt, NQ, HEAD_DIM, LANES), lambda i: (0, post_tile(i), 0, 0, 0)),
                   const((SHIFT_DIM, LANES)), const((groups, HEAD_DIM, LANES))],
        out_shape=[jax.ShapeDtypeStruct((groups, T, NQ, HEAD_DIM, LANES), f32),
                   jax.ShapeDtypeStruct((SHIFT_DIM, LANES), f32),
                   jax.ShapeDtypeStruct((groups, HEAD_DIM, LANES), f32)],
        scratch_shapes=scratch,
        compiler_params=_cparams(1),
        name="rwkv_proj",
    )(x, g_norm, wt, mul_t, w0_t, a0_t, w2t, a2t, shift0_t, mu3_t, kk_t, ka_t)


def _rope(x, c, a, b):
    return x * c + pltpu.roll(x, LANES - ROPE_DIM // 2, 1) * a + pltpu.roll(x, ROPE_DIM // 2, 1) * b


def _nat_proj_kernel(x_ref, g_ref, w_ref, rc_ref, ra_ref, rb_ref,
                     sgr_ref, q_ref, k_ref, v_ref, sga_ref):
    u = _rmsnorm(x_ref[...], g_ref[...]).astype(bf16)
    z = _dot(u, w_ref[...])
    o_q = RWKV_WIDTH
    o_k = o_q + ATTN_WIDTH
    o_v = o_k + KV_WIDTH
    o_g = o_v + KV_WIDTH
    gr = z[:, :o_q]
    sgr_ref[...] = gr * _sigmoid(gr)
    rc, ra, rb = rc_ref[...], ra_ref[...], rb_ref[...]
    for j in range(ATTN_WIDTH // LANES):
        qj = z[:, o_q + j * LANES:o_q + (j + 1) * LANES]
        q_ref[:, j * LANES:(j + 1) * LANES] = (_rope(qj, rc, ra, rb) * (HEAD_DIM ** -0.5)).astype(q_ref.dtype)
    k_ref[...] = _rope(z[:, o_k:o_v], rc, ra, rb)
    v_ref[...] = z[:, o_v:o_g]
    ga = z[:, o_g:]
    sga_ref[...] = ga * _sigmoid(ga)


def _nat_proj(x2d, g_norm, w_nat, rope_c, rope_a, rope_b, *, tm, q_dtype):
    n_tok = x2d.shape[0]
    n_tab = rope_c.shape[0] // tm
    const = lambda shape: pl.BlockSpec(shape, lambda i: (0,) * len(shape))
    row = lambda w: pl.BlockSpec((tm, w), lambda i: (i, 0))
    tab = pl.BlockSpec((tm, LANES), lambda i: (i % n_tab, 0))
    return pl.pallas_call(
        _nat_proj_kernel,
        grid=(n_tok // tm,),
        in_specs=[row(D_MODEL), const((1, D_MODEL)), const((D_MODEL, NAT_DIM)), tab, tab, tab],
        out_specs=[row(RWKV_WIDTH), row(ATTN_WIDTH), row(KV_WIDTH), row(KV_WIDTH), row(ATTN_WIDTH)],
        out_shape=[jax.ShapeDtypeStruct((n_tok, w), dt) for w, dt in
                   ((RWKV_WIDTH, f32), (ATTN_WIDTH, q_dtype), (KV_WIDTH, f32), (KV_WIDTH, f32),
                    (ATTN_WIDTH, f32))],
        compiler_params=_cparams(1),
        name="nat_proj",
    )(x2d, g_norm, w_nat, rope_c, rope_a, rope_b)


def _row_bcast(ref, idx, k):
    return jnp.broadcast_to(ref[idx + (pl.ds(k, 1), slice(None))], (HEAD_DIM, LANES))


def _wkv_step(s_ref, ref, at, at_next, sa, rk, lnw, lnb):
    vv = ref[at + (Q_V,)]
    y = jnp.zeros((HEAD_DIM, LANES), f32)
    sa_next = jnp.zeros((HEAD_DIM, LANES), f32)
    for k in range(HEAD_DIM):
        s_new = (s_ref[k] - sa * _row_bcast(ref, at + (Q_B,), k)
                 + vv * _row_bcast(ref, at + (Q_K,), k))
        s_ref[k] = s_new
        y = y + s_new * _row_bcast(ref, at + (Q_R,), k)
        sa_next = sa_next + s_new * _row_bcast(ref, at_next + (Q_KK,), k)
    mean = jnp.mean(y, axis=0, keepdims=True)
    d = y - mean
    var = jnp.mean(d * d, axis=0, keepdims=True)
    yn = d * lax.rsqrt(var + GN_EPS) * lnw + lnb
    rkk = jnp.sum(ref[at + (Q_R,)] * ref[at + (Q_K,)] * rk, axis=0, keepdims=True)
    return yn + rkk * vv, sa_next


def _wkv_scan_kernel(in_ref, s0_ref, wend_ref, rk_ref, lnw_ref, lnb_ref, o_ref, sout_ref, s_ref, *rest,
                     tc, period, natural_out):
    c = pl.program_id(1)

    @pl.when(c == 0)
    def _():
        s_ref[...] = s0_ref[0]

    obuf_ref = rest[0] if natural_out else None
    rk, lnw, lnb = rk_ref[0], lnw_ref[0], lnb_ref[0]

    for t0 in range(0, tc, period):
        sa0 = jnp.zeros((HEAD_DIM, LANES), f32)
        for k in range(HEAD_DIM):
            s_k = s_ref[k] * _row_bcast(in_ref, (0, t0, Q_W), k)
            s_ref[k] = s_k
            sa0 = sa0 + s_k * _row_bcast(in_ref, (0, t0, Q_KK), k)

        def step(t, sa, last=t0 + period - 1):
            t_next = jnp.minimum(t + 1, last)
            o, sa_next = _wkv_step(s_ref, in_ref, (0, t), (0, t_next), sa, rk, lnw, lnb)
            if natural_out:
                obuf_ref[t] = o
            else:
                o_ref[0, t] = o
            return sa_next

        lax.fori_loop(t0, t0 + period, step, sa0)

    if natural_out:
        nb = LANES // RWKV_HEADS
        for j in range(tc // GROUP_T):
            per_h = _chunk_transpose([obuf_ref[j * GROUP_T + t] for t in range(GROUP_T)], nb)
            ot = jnp.concatenate(per_h, axis=0)
            o_ref[j * GROUP_T:(j + 1) * GROUP_T] = ot.T.reshape(GROUP_T, nb, RWKV_WIDTH)

    @pl.when(c == pl.num_programs(1) - 1)
    def _():
        for k in range(HEAD_DIM):
            sout_ref[0, k] = s_ref[k] * _row_bcast(wend_ref, (0,), k)


def _wkv_scan(scan_in, s0, wend, rk_t, lnw_t, lnb_t, *, tc, period, natural_out):
    groups, T = scan_in.shape[:2]
    assert tc % period == 0
    tile = pl.BlockSpec((1, HEAD_DIM, LANES), lambda g, c: (g, 0, 0))
    state = pl.BlockSpec((1, HEAD_DIM, HEAD_DIM, LANES), lambda g, c: (g, 0, 0, 0))
    scratch = [pltpu.VMEM((HEAD_DIM, HEAD_DIM, LANES), f32)]
    if natural_out:
        nb = LANES // RWKV_HEADS
        o_spec = pl.BlockSpec((tc, nb, RWKV_WIDTH), lambda g, c: (c, 0, 0))
        o_shape = jax.ShapeDtypeStruct((T, nb, RWKV_WIDTH), f32)
        scratch.append(pltpu.VMEM((tc, HEAD_DIM, LANES), f32))
    else:
        o_spec = pl.BlockSpec((1, tc, HEAD_DIM, LANES), lambda g, c: (g, c, 0, 0))
        o_shape = jax.ShapeDtypeStruct((groups, T, HEAD_DIM, LANES), f32)
    return pl.pallas_call(
        functools.partial(_wkv_scan_kernel, tc=tc, period=period, natural_out=natural_out),
        grid=(groups, T // tc),
        in_specs=[
            pl.BlockSpec((1, tc, NQ, HEAD_DIM, LANES), lambda g, c: (g, c, 0, 0, 0)),
            state, tile, tile, tile, tile,
        ],
        out_specs=[o_spec, state],
        out_shape=[o_shape, jax.ShapeDtypeStruct((groups, HEAD_DIM, HEAD_DIM, LANES), f32)],
        scratch_shapes=scratch,
        compiler_params=_cparams(2),
        name="wkv_scan",
    )(scan_in, s0, wend, rk_t, lnw_t, lnb_t)


def _swa_attn_kernel(q_ref, kp_ref, kc_ref, vp_ref, vc_ref, sga_ref, sink_ref, o_ref, *,
                     bb, tq, first_block_has_no_prev):
    n = pl.program_id(1)
    nr = bb * tq
    half = LANES // 2
    tq_bits = tq.bit_length() - 1

    def key_mask(n_keys_per_batch, is_prev):
        rows = lax.broadcasted_iota(jnp.int32, (nr, bb * n_keys_per_batch), 0)
        cols = lax.broadcasted_iota(jnp.int32, (nr, bb * n_keys_per_batch), 1)
        i = rows & (tq - 1)
        j = cols & (n_keys_per_batch - 1)
        ok = (j > i) if is_prev else (j <= i)
        if bb > 1:
            same = (rows >> tq_bits) == (cols >> (n_keys_per_batch.bit_length() - 1))
            ok = jnp.logical_and(same, ok)
        if is_prev and first_block_has_no_prev:
            ok = jnp.logical_and(ok, n > 0)
        return jnp.tile(ok, (ATTN_GROUP, 1))

    mask_p = key_mask(WINDOW, True)
    mask_c = key_mask(tq, False)
    lane_q = lax.broadcasted_iota(jnp.int32, (nr, LANES), 1)
    lane_kp = lax.broadcasted_iota(jnp.int32, (bb * WINDOW, LANES), 1)

    kp = kp_ref[...].reshape(bb * WINDOW, LANES).astype(bf16)
    kc = kc_ref[...].reshape(nr, LANES).astype(bf16)
    vp = vp_ref[...].reshape(bb * WINDOW, LANES)
    vc = vc_ref[...].reshape(nr, LANES)

    res = []
    for g in range(ATTN_KV_HEADS):
        in_g = (lambda lane: lane < half) if g == 0 else (lambda lane: lane >= half)
        qs, sinks = [], []
        for hh in range(ATTN_GROUP):
            x = q_ref[:, :, hh * LANES:(hh + 1) * LANES].reshape(nr, LANES)
            qs.append(jnp.where(in_g(lane_q), x, 0.0).astype(bf16))
            sinks.append(jnp.broadcast_to(sink_ref[g * ATTN_GROUP + hh:g * ATTN_GROUP + hh + 1, :], (nr, LANES)))
        qg = jnp.concatenate(qs, axis=0)
        sink = jnp.concatenate(sinks, axis=0)[:, 0:1]
        sp = jnp.where(mask_p, _dot_nt(qg, kp), NEG_INF)
        sc = jnp.where(mask_c, _dot_nt(qg, kc), NEG_INF)
        m = jnp.maximum(jnp.maximum(jnp.max(sp, axis=-1, keepdims=True),
                                    jnp.max(sc, axis=-1, keepdims=True)), sink)
        pp = jnp.exp(sp - m).astype(bf16)
        pc = jnp.exp(sc - m).astype(bf16)
        e_sink = jnp.exp(sink - m)
        vpg = jnp.where(in_g(lane_kp), vp, 1.0).astype(bf16)
        vcg = jnp.where(in_g(lane_q), vc, 1.0).astype(bf16)
        pv = _dot(pp, vpg) + _dot(pc, vcg)
        res.append(pv * (1.0 / (pltpu.roll(pv, half, 1) + e_sink)))
    for hh in range(ATTN_GROUP):
        blk = slice(hh * nr, (hh + 1) * nr)
        cs = slice(hh * LANES, (hh + 1) * LANES)
        out = jnp.where(lane_q < half, res[0][blk], res[1][blk])
        o_ref[:, :, cs] = (out.reshape(bb, tq, LANES) * sga_ref[:, :, cs]).astype(o_ref.dtype)


def _swa_prompt_kernel(q_ref, kp_ref, kc_ref, vp_ref, vc_ref, sga_ref, sink_ref, o_ref, *, n_blk):
    n = pl.program_id(1)
    tq = WINDOW
    half = LANES // 2
    keys = lax.broadcasted_iota(jnp.int32, (2 * WINDOW, tq), 0)
    qi = lax.broadcasted_iota(jnp.int32, (2 * WINDOW, tq), 1)
    prev_ok = jnp.logical_and(keys < WINDOW, keys > qi)
    cur_ok = jnp.logical_and(keys >= WINDOW, keys - WINDOW <= qi)
    lane_q = lax.broadcasted_iota(jnp.int32, (tq, LANES), 1)
    lane_k = lax.broadcasted_iota(jnp.int32, (2 * WINDOW, LANES), 1)

    for blk in range(n_blk):
        rows = slice(blk * tq, (blk + 1) * tq)
        if blk == 0:
            kp, vp = kp_ref[0], vp_ref[0]
            ok = jnp.logical_or(jnp.logical_and(prev_ok, n > 0), cur_ok)
        else:
            before = slice((blk - 1) * tq, blk * tq)
            kp, vp = kc_ref[0, before, :], vc_ref[0, before, :]
            ok = jnp.logical_or(prev_ok, cur_ok)
        mask_t = jnp.tile(ok, (1, ATTN_GROUP))
        k_all = jnp.concatenate([kp, kc_ref[0, rows, :]], axis=0).astype(bf16)
        v_all = jnp.concatenate([vp, vc_ref[0, rows, :]], axis=0)

        norm_t = []
        for g in range(ATTN_KV_HEADS):
            in_g = (lambda lane: lane < half) if g == 0 else (lambda lane: lane >= half)
            qs, sinks = [], []
            for hh in range(ATTN_GROUP):
                x = q_ref[0, rows, hh * LANES:(hh + 1) * LANES]
                qs.append(jnp.where(in_g(lane_q), x, jnp.zeros_like(x)).astype(bf16))
                sinks.append(sink_ref[g * ATTN_GROUP + hh:g * ATTN_GROUP + hh + 1, :])
            qg = jnp.concatenate(qs, axis=0)
            sink = jnp.concatenate(sinks, axis=1)
            st = jnp.where(mask_t, _dot_nt(k_all, qg), NEG_INF)
            m = jnp.maximum(jnp.max(st, axis=0, keepdims=True), sink)
            p = jnp.exp(st - m).astype(bf16)
            e_sink = jnp.exp(sink - m)
            vg = jnp.where(in_g(lane_k), v_all, 1.0).astype(bf16)
            ot = lax.dot_general(vg, p, (((0,), (0,)), ((), ())), preferred_element_type=f32)
            lo, hi = ot[:half], ot[half:]
            num, den = (lo, hi) if g == 0 else (hi, lo)
            norm_t.append(num * (1.0 / (den + e_sink)))
        for hh in range(ATTN_GROUP):
            cs = slice(hh * LANES, (hh + 1) * LANES)
            blk_t = jnp.concatenate([norm_t[0][:, hh * tq:(hh + 1) * tq],
                                     norm_t[1][:, hh * tq:(hh + 1) * tq]], axis=0)
            o_ref[0, rows, cs] = (blk_t.T * sga_ref[0, rows, cs]).astype(o_ref.dtype)


def _swa_attn(q, k_prev, k_cur, v_prev, v_cur, sga, sink_t, *, bb, tq, prev_is_same_array, out_dtype):
    B, T, _ = q.shape
    nq = T // tq
    cur = lambda w: pl.BlockSpec((bb, tq, w), lambda bi, n: (bi, n, 0))
    if prev_is_same_array:
        assert bb == 1 and tq % WINDOW == 0
        n_blk = tq // WINDOW
        prev = pl.BlockSpec((bb, WINDOW, KV_WIDTH), lambda bi, n: (bi, jnp.maximum(n * n_blk - 1, 0), 0))
        kern = functools.partial(_swa_prompt_kernel, n_blk=n_blk)
    else:
        prev = pl.BlockSpec((bb, WINDOW, KV_WIDTH), lambda bi, n: (bi, 0, 0))
        kern = functools.partial(_swa_attn_kernel, bb=bb, tq=tq, first_block_has_no_prev=False)
    return pl.pallas_call(
        kern,
        grid=(B // bb, nq),
        in_specs=[cur(ATTN_WIDTH), prev, cur(KV_WIDTH), prev, cur(KV_WIDTH), cur(ATTN_WIDTH),
                  pl.BlockSpec((ATTN_Q_HEADS, LANES), lambda bi, n: (0, 0))],
        out_specs=cur(ATTN_WIDTH),
        out_shape=jax.ShapeDtypeStruct((B, T, ATTN_WIDTH), out_dtype),
        compiler_params=_cparams(2),
        name="swa_attn",
    )(q, k_prev, k_cur, v_prev, v_cur, sga, sink_t)


def _out_mix_kernel(x_ref, or_ref, sgr_ref, ma_ref, p_ref, wo_ref, gp_ref, wpg_ref, wpp_ref, gf_ref,
                    y_ref):
    mr = (or_ref[...] * sgr_ref[...]).astype(bf16)
    ma = ma_ref[...].astype(bf16)
    h = (x_ref[...] + _dot(mr, wo_ref[:RWKV_WIDTH, :]) + _dot(ma, wo_ref[RWKV_WIDTH:, :]))
    gate = _sigmoid(_dot(_rmsnorm(h, gp_ref[...]).astype(bf16), wpg_ref[...]))
    h = h + gate * _dot(p_ref[...].astype(bf16), wpp_ref[...])
    y_ref[...] = _rmsnorm(h, gf_ref[...])


def _out_mix(x2d, o_r, sgr, ma, p2d, w_out, g_ple, w_pg, w_pp, g_final, *, tm):
    n_tok = x2d.shape[0]
    const = lambda shape: pl.BlockSpec(shape, lambda i: (0,) * len(shape))
    row = lambda w: pl.BlockSpec((tm, w), lambda i: (i, 0))
    return pl.pallas_call(
        _out_mix_kernel,
        grid=(n_tok // tm,),
        in_specs=[row(D_MODEL), row(RWKV_WIDTH), row(RWKV_WIDTH), row(ATTN_WIDTH), row(PLE_DIM),
                  const((D_MODEL, D_MODEL)), const((1, D_MODEL)), const((D_MODEL, D_MODEL)),
                  const((PLE_DIM, D_MODEL)), const((1, D_MODEL))],
        out_specs=row(D_MODEL),
        out_shape=jax.ShapeDtypeStruct((n_tok, D_MODEL), f32),
        compiler_params=_cparams(1),
        name="out_mix",
    )(x2d, o_r, sgr, ma, p2d, w_out, g_ple, w_pg, w_pp, g_final)


def _rope_tables(pos):
    half = ROPE_DIM // 2
    inv = ROPE_THETA ** (-jnp.arange(half, dtype=f32) / half)
    ang = pos.astype(f32)[:, None] * inv[None, :]
    cos, sin = jnp.cos(ang), jnp.sin(ang)
    n = pos.shape[0]
    ones = jnp.ones((n, HEAD_DIM - ROPE_DIM), f32)
    zeros = jnp.zeros((n, HEAD_DIM - ROPE_DIM), f32)
    zh = jnp.zeros((n, half), f32)
    c = jnp.concatenate([cos, cos, ones], axis=1)
    a = jnp.concatenate([-sin, zh, zeros], axis=1)
    b = jnp.concatenate([zh, sin, zeros], axis=1)
    rep = LANES // HEAD_DIM
    return tuple(jnp.tile(t, (1, rep)) for t in (c, a, b))


def _col_tile(vec):
    return jnp.broadcast_to(vec[:, None], (vec.shape[0], LANES))


def _lane_param_tile(vec, heads_on_lanes):
    ph = vec.reshape(RWKV_HEADS, HEAD_DIM)
    if heads_on_lanes:
        nb = LANES // RWKV_HEADS
        return jnp.repeat(ph.T, nb, axis=1)[None]
    return jnp.broadcast_to(ph[:, :, None], (RWKV_HEADS, HEAD_DIM, LANES))


def _layer(x, p, pos, s0, shift0, kbuf, vbuf, wts, *, heads_on_lanes, tt, tc, period, tm, att_bb, att_tq):
    B, T, _ = x.shape
    n_tok = B * T
    x2d = x.reshape(n_tok, D_MODEL)

    if shift0 is None:
        shift0_t = jnp.zeros((SHIFT_DIM, LANES), f32)
    else:
        shift0_t = shift0.T
    mu = wts["mu"]
    mu3_t = jnp.stack([_lane_param_tile(mu[q * RWKV_WIDTH:(q + 1) * RWKV_WIDTH], heads_on_lanes)
                       for q in range(3)], axis=1)
    tile = lambda name: _lane_param_tile(wts[name], heads_on_lanes)
    proj_args = (wts["g_norm"], wts["wt_rwkv"], _col_tile(mu[3 * RWKV_WIDTH:]), wts["w0_t"], wts["a0_t"],
                 wts["w2t"], wts["a2t"], shift0_t, mu3_t, tile("k_k"), tile("k_a"))
    scan_params = (tile("r_k"), tile("ln_w"), tile("ln_b"))
    if heads_on_lanes:
        assert s0 is None and shift0 is None
        scan_in, shift_t, wend = _rwkv_proj(x, *proj_args, tt=tt, heads_on_lanes=True, period=period)
        shift_new = shift_t[:, LANES - B:].T
        s0_t = jnp.zeros((1, HEAD_DIM, HEAD_DIM, LANES), f32)
        o_tb, s_fin = _wkv_scan(scan_in, s0_t, wend, *scan_params, tc=tc, period=period, natural_out=True)
        o_r = jnp.swapaxes(o_tb, 0, 1).reshape(n_tok, RWKV_WIDTH)
        s_new = jnp.transpose(s_fin[0].reshape(HEAD_DIM, HEAD_DIM, RWKV_HEADS, B), (3, 2, 1, 0))
    else:
        scan_in, shift_t, wend = _rwkv_proj(jnp.swapaxes(x, 0, 1), *proj_args, tt=tt, heads_on_lanes=False,
                                            period=period)
        shift_new = shift_t.T
        s0_t = jnp.transpose(s0, (1, 3, 2, 0))
        o_scan, s_fin = _wkv_scan(scan_in, s0_t, wend, *scan_params, tc=tc, period=period, natural_out=False)
        o_r = jnp.transpose(o_scan, (3, 1, 0, 2)).reshape(n_tok, RWKV_WIDTH)
        s_new = jnp.transpose(s_fin, (3, 0, 2, 1))

    rope_c, rope_a, rope_b = _rope_tables(pos)
    if T < tm:
        reps = tm // T
        rope_c, rope_a, rope_b = (jnp.tile(t, (reps, 1)) for t in (rope_c, rope_a, rope_b))
    act_dtype = bf16 if heads_on_lanes else f32
    sgr, q, k, v, sga = _nat_proj(x2d, wts["g_norm"], wts["w_nat"], rope_c, rope_a, rope_b, tm=tm,
                                  q_dtype=act_dtype)
    q3 = q.reshape(B, T, ATTN_WIDTH)
    k3 = k.reshape(B, T, KV_WIDTH)
    v3 = v.reshape(B, T, KV_WIDTH)
    sga3 = sga.reshape(B, T, ATTN_WIDTH)
    if kbuf is None:
        ma = _swa_attn(q3, k3, k3, v3, v3, sga3, wts["sink_t"], bb=att_bb, tq=att_tq,
                       prev_is_same_array=True, out_dtype=act_dtype)
        k_new = k3[:, T - WINDOW:].reshape(B, WINDOW, ATTN_KV_HEADS, HEAD_DIM)
        v_new = v3[:, T - WINDOW:].reshape(B, WINDOW, ATTN_KV_HEADS, HEAD_DIM)
    else:
        kb = kbuf.reshape(B, WINDOW, KV_WIDTH)
        vb = vbuf.reshape(B, WINDOW, KV_WIDTH)
        ma = _swa_attn(q3, kb, k3, vb, v3, sga3, wts["sink_t"], bb=att_bb, tq=att_tq,
                       prev_is_same_array=False, out_dtype=act_dtype)
        k_new = jnp.concatenate([kb, k3], axis=1)[:, -WINDOW:].reshape(B, WINDOW, ATTN_KV_HEADS, HEAD_DIM)
        v_new = jnp.concatenate([vb, v3], axis=1)[:, -WINDOW:].reshape(B, WINDOW, ATTN_KV_HEADS, HEAD_DIM)

    y = _out_mix(x2d, o_r, sgr, ma.reshape(n_tok, ATTN_WIDTH), p.reshape(n_tok, PLE_DIM),
                 wts["w_out"], wts["g_ple"], wts["w_pg"], wts["w_pp"], wts["g_final"], tm=tm)
    return y.reshape(B, T, D_MODEL), s_new, shift_new, k_new, v_new


def kernel(x_prompt, x_sample, state_rwkv_wkv, state_rwkv_shift, cache_swa_k, cache_swa_v,
           p_prompt, p_sample, g_norm, w_in, mu_shift, w0, w2, a0, a2, k_k, k_a, r_k,
           ln_w, ln_b, sinks, w_out, g_ple, w_ple_gate, w_ple_proj, g_final):
    assert w_in.shape[0] == 1, "single layer"
    w_in0 = w_in[0]
    head_order = [g * ATTN_GROUP + i for i in range(ATTN_GROUP) for g in range(ATTN_KV_HEADS)]
    cols = jnp.concatenate([jnp.arange(h * HEAD_DIM, (h + 1) * HEAD_DIM) for h in head_order])
    o_q = SHIFT_DIM + RWKV_WIDTH
    o_ga = o_q + ATTN_WIDTH + 2 * KV_WIDTH
    w_nat = jnp.concatenate([w_in0[:, SHIFT_DIM:o_q], w_in0[:, o_q + cols], w_in0[:, o_q + ATTN_WIDTH:o_ga],
                             w_in0[:, o_ga + cols]], axis=1)
    w_out0 = jnp.concatenate([w_out[0][:RWKV_WIDTH], w_out[0][RWKV_WIDTH + cols]], axis=0)
    wts = {
        "g_norm": g_norm[0][None, :],
        "wt_rwkv": w_in0[:, :SHIFT_DIM].T.astype(bf16),
        "w_nat": w_nat.astype(bf16),
        "mu": mu_shift[0],
        "w0_t": _col_tile(w0[0]), "a0_t": _col_tile(a0[0]),
        "k_k": k_k[0], "k_a": k_a[0],
        "w2t": w2[0].T.astype(bf16), "a2t": a2[0].T.astype(bf16),
        "r_k": r_k[0], "ln_w": ln_w[0], "ln_b": ln_b[0],
        "sink_t": jnp.broadcast_to(sinks[0][:, None], (ATTN_Q_HEADS, LANES)),
        "w_out": w_out0.astype(bf16), "g_ple": g_ple[0][None, :],
        "w_pg": w_ple_gate[0].astype(bf16), "w_pp": w_ple_proj[0].astype(bf16),
        "g_final": g_final[None, :],
    }
    Bp, Tp, _ = x_prompt.shape
    Bs, Ts, _ = x_sample.shape
    assert Bp * RWKV_HEADS == LANES and Bs == LANES and Tp % WINDOW == 0 and Ts % SUBLANES == 0

    yp, s1, sh1, k1, v1 = _layer(x_prompt, p_prompt[0], jnp.arange(Tp), None, None, None, None, wts,
                                 heads_on_lanes=True, tt=16, tc=32, period=32, tm=1024, att_bb=1, att_tq=4 * WINDOW)
    ys, s2, sh2, k2, v2 = _layer(x_sample, p_sample[0], PAST_LEN + jnp.arange(Ts),
                                 state_rwkv_wkv[0], state_rwkv_shift[0], cache_swa_k[0], cache_swa_v[0],
                                 wts, heads_on_lanes=False, tt=2, tc=Ts, period=Ts, tm=Bs * Ts, att_bb=8, att_tq=Ts)
    return (yp, ys, s1[None], sh1[None], k1[None], v1[None], s2[None], sh2[None], k2[None], v2[None])
```

```python
import functools
import math

import jax
import jax.numpy as jnp
from jax import lax
from jax.experimental import pallas as pl
from jax.experimental.pallas import tpu as pltpu

D_MODEL = 1024
HEAD_DIM = 64
RWKV_WIDTH = 512
RWKV_HEADS = 8
ATTN_WIDTH = 512
ATTN_Q_HEADS = 8
ATTN_KV_HEADS = 2
ATTN_GROUP = 4
KV_WIDTH = 128
LORA = 64
WINDOW = 128
ROPE_THETA = 500000.0
ROPE_DIM = 16
PLE_DIM = 256
NORM_EPS = 1e-6
GN_EPS = 64e-5
NEG_INF = -1e30
PAST_LEN = 16384
SHIFT_DIM = 3 * RWKV_WIDTH + 2 * LORA
NAT_DIM = RWKV_WIDTH + ATTN_WIDTH + 2 * KV_WIDTH + ATTN_WIDTH

LANES = 128
SUBLANES = 8
VMEM_LIMIT = 56 * 1024 * 1024
DECAY_SCALE = math.exp(-0.5)
GROUP_T = 8

Q_KK, Q_W, Q_B, Q_K, Q_R, Q_V = range(6)
NQ = 6

f32 = jnp.float32
bf16 = jnp.bfloat16


def _cparams(n_axes):
    return pltpu.CompilerParams(dimension_semantics=("arbitrary",) * n_axes,
                                vmem_limit_bytes=VMEM_LIMIT)


def _rmsnorm(x, g):
    ms = jnp.mean(x * x, axis=-1, keepdims=True)
    return x * lax.rsqrt(ms + NORM_EPS) * g


def _sigmoid(x):
    return 1.0 / (1.0 + jnp.exp(-x))


def _dot_nt(a, b):
    return lax.dot_general(a, b, (((1,), (1,)), ((), ())), preferred_element_type=f32)


def _dot(a, b):
    return jnp.dot(a, b, preferred_element_type=f32)


def _chunk_transpose(xs, chunk):
    lane = lax.broadcasted_iota(jnp.int32, xs[0].shape, 1)
    xs = list(xs)
    for d in (4, 2, 1):
        hi_lanes = (lane & (chunk * d)) != 0
        nxt = list(xs)
        for i in range(8):
            if i & d:
                continue
            lo, hi = xs[i], xs[i + d]
            if 2 * chunk * d == LANES:
                moved = pltpu.roll(jnp.where(hi_lanes, lo, hi), chunk * d, 1)
                nxt[i] = jnp.where(hi_lanes, moved, lo)
                nxt[i + d] = jnp.where(hi_lanes, hi, moved)
            else:
                nxt[i] = jnp.where(hi_lanes, pltpu.roll(hi, chunk * d, 1), lo)
                nxt[i + d] = jnp.where(hi_lanes, hi, pltpu.roll(lo, LANES - chunk * d, 1))
        xs = nxt
    return xs


def _rwkv_proj_kernel(x_ref, g_ref, wt_ref, mul_ref, w0_ref, a0_ref, w2t_ref, a2t_ref, shift0_ref,
                      mu3_ref, kkp_ref, kap_ref, out_ref, shift_ref, wend_ref,
                      carry_ref, prev_ref, wc_ref, za_ref, zb_ref, *rest, tt, nb, heads_on_lanes, period):
    i = pl.program_id(0)
    groups = 1 if heads_on_lanes else RWKV_HEADS
    lora0 = 3 * RWKV_WIDTH
    n = tt * nb
    first = i == 0

    def carry0():
        return shift0_ref[lora0:, :]

    def prev0(g, q):
        if heads_on_lanes:
            return jnp.zeros((HEAD_DIM, LANES), f32)
        r0 = q * RWKV_WIDTH + g * HEAD_DIM
        return shift0_ref[r0:r0 + HEAD_DIM, :]

    @pl.when(first)
    def _():
        zb_ref[...] = jnp.zeros(zb_ref.shape, f32)
        carry_ref[...] = carry0()
        wc_ref[...] = jnp.ones(wc_ref.shape, f32)
        for g in range(groups):
            for q in range(3):
                prev_ref[g, q] = prev0(g, q)

    def project(z_ref):
        if heads_on_lanes:
            u_ref = rest[0]
            n_col = D_MODEL // LANES
            for bi in range(nb):
                ub = _rmsnorm(x_ref[bi], g_ref[...])
                for ci in range(n_col):
                    u_ref[ci, pl.ds(bi, tt, stride=nb), :] = ub[:, ci * LANES:(ci + 1) * LANES]
            u = jnp.concatenate([u_ref[ci] for ci in range(n_col)], axis=1).astype(bf16)
        else:
            u = _rmsnorm(x_ref[...].reshape(n, D_MODEL), g_ref[...]).astype(bf16)
        z_ref[...] = _dot_nt(wt_ref[...], u)

    def post(z_ref):
        tile = i - 1
        lane = lax.broadcasted_iota(jnp.int32, (2 * LORA, LANES), 1)
        ones = jnp.ones((HEAD_DIM, LANES), f32)
        prev = [[prev_ref[g, q] for q in range(3)] for g in range(groups)]
        wc = [wc_ref[g] for g in range(groups)]
        prev_rot = carry_ref[...]
        steps_per_blk = LANES // nb if heads_on_lanes else 1
        for j in range(n // LANES):
            z = z_ref[:, j * LANES:(j + 1) * LANES]
            zl = z[lora0:]
            if nb % LANES == 0:
                prev_l, zl_rot = prev_rot, zl
            else:
                zl_rot = pltpu.roll(zl, nb, 1)
                prev_l = jnp.where(lane < nb, prev_rot, zl_rot)
            prev_rot = zl_rot
            zls = zl + mul_ref[...] * (prev_l - zl)
            w_pre = w0_ref[...] + _dot(w2t_ref[...], jnp.tanh(zls[:LORA]).astype(bf16))
            decay = jnp.exp(-DECAY_SCALE * _sigmoid(w_pre))
            a_all = _sigmoid(a0_ref[...] + _dot(a2t_ref[...], zls[LORA:].astype(bf16)))
            raw = [z[0:RWKV_WIDTH], z[RWKV_WIDTH:2 * RWKV_WIDTH], z[2 * RWKV_WIDTH:lora0], decay, a_all]
            per_head = [[val[h * HEAD_DIM:(h + 1) * HEAD_DIM] for h in range(RWKV_HEADS)] for val in raw]
            if heads_on_lanes:
                tiles = [_chunk_transpose(ph, nb) for ph in per_head]
            for ls in range(steps_per_blk):
                local = j * steps_per_blk + ls
                for g in range(groups):
                    idx = ls if heads_on_lanes else g
                    r_raw, kx_raw, v_raw, w_t, a_t = (tiles[q][idx] if heads_on_lanes else per_head[q][idx]
                                                      for q in range(5))
                    mu3 = mu3_ref[g]
                    r = r_raw + mu3[0] * (prev[g][0] - r_raw)
                    kx = kx_raw + mu3[1] * (prev[g][1] - kx_raw)
                    v = v_raw + mu3[2] * (prev[g][2] - v_raw)
                    prev[g] = [r_raw, kx_raw, v_raw]
                    kkr = kx * kkp_ref[g]
                    ss = jnp.sum(kkr * kkr, axis=0, keepdims=True)
                    kk = kkr * (1.0 / jnp.maximum(jnp.sqrt(ss), 1e-12))
                    k = kx * (1.0 + (a_t - 1.0) * kap_ref[g])
                    b = kk * a_t
                    wc_in = wc[g]
                    if tt >= period:
                        wc_base = ones if local % period == 0 else wc_in
                    elif local == 0:
                        wc_base = jnp.where(tile % (period // tt) == 0, ones, wc_in)
                    else:
                        wc_base = wc_in
                    wc_t = wc_base * w_t
                    inv_wc = 1.0 / wc_t
                    step = (g, local)
                    out_ref[step + (Q_KK,)] = kk * wc_base
                    out_ref[step + (Q_W,)] = wc_in
                    out_ref[step + (Q_B,)] = b * inv_wc
                    out_ref[step + (Q_K,)] = k * inv_wc
                    out_ref[step + (Q_R,)] = r * wc_t
                    out_ref[step + (Q_V,)] = v
                    wc[g] = wc_t
        carry_ref[...] = jnp.where(first, carry0(), prev_rot)
        shift_ref[...] = z
        for g in range(groups):
            wc_g = jnp.where(first, ones, wc[g])
            wc_ref[g] = wc_g
            wend_ref[g] = wc_g
            for q in range(3):
                prev_ref[g, q] = jnp.where(first, prev0(g, q), prev[g][q])

    @pl.when(i % 2 == 0)
    def _():
        project(za_ref)
        post(zb_ref)

    @pl.when(i % 2 == 1)
    def _():
        project(zb_ref)
        post(za_ref)


def _rwkv_proj(x, g_norm, wt, mul_t, w0_t, a0_t, w2t, a2t, shift0_t, mu3_t, kk_t, ka_t, *,
               tt, heads_on_lanes, period):
    const = lambda shape: pl.BlockSpec(shape, lambda i: (0,) * len(shape))
    groups = 1 if heads_on_lanes else RWKV_HEADS
    if heads_on_lanes:
        nb, T, _ = x.shape
    else:
        T, nb, _ = x.shape
    n_tiles = T // tt
    proj_tile = lambda i: jnp.minimum(i, n_tiles - 1)
    post_tile = lambda i: jnp.maximum(i - 1, 0)
    scratch = [pltpu.VMEM((2 * LORA, LANES), f32), pltpu.VMEM((groups, 3, HEAD_DIM, LANES), f32),
               pltpu.VMEM((groups, HEAD_DIM, LANES), f32),
               pltpu.VMEM((SHIFT_DIM, tt * nb), f32), pltpu.VMEM((SHIFT_DIM, tt * nb), f32)]
    if heads_on_lanes:
        x_spec = pl.BlockSpec((nb, tt, D_MODEL), lambda i: (0, proj_tile(i), 0))
        scratch.append(pltpu.VMEM((D_MODEL // LANES, tt * nb, LANES), f32))
    else:
        x_spec = pl.BlockSpec((tt, nb, D_MODEL), lambda i: (proj_tile(i), 0, 0))
    assert period % tt == 0 or tt % period == 0
    kern = functools.partial(_rwkv_proj_kernel, tt=tt, nb=nb, heads_on_lanes=heads_on_lanes, period=period)
    return pl.pallas_call(
        kern,
        grid=(n_tiles + 1,),
        in_specs=[
            x_spec,
            const((1, D_MODEL)),
            const((SHIFT_DIM, D_MODEL)),
            const((2 * LORA, LANES)),
            const((RWKV_WIDTH, LANES)), const((RWKV_WIDTH, LANES)),
            const((RWKV_WIDTH, LORA)), const((RWKV_WIDTH, LORA)),
            const((SHIFT_DIM, LANES)),
            const((groups, 3, HEAD_DIM, LANES)), const((groups, HEAD_DIM, LANES)),
            const((groups, HEAD_DIM, LANES)),
        ],
        out_specs=[pl.BlockSpec((groups, tt, NQ, HEAD_DIM, LANES), lambda i: (0, post_tile(i), 0, 0, 0)),
                   const((SHIFT_DIM, LANES)), const((groups, HEAD_DIM, LANES))],
        out_shape=[jax.ShapeDtypeStruct((groups, T, NQ, HEAD_DIM, LANES), f32),
                   jax.ShapeDtypeStruct((SHIFT_DIM, LANES), f32),
                   jax.ShapeDtypeStruct((groups, HEAD_DIM, LANES), f32)],
        scratch_shapes=scratch,
        compiler_params=_cparams(1),
        name="rwkv_proj",
    )(x, g_norm, wt, mul_t, w0_t, a0_t, w2t, a2t, shift0_t, mu3_t, kk_t, ka_t)


def _rope(x, c, a, b):
    return x * c + pltpu.roll(x, LANES - ROPE_DIM // 2, 1) * a + pltpu.roll(x, ROPE_DIM // 2, 1) * b


def _nat_proj_kernel(x_ref, g_ref, w_ref, rc_ref, ra_ref, rb_ref,
                     sgr_ref, q_ref, k_ref, v_ref, sga_ref):
    u = _rmsnorm(x_ref[...], g_ref[...]).astype(bf16)
    z = _dot(u, w_ref[...])
    o_q = RWKV_WIDTH
    o_k = o_q + ATTN_WIDTH
    o_v = o_k + KV_WIDTH
    o_g = o_v + KV_WIDTH
    gr = z[:, :o_q]
    sgr_ref[...] = gr * _sigmoid(gr)
    rc, ra, rb = rc_ref[...], ra_ref[...], rb_ref[...]
    for j in range(ATTN_WIDTH // LANES):
        qj = z[:, o_q + j * LANES:o_q + (j + 1) * LANES]
        q_ref[:, j * LANES:(j + 1) * LANES] = _rope(qj, rc, ra, rb) * (HEAD_DIM ** -0.5)
    k_ref[...] = _rope(z[:, o_k:o_v], rc, ra, rb)
    v_ref[...] = z[:, o_v:o_g]
    ga = z[:, o_g:]
    sga_ref[...] = ga * _sigmoid(ga)


def _nat_proj(x2d, g_norm, w_nat, rope_c, rope_a, rope_b, *, tm):
    n_tok = x2d.shape[0]
    n_tab = rope_c.shape[0] // tm
    const = lambda shape: pl.BlockSpec(shape, lambda i: (0,) * len(shape))
    row = lambda w: pl.BlockSpec((tm, w), lambda i: (i, 0))
    tab = pl.BlockSpec((tm, LANES), lambda i: (i % n_tab, 0))
    return pl.pallas_call(
        _nat_proj_kernel,
        grid=(n_tok // tm,),
        in_specs=[row(D_MODEL), const((1, D_MODEL)), const((D_MODEL, NAT_DIM)), tab, tab, tab],
        out_specs=[row(RWKV_WIDTH), row(ATTN_WIDTH), row(KV_WIDTH), row(KV_WIDTH), row(ATTN_WIDTH)],
        out_shape=[jax.ShapeDtypeStruct((n_tok, w), f32)
                   for w in (RWKV_WIDTH, ATTN_WIDTH, KV_WIDTH, KV_WIDTH, ATTN_WIDTH)],
        compiler_params=_cparams(1),
        name="nat_proj",
    )(x2d, g_norm, w_nat, rope_c, rope_a, rope_b)


def _row_bcast(ref, idx, k):
    return jnp.broadcast_to(ref[idx + (pl.ds(k, 1), slice(None))], (HEAD_DIM, LANES))


def _wkv_step(s_ref, ref, at, at_next, sa, rk, lnw, lnb):
    vv = ref[at + (Q_V,)]
    y = jnp.zeros((HEAD_DIM, LANES), f32)
    sa_next = jnp.zeros((HEAD_DIM, LANES), f32)
    for k in range(HEAD_DIM):
        s_new = (s_ref[k] - sa * _row_bcast(ref, at + (Q_B,), k)
                 + vv * _row_bcast(ref, at + (Q_K,), k))
        s_ref[k] = s_new
        y = y + s_new * _row_bcast(ref, at + (Q_R,), k)
        sa_next = sa_next + s_new * _row_bcast(ref, at_next + (Q_KK,), k)
    mean = jnp.mean(y, axis=0, keepdims=True)
    d = y - mean
    var = jnp.mean(d * d, axis=0, keepdims=True)
    yn = d * lax.rsqrt(var + GN_EPS) * lnw + lnb
    rkk = jnp.sum(ref[at + (Q_R,)] * ref[at + (Q_K,)] * rk, axis=0, keepdims=True)
    return yn + rkk * vv, sa_next


def _wkv_scan_kernel(in_ref, s0_ref, wend_ref, rk_ref, lnw_ref, lnb_ref, o_ref, sout_ref, s_ref, *rest,
                     tc, period, natural_out):
    c = pl.program_id(1)

    @pl.when(c == 0)
    def _():
        s_ref[...] = s0_ref[0]

    obuf_ref = rest[0] if natural_out else None
    rk, lnw, lnb = rk_ref[0], lnw_ref[0], lnb_ref[0]

    for t0 in range(0, tc, period):
        sa0 = jnp.zeros((HEAD_DIM, LANES), f32)
        for k in range(HEAD_DIM):
            s_k = s_ref[k] * _row_bcast(in_ref, (0, t0, Q_W), k)
            s_ref[k] = s_k
            sa0 = sa0 + s_k * _row_bcast(in_ref, (0, t0, Q_KK), k)

        def step(t, sa, last=t0 + period - 1):
            t_next = jnp.minimum(t + 1, last)
            o, sa_next = _wkv_step(s_ref, in_ref, (0, t), (0, t_next), sa, rk, lnw, lnb)
            if natural_out:
                obuf_ref[t] = o
            else:
                o_ref[0, t] = o
            return sa_next

        lax.fori_loop(t0, t0 + period, step, sa0)

    if natural_out:
        nb = LANES // RWKV_HEADS
        for j in range(tc // GROUP_T):
            per_h = _chunk_transpose([obuf_ref[j * GROUP_T + t] for t in range(GROUP_T)], nb)
            ot = jnp.concatenate(per_h, axis=0)
            o_ref[j * GROUP_T:(j + 1) * GROUP_T] = ot.T.reshape(GROUP_T, nb, RWKV_WIDTH)

    @pl.when(c == pl.num_programs(1) - 1)
    def _():
        for k in range(HEAD_DIM):
            sout_ref[0, k] = s_ref[k] * _row_bcast(wend_ref, (0,), k)


def _wkv_scan(scan_in, s0, wend, rk_t, lnw_t, lnb_t, *, tc, period, natural_out):
    groups, T = scan_in.shape[:2]
    assert tc % period == 0
    tile = pl.BlockSpec((1, HEAD_DIM, LANES), lambda g, c: (g, 0, 0))
    state = pl.BlockSpec((1, HEAD_DIM, HEAD_DIM, LANES), lambda g, c: (g, 0, 0, 0))
    scratch = [pltpu.VMEM((HEAD_DIM, HEAD_DIM, LANES), f32)]
    if natural_out:
        nb = LANES // RWKV_HEADS
        o_spec = pl.BlockSpec((tc, nb, RWKV_WIDTH), lambda g, c: (c, 0, 0))
        o_shape = jax.ShapeDtypeStruct((T, nb, RWKV_WIDTH), f32)
        scratch.append(pltpu.VMEM((tc, HEAD_DIM, LANES), f32))
    else:
        o_spec = pl.BlockSpec((1, tc, HEAD_DIM, LANES), lambda g, c: (g, c, 0, 0))
        o_shape = jax.ShapeDtypeStruct((groups, T, HEAD_DIM, LANES), f32)
    return pl.pallas_call(
        functools.partial(_wkv_scan_kernel, tc=tc, period=period, natural_out=natural_out),
        grid=(groups, T // tc),
        in_specs=[
            pl.BlockSpec((1, tc, NQ, HEAD_DIM, LANES), lambda g, c: (g, c, 0, 0, 0)),
            state, tile, tile, tile, tile,
        ],
        out_specs=[o_spec, state],
        out_shape=[o_shape, jax.ShapeDtypeStruct((groups, HEAD_DIM, HEAD_DIM, LANES), f32)],
        scratch_shapes=scratch,
        compiler_params=_cparams(2),
        name="wkv_scan",
    )(scan_in, s0, wend, rk_t, lnw_t, lnb_t)


def _swa_attn_kernel(q_ref, kp_ref, kc_ref, vp_ref, vc_ref, sga_ref, sink_ref, o_ref, *,
                     bb, tq, first_block_has_no_prev):
    n = pl.program_id(1)
    nr = bb * tq
    half = LANES // 2
    tq_bits = tq.bit_length() - 1

    def key_mask(n_keys_per_batch, is_prev):
        rows = lax.broadcasted_iota(jnp.int32, (nr, bb * n_keys_per_batch), 0)
        cols = lax.broadcasted_iota(jnp.int32, (nr, bb * n_keys_per_batch), 1)
        i = rows & (tq - 1)
        j = cols & (n_keys_per_batch - 1)
        ok = (j > i) if is_prev else (j <= i)
        if bb > 1:
            same = (rows >> tq_bits) == (cols >> (n_keys_per_batch.bit_length() - 1))
            ok = jnp.logical_and(same, ok)
        if is_prev and first_block_has_no_prev:
            ok = jnp.logical_and(ok, n > 0)
        return jnp.tile(ok, (ATTN_GROUP, 1))

    mask_p = key_mask(WINDOW, True)
    mask_c = key_mask(tq, False)
    lane_q = lax.broadcasted_iota(jnp.int32, (nr, LANES), 1)
    lane_kp = lax.broadcasted_iota(jnp.int32, (bb * WINDOW, LANES), 1)

    kp = kp_ref[...].reshape(bb * WINDOW, LANES).astype(bf16)
    kc = kc_ref[...].reshape(nr, LANES).astype(bf16)
    vp = vp_ref[...].reshape(bb * WINDOW, LANES)
    vc = vc_ref[...].reshape(nr, LANES)

    res = []
    for g in range(ATTN_KV_HEADS):
        in_g = (lambda lane: lane < half) if g == 0 else (lambda lane: lane >= half)
        qs, sinks = [], []
        for hh in range(ATTN_GROUP):
            x = q_ref[:, :, hh * LANES:(hh + 1) * LANES].reshape(nr, LANES)
            qs.append(jnp.where(in_g(lane_q), x, 0.0).astype(bf16))
            sinks.append(jnp.broadcast_to(sink_ref[g * ATTN_GROUP + hh:g * ATTN_GROUP + hh + 1, :], (nr, LANES)))
        qg = jnp.concatenate(qs, axis=0)
        sink = jnp.concatenate(sinks, axis=0)[:, 0:1]
        sp = jnp.where(mask_p, _dot_nt(qg, kp), NEG_INF)
        sc = jnp.where(mask_c, _dot_nt(qg, kc), NEG_INF)
        m = jnp.maximum(jnp.maximum(jnp.max(sp, axis=-1, keepdims=True),
                                    jnp.max(sc, axis=-1, keepdims=True)), sink)
        pp = jnp.exp(sp - m).astype(bf16)
        pc = jnp.exp(sc - m).astype(bf16)
        e_sink = jnp.exp(sink - m)
        vpg = jnp.where(in_g(lane_kp), vp, 1.0).astype(bf16)
        vcg = jnp.where(in_g(lane_q), vc, 1.0).astype(bf16)
        pv = _dot(pp, vpg) + _dot(pc, vcg)
        res.append(pv * (1.0 / (pltpu.roll(pv, half, 1) + e_sink)))
    for hh in range(ATTN_GROUP):
        blk = slice(hh * nr, (hh + 1) * nr)
        cs = slice(hh * LANES, (hh + 1) * LANES)
        out = jnp.where(lane_q < half, res[0][blk], res[1][blk])
        o_ref[:, :, cs] = (out.reshape(bb, tq, LANES) * sga_ref[:, :, cs]).astype(o_ref.dtype)


def _attn_block_t(q_cols, k_all, v_all, ok, sink_ref):
    tq = WINDOW
    half = LANES // 2
    lane_q = lax.broadcasted_iota(jnp.int32, (tq, LANES), 1)
    lane_k = lax.broadcasted_iota(jnp.int32, (2 * WINDOW, LANES), 1)
    mask_t = jnp.tile(ok, (1, ATTN_GROUP))
    k_bf = k_all.astype(bf16)
    norm_t = []
    for g in range(ATTN_KV_HEADS):
        in_g = (lambda lane: lane < half) if g == 0 else (lambda lane: lane >= half)
        qs = [jnp.where(in_g(lane_q), x, jnp.zeros_like(x)).astype(bf16) for x in q_cols]
        sinks = [sink_ref[g * ATTN_GROUP + hh:g * ATTN_GROUP + hh + 1, :] for hh in range(ATTN_GROUP)]
        qg = jnp.concatenate(qs, axis=0)
        sink = jnp.concatenate(sinks, axis=1)
        st = jnp.where(mask_t, _dot_nt(k_bf, qg), NEG_INF)
        m = jnp.maximum(jnp.max(st, axis=0, keepdims=True), sink)
        p = jnp.exp(st - m).astype(bf16)
        e_sink = jnp.exp(sink - m)
        vg = jnp.where(in_g(lane_k), v_all, 1.0).astype(bf16)
        ot = lax.dot_general(vg, p, (((0,), (0,)), ((), ())), preferred_element_type=f32)
        lo, hi = ot[:half], ot[half:]
        num, den = (lo, hi) if g == 0 else (hi, lo)
        norm_t.append(num * (1.0 / (den + e_sink)))
    return [jnp.concatenate([norm_t[0][:, hh * tq:(hh + 1) * tq],
                             norm_t[1][:, hh * tq:(hh + 1) * tq]], axis=0).T for hh in range(ATTN_GROUP)]


def _nat_attn_kernel(x_ref, g_ref, w_ref, rc_ref, ra_ref, rb_ref, sink_ref,
                     sgr_ref, ma_ref, klast_ref, vlast_ref, kprev_ref, vprev_ref, *bufs, n_blk, tiles_per_seq):
    s = pl.program_id(0)
    tq = WINDOW
    buf_a, buf_b = bufs[:4], bufs[4:]

    @pl.when(s == 0)
    def _():
        kprev_ref[...] = jnp.zeros(kprev_ref.shape, f32)
        vprev_ref[...] = jnp.zeros(vprev_ref.shape, f32)
        for ref in buf_b:
            ref[...] = jnp.zeros(ref.shape, ref.dtype)

    def project(q_ref, k_ref, v_ref, sg_ref):
        u = _rmsnorm(x_ref[0], g_ref[...]).astype(bf16)
        z = _dot(u, w_ref[...])
        o_q = RWKV_WIDTH
        o_k = o_q + ATTN_WIDTH
        o_v = o_k + KV_WIDTH
        o_g = o_v + KV_WIDTH
        gr = z[:, :o_q]
        sgr_ref[0] = gr * _sigmoid(gr)
        rc, ra, rb = rc_ref[...], ra_ref[...], rb_ref[...]
        for j in range(ATTN_WIDTH // LANES):
            qj = z[:, o_q + j * LANES:o_q + (j + 1) * LANES]
            q_ref[j] = (_rope(qj, rc, ra, rb) * (HEAD_DIM ** -0.5)).astype(bf16)
        k = _rope(z[:, o_k:o_v], rc, ra, rb)
        v = z[:, o_v:o_g]
        k_ref[...] = k
        v_ref[...] = v
        ga = z[:, o_g:]
        sg_ref[...] = ga * _sigmoid(ga)
        last = slice((n_blk - 1) * tq, n_blk * tq)
        klast_ref[0] = k[last]
        vlast_ref[0] = v[last]

    def attend(q_ref, k_ref, v_ref, sg_ref):
        seq_start = (s - 1) % tiles_per_seq == 0
        keys = lax.broadcasted_iota(jnp.int32, (2 * WINDOW, tq), 0)
        qi = lax.broadcasted_iota(jnp.int32, (2 * WINDOW, tq), 1)
        prev_ok = jnp.logical_and(keys < WINDOW, keys > qi)
        cur_ok = jnp.logical_and(keys >= WINDOW, keys - WINDOW <= qi)
        for blk in range(n_blk):
            rows = slice(blk * tq, (blk + 1) * tq)
            if blk == 0:
                kp, vp = kprev_ref[...], vprev_ref[...]
                ok = jnp.logical_or(jnp.logical_and(prev_ok, jnp.logical_not(seq_start)), cur_ok)
            else:
                before = slice((blk - 1) * tq, blk * tq)
                kp, vp = k_ref[before, :], v_ref[before, :]
                ok = jnp.logical_or(prev_ok, cur_ok)
            outs = _attn_block_t([q_ref[j, rows, :] for j in range(ATTN_GROUP)],
                                 jnp.concatenate([kp, k_ref[rows, :]], axis=0),
                                 jnp.concatenate([vp, v_ref[rows, :]], axis=0), ok, sink_ref)
            for hh in range(ATTN_GROUP):
                cs = slice(hh * LANES, (hh + 1) * LANES)
                ma_ref[0, rows, cs] = (outs[hh] * sg_ref[rows, cs]).astype(ma_ref.dtype)
        last = slice((n_blk - 1) * tq, n_blk * tq)
        kprev_ref[...] = k_ref[last, :]
        vprev_ref[...] = v_ref[last, :]

    @pl.when(s % 2 == 0)
    def _():
        project(*buf_a)
        attend(*buf_b)

    @pl.when(s % 2 == 1)
    def _():
        project(*buf_b)
        attend(*buf_a)


def _nat_attn(x, g_norm, w_nat, rope_c, rope_a, rope_b, sink_t, *, n_blk):
    B, T, _ = x.shape
    tm = n_blk * WINDOW
    tps = T // tm
    n_tiles = B * tps
    proj_tile = lambda s: jnp.minimum(s, n_tiles - 1)
    attn_tile = lambda s: jnp.maximum(s - 1, 0)
    const = lambda shape: pl.BlockSpec(shape, lambda s: (0,) * len(shape))
    tab = pl.BlockSpec((tm, LANES), lambda s: (proj_tile(s) % tps, 0))
    last = pl.BlockSpec((1, WINDOW, KV_WIDTH), lambda s: (proj_tile(s) // tps, 0, 0))
    buf = [pltpu.VMEM((ATTN_WIDTH // LANES, tm, LANES), bf16), pltpu.VMEM((tm, KV_WIDTH), f32),
           pltpu.VMEM((tm, KV_WIDTH), f32), pltpu.VMEM((tm, ATTN_WIDTH), f32)]
    sgr, ma, k_last, v_last = pl.pallas_call(
        functools.partial(_nat_attn_kernel, n_blk=n_blk, tiles_per_seq=tps),
        grid=(n_tiles + 1,),
        in_specs=[pl.BlockSpec((1, tm, D_MODEL), lambda s: (proj_tile(s), 0, 0)),
                  const((1, D_MODEL)), const((D_MODEL, NAT_DIM)), tab, tab, tab,
                  const((ATTN_Q_HEADS, LANES))],
        out_specs=[pl.BlockSpec((1, tm, RWKV_WIDTH), lambda s: (proj_tile(s), 0, 0)),
                   pl.BlockSpec((1, tm, ATTN_WIDTH), lambda s: (attn_tile(s), 0, 0)), last, last],
        out_shape=[jax.ShapeDtypeStruct((n_tiles, tm, RWKV_WIDTH), f32),
                   jax.ShapeDtypeStruct((n_tiles, tm, ATTN_WIDTH), bf16),
                   jax.ShapeDtypeStruct((B, WINDOW, KV_WIDTH), f32),
                   jax.ShapeDtypeStruct((B, WINDOW, KV_WIDTH), f32)],
        scratch_shapes=[pltpu.VMEM((WINDOW, KV_WIDTH), f32), pltpu.VMEM((WINDOW, KV_WIDTH), f32)] + buf + buf,
        compiler_params=_cparams(1),
        name="nat_attn",
    )(x.reshape(n_tiles, tm, D_MODEL), g_norm, w_nat, rope_c, rope_a, rope_b, sink_t)
    return sgr.reshape(B, T, RWKV_WIDTH), ma.reshape(B, T, ATTN_WIDTH), k_last, v_last


def _swa_attn(q, k_prev, k_cur, v_prev, v_cur, sga, sink_t, *, bb, tq):
    B, T, _ = q.shape
    cur = lambda w: pl.BlockSpec((bb, tq, w), lambda bi, n: (bi, n, 0))
    prev = pl.BlockSpec((bb, WINDOW, KV_WIDTH), lambda bi, n: (bi, 0, 0))
    return pl.pallas_call(
        functools.partial(_swa_attn_kernel, bb=bb, tq=tq, first_block_has_no_prev=False),
        grid=(B // bb, T // tq),
        in_specs=[cur(ATTN_WIDTH), prev, cur(KV_WIDTH), prev, cur(KV_WIDTH), cur(ATTN_WIDTH),
                  pl.BlockSpec((ATTN_Q_HEADS, LANES), lambda bi, n: (0, 0))],
        out_specs=cur(ATTN_WIDTH),
        out_shape=jax.ShapeDtypeStruct((B, T, ATTN_WIDTH), f32),
        compiler_params=_cparams(2),
        name="swa_attn",
    )(q, k_prev, k_cur, v_prev, v_cur, sga, sink_t)


def _out_mix_kernel(x_ref, or_ref, sgr_ref, ma_ref, p_ref, wo_ref, gp_ref, wpg_ref, wpp_ref, gf_ref,
                    y_ref):
    mr = (or_ref[...] * sgr_ref[...]).astype(bf16)
    ma = ma_ref[...].astype(bf16)
    h = (x_ref[...] + _dot(mr, wo_ref[:RWKV_WIDTH, :]) + _dot(ma, wo_ref[RWKV_WIDTH:, :]))
    gate = _sigmoid(_dot(_rmsnorm(h, gp_ref[...]).astype(bf16), wpg_ref[...]))
    h = h + gate * _dot(p_ref[...].astype(bf16), wpp_ref[...])
    y_ref[...] = _rmsnorm(h, gf_ref[...])


def _out_mix(x2d, o_r, sgr, ma, p2d, w_out, g_ple, w_pg, w_pp, g_final, *, tm):
    n_tok = x2d.shape[0]
    const = lambda shape: pl.BlockSpec(shape, lambda i: (0,) * len(shape))
    row = lambda w: pl.BlockSpec((tm, w), lambda i: (i, 0))
    return pl.pallas_call(
        _out_mix_kernel,
        grid=(n_tok // tm,),
        in_specs=[row(D_MODEL), row(RWKV_WIDTH), row(RWKV_WIDTH), row(ATTN_WIDTH), row(PLE_DIM),
                  const((D_MODEL, D_MODEL)), const((1, D_MODEL)), const((D_MODEL, D_MODEL)),
                  const((PLE_DIM, D_MODEL)), const((1, D_MODEL))],
        out_specs=row(D_MODEL),
        out_shape=jax.ShapeDtypeStruct((n_tok, D_MODEL), f32),
        compiler_params=_cparams(1),
        name="out_mix",
    )(x2d, o_r, sgr, ma, p2d, w_out, g_ple, w_pg, w_pp, g_final)


def _rope_tables(pos):
    half = ROPE_DIM // 2
    inv = ROPE_THETA ** (-jnp.arange(half, dtype=f32) / half)
    ang = pos.astype(f32)[:, None] * inv[None, :]
    cos, sin = jnp.cos(ang), jnp.sin(ang)
    n = pos.shape[0]
    ones = jnp.ones((n, HEAD_DIM - ROPE_DIM), f32)
    zeros = jnp.zeros((n, HEAD_DIM - ROPE_DIM), f32)
    zh = jnp.zeros((n, half), f32)
    c = jnp.concatenate([cos, cos, ones], axis=1)
    a = jnp.concatenate([-sin, zh, zeros], axis=1)
    b = jnp.concatenate([zh, sin, zeros], axis=1)
    rep = LANES // HEAD_DIM
    return tuple(jnp.tile(t, (1, rep)) for t in (c, a, b))


def _col_tile(vec):
    return jnp.broadcast_to(vec[:, None], (vec.shape[0], LANES))


def _lane_param_tile(vec, heads_on_lanes):
    ph = vec.reshape(RWKV_HEADS, HEAD_DIM)
    if heads_on_lanes:
        nb = LANES // RWKV_HEADS
        return jnp.repeat(ph.T, nb, axis=1)[None]
    return jnp.broadcast_to(ph[:, :, None], (RWKV_HEADS, HEAD_DIM, LANES))


def _layer(x, p, pos, s0, shift0, kbuf, vbuf, wts, *, heads_on_lanes, tt, tc, period, tm, att_bb, att_tq):
    B, T, _ = x.shape
    n_tok = B * T
    x2d = x.reshape(n_tok, D_MODEL)

    if shift0 is None:
        shift0_t = jnp.zeros((SHIFT_DIM, LANES), f32)
    else:
        shift0_t = shift0.T
    mu = wts["mu"]
    mu3_t = jnp.stack([_lane_param_tile(mu[q * RWKV_WIDTH:(q + 1) * RWKV_WIDTH], heads_on_lanes)
                       for q in range(3)], axis=1)
    tile = lambda name: _lane_param_tile(wts[name], heads_on_lanes)
    proj_args = (wts["g_norm"], wts["wt_rwkv"], _col_tile(mu[3 * RWKV_WIDTH:]), wts["w0_t"], wts["a0_t"],
                 wts["w2t"], wts["a2t"], shift0_t, mu3_t, tile("k_k"), tile("k_a"))
    scan_params = (tile("r_k"), tile("ln_w"), tile("ln_b"))
    if heads_on_lanes:
        assert s0 is None and shift0 is None
        scan_in, shift_t, wend = _rwkv_proj(x, *proj_args, tt=tt, heads_on_lanes=True, period=period)
        shift_new = shift_t[:, LANES - B:].T
        s0_t = jnp.zeros((1, HEAD_DIM, HEAD_DIM, LANES), f32)
        o_tb, s_fin = _wkv_scan(scan_in, s0_t, wend, *scan_params, tc=tc, period=period, natural_out=True)
        o_r = jnp.swapaxes(o_tb, 0, 1).reshape(n_tok, RWKV_WIDTH)
        s_new = jnp.transpose(s_fin[0].reshape(HEAD_DIM, HEAD_DIM, RWKV_HEADS, B), (3, 2, 1, 0))
    else:
        scan_in, shift_t, wend = _rwkv_proj(jnp.swapaxes(x, 0, 1), *proj_args, tt=tt, heads_on_lanes=False,
                                            period=period)
        shift_new = shift_t.T
        s0_t = jnp.transpose(s0, (1, 3, 2, 0))
        o_scan, s_fin = _wkv_scan(scan_in, s0_t, wend, *scan_params, tc=tc, period=period, natural_out=False)
        o_r = jnp.transpose(o_scan, (3, 1, 0, 2)).reshape(n_tok, RWKV_WIDTH)
        s_new = jnp.transpose(s_fin, (3, 0, 2, 1))

    rope_c, rope_a, rope_b = _rope_tables(pos)
    if kbuf is None:
        sgr3, ma, k_last, v_last = _nat_attn(x, wts["g_norm"], wts["w_nat"], rope_c, rope_a, rope_b,
                                             wts["sink_t"], n_blk=att_tq // WINDOW)
        sgr = sgr3.reshape(n_tok, RWKV_WIDTH)
        k_new = k_last.reshape(B, WINDOW, ATTN_KV_HEADS, HEAD_DIM)
        v_new = v_last.reshape(B, WINDOW, ATTN_KV_HEADS, HEAD_DIM)
    else:
        reps = tm // T
        rope_c, rope_a, rope_b = (jnp.tile(t, (reps, 1)) for t in (rope_c, rope_a, rope_b))
        sgr, q, k, v, sga = _nat_proj(x2d, wts["g_norm"], wts["w_nat"], rope_c, rope_a, rope_b, tm=tm)
        k3 = k.reshape(B, T, KV_WIDTH)
        v3 = v.reshape(B, T, KV_WIDTH)
        kb = kbuf.reshape(B, WINDOW, KV_WIDTH)
        vb = vbuf.reshape(B, WINDOW, KV_WIDTH)
        ma = _swa_attn(q.reshape(B, T, ATTN_WIDTH), kb, k3, vb, v3, sga.reshape(B, T, ATTN_WIDTH),
                       wts["sink_t"], bb=att_bb, tq=att_tq)
        k_new = jnp.concatenate([kb, k3], axis=1)[:, -WINDOW:].reshape(B, WINDOW, ATTN_KV_HEADS, HEAD_DIM)
        v_new = jnp.concatenate([vb, v3], axis=1)[:, -WINDOW:].reshape(B, WINDOW, ATTN_KV_HEADS, HEAD_DIM)

    y = _out_mix(x2d, o_r, sgr, ma.reshape(n_tok, ATTN_WIDTH), p.reshape(n_tok, PLE_DIM),
                 wts["w_out"], wts["g_ple"], wts["w_pg"], wts["w_pp"], wts["g_final"], tm=tm)
    return y.reshape(B, T, D_MODEL), s_new, shift_new, k_new, v_new


def kernel(x_prompt, x_sample, state_rwkv_wkv, state_rwkv_shift, cache_swa_k, cache_swa_v,
           p_prompt, p_sample, g_norm, w_in, mu_shift, w0, w2, a0, a2, k_k, k_a, r_k,
           ln_w, ln_b, sinks, w_out, g_ple, w_ple_gate, w_ple_proj, g_final):
    assert w_in.shape[0] == 1, "single layer"
    w_in0 = w_in[0]
    head_order = [g * ATTN_GROUP + i for i in range(ATTN_GROUP) for g in range(ATTN_KV_HEADS)]
    cols = jnp.concatenate([jnp.arange(h * HEAD_DIM, (h + 1) * HEAD_DIM) for h in head_order])
    o_q = SHIFT_DIM + RWKV_WIDTH
    o_ga = o_q + ATTN_WIDTH + 2 * KV_WIDTH
    w_nat = jnp.concatenate([w_in0[:, SHIFT_DIM:o_q], w_in0[:, o_q + cols], w_in0[:, o_q + ATTN_WIDTH:o_ga],
                             w_in0[:, o_ga + cols]], axis=1)
    w_out0 = jnp.concatenate([w_out[0][:RWKV_WIDTH], w_out[0][RWKV_WIDTH + cols]], axis=0)
    wts = {
        "g_norm": g_norm[0][None, :],
        "wt_rwkv": w_in0[:, :SHIFT_DIM].T.astype(bf16),
        "w_nat": w_nat.astype(bf16),
        "mu": mu_shift[0],
        "w0_t": _col_tile(w0[0]), "a0_t": _col_tile(a0[0]),
        "k_k": k_k[0], "k_a": k_a[0],
        "w2t": w2[0].T.astype(bf16), "a2t": a2[0].T.astype(bf16),
        "r_k": r_k[0], "ln_w": ln_w[0], "ln_b": ln_b[0],
        "sink_t": jnp.broadcast_to(sinks[0][:, None], (ATTN_Q_HEADS, LANES)),
        "w_out": w_out0.astype(bf16), "g_ple": g_ple[0][None, :],
        "w_pg": w_ple_gate[0].astype(bf16), "w_pp": w_ple_proj[0].astype(bf16),
        "g_final": g_final[None, :],
    }
    Bp, Tp, _ = x_prompt.shape
    Bs, Ts, _ = x_sample.shape
    assert Bp * RWKV_HEADS == LANES and Bs == LANES and Tp % WINDOW == 0 and Ts % SUBLANES == 0

    yp, s1, sh1, k1, v1 = _layer(x_prompt, p_prompt[0], jnp.arange(Tp), None, None, None, None, wts,
                                 heads_on_lanes=True, tt=16, tc=32, period=32, tm=1024, att_bb=1, att_tq=4 * WINDOW)
    ys, s2, sh2, k2, v2 = _layer(x_sample, p_sample[0], PAST_LEN + jnp.arange(Ts),
                                 state_rwkv_wkv[0], state_rwkv_shift[0], cache_swa_k[0], cache_swa_v[0],
                                 wts, heads_on_lanes=False, tt=2, tc=Ts, period=Ts, tm=Bs * Ts, att_bb=8, att_tq=Ts)
    return (yp, ys, s1[None], sh1[None], k1[None], v1[None], s2[None], sh2[None], k2[None], v2[None])
```

```python
import functools
import math

import jax
import jax.numpy as jnp
from jax import lax
from jax.experimental import pallas as pl
from jax.experimental.pallas import tpu as pltpu

D_MODEL = 1024
HEAD_DIM = 64
RWKV_WIDTH = 512
RWKV_HEADS = 8
ATTN_WIDTH = 512
ATTN_Q_HEADS = 8
ATTN_KV_HEADS = 2
ATTN_GROUP = 4
KV_WIDTH = 128
LORA = 64
WINDOW = 128
ROPE_THETA = 500000.0
ROPE_DIM = 16
PLE_DIM = 256
NORM_EPS = 1e-6
GN_EPS = 64e-5
NEG_INF = -1e30
PAST_LEN = 16384
SHIFT_DIM = 3 * RWKV_WIDTH + 2 * LORA
NAT_DIM = RWKV_WIDTH + ATTN_WIDTH + 2 * KV_WIDTH + ATTN_WIDTH

LANES = 128
SUBLANES = 8
VMEM_LIMIT = 56 * 1024 * 1024
DECAY_SCALE = math.exp(-0.5)
GROUP_T = 8

Q_KK, Q_W, Q_B, Q_K, Q_R, Q_V = range(6)
NQ = 6
P_MU_R, P_MU_K, P_MU_V, P_KK, P_KA, P_RK, P_LNW, P_LNB = range(8)
NP = 8

f32 = jnp.float32
bf16 = jnp.bfloat16


def _cparams(n_axes):
    return pltpu.CompilerParams(dimension_semantics=("arbitrary",) * n_axes,
                                vmem_limit_bytes=VMEM_LIMIT)


def _rmsnorm(x, g):
    ms = jnp.mean(x * x, axis=-1, keepdims=True)
    return x * lax.rsqrt(ms + NORM_EPS) * g


def _sigmoid(x):
    return 1.0 / (1.0 + jnp.exp(-x))


def _dot_nt(a, b):
    return lax.dot_general(a, b, (((1,), (1,)), ((), ())), preferred_element_type=f32)


def _dot(a, b):
    return jnp.dot(a, b, preferred_element_type=f32)


def _chunk_transpose(xs, chunk):
    lane = lax.broadcasted_iota(jnp.int32, xs[0].shape, 1)
    xs = list(xs)
    for d in (4, 2, 1):
        hi_lanes = (lane & (chunk * d)) != 0
        nxt = list(xs)
        for i in range(8):
            if i & d:
                continue
            lo, hi = xs[i], xs[i + d]
            if 2 * chunk * d == LANES:
                moved = pltpu.roll(jnp.where(hi_lanes, lo, hi), chunk * d, 1)
                nxt[i] = jnp.where(hi_lanes, moved, lo)
                nxt[i + d] = jnp.where(hi_lanes, hi, moved)
            else:
                nxt[i] = jnp.where(hi_lanes, pltpu.roll(hi, chunk * d, 1), lo)
                nxt[i + d] = jnp.where(hi_lanes, hi, pltpu.roll(lo, LANES - chunk * d, 1))
        xs = nxt
    return xs


def _rwkv_proj_kernel(x_ref, g_ref, wt_ref, mul_ref, c2_ref, w2t_ref, a2t_ref, shift0_ref,
                      pt_ref, out_ref, shift_ref, wend_ref,
                      carry_ref, prev_ref, wc_ref, za_ref, zb_ref, *rest, tt, nb, heads_on_lanes, period):
    i = pl.program_id(0)
    groups = 1 if heads_on_lanes else RWKV_HEADS
    lora0 = 3 * RWKV_WIDTH
    n = tt * nb
    first = i == 0

    def carry0():
        return shift0_ref[lora0:, :]

    def prev0(g, q):
        if heads_on_lanes:
            return jnp.zeros((HEAD_DIM, LANES), f32)
        r0 = q * RWKV_WIDTH + g * HEAD_DIM
        return shift0_ref[r0:r0 + HEAD_DIM, :]

    @pl.when(first)
    def _():
        zb_ref[...] = jnp.zeros(zb_ref.shape, f32)
        carry_ref[...] = carry0()
        wc_ref[...] = jnp.ones(wc_ref.shape, f32)
        for g in range(groups):
            for q in range(3):
                prev_ref[g, q] = prev0(g, q)

    def project(z_ref):
        if heads_on_lanes:
            u_ref = rest[0]
            n_col = D_MODEL // LANES
            for bi in range(nb):
                ub = _rmsnorm(x_ref[bi], g_ref[...])
                for ci in range(n_col):
                    u_ref[ci, pl.ds(bi, tt, stride=nb), :] = ub[:, ci * LANES:(ci + 1) * LANES]
            u = jnp.concatenate([u_ref[ci] for ci in range(n_col)], axis=1).astype(bf16)
        else:
            u = _rmsnorm(x_ref[...].reshape(n, D_MODEL), g_ref[...]).astype(bf16)
        z_ref[...] = _dot_nt(wt_ref[...], u)

    def post(z_ref):
        tile = i - 1
        lane = lax.broadcasted_iota(jnp.int32, (2 * LORA, LANES), 1)
        ones = jnp.ones((HEAD_DIM, LANES), f32)
        prev = [[prev_ref[g, q] for q in range(3)] for g in range(groups)]
        wc = [wc_ref[g] for g in range(groups)]
        prev_rot = carry_ref[...]
        steps_per_blk = LANES // nb if heads_on_lanes else 1
        for j in range(n // LANES):
            z = z_ref[:, j * LANES:(j + 1) * LANES]
            zl = z[lora0:]
            if nb % LANES == 0:
                prev_l, zl_rot = prev_rot, zl
            else:
                zl_rot = pltpu.roll(zl, nb, 1)
                prev_l = jnp.where(lane < nb, prev_rot, zl_rot)
            prev_rot = zl_rot
            zls = zl + mul_ref[...] * (prev_l - zl)
            w_pre = c2_ref[0] + _dot(w2t_ref[...], jnp.tanh(zls[:LORA]).astype(bf16))
            decay = jnp.exp(-DECAY_SCALE * _sigmoid(w_pre))
            a_all = _sigmoid(c2_ref[1] + _dot(a2t_ref[...], zls[LORA:].astype(bf16)))
            raw = [z[0:RWKV_WIDTH], z[RWKV_WIDTH:2 * RWKV_WIDTH], z[2 * RWKV_WIDTH:lora0], decay, a_all]
            per_head = [[val[h * HEAD_DIM:(h + 1) * HEAD_DIM] for h in range(RWKV_HEADS)] for val in raw]
            if heads_on_lanes:
                tiles = [_chunk_transpose(ph, nb) for ph in per_head]
            for ls in range(steps_per_blk):
                local = j * steps_per_blk + ls
                for g in range(groups):
                    idx = ls if heads_on_lanes else g
                    r_raw, kx_raw, v_raw, w_t, a_t = (tiles[q][idx] if heads_on_lanes else per_head[q][idx]
                                                      for q in range(5))
                    r = r_raw + pt_ref[g, P_MU_R] * (prev[g][0] - r_raw)
                    kx = kx_raw + pt_ref[g, P_MU_K] * (prev[g][1] - kx_raw)
                    v = v_raw + pt_ref[g, P_MU_V] * (prev[g][2] - v_raw)
                    prev[g] = [r_raw, kx_raw, v_raw]
                    kkr = kx * pt_ref[g, P_KK]
                    ss = jnp.sum(kkr * kkr, axis=0, keepdims=True)
                    kk = kkr * (1.0 / jnp.maximum(jnp.sqrt(ss), 1e-12))
                    k = kx * (1.0 + (a_t - 1.0) * pt_ref[g, P_KA])
                    b = kk * a_t
                    wc_in = wc[g]
                    if tt >= period:
                        wc_base = ones if local % period == 0 else wc_in
                    elif local == 0:
                        wc_base = jnp.where(tile % (period // tt) == 0, ones, wc_in)
                    else:
                        wc_base = wc_in
                    wc_t = wc_base * w_t
                    inv_wc = 1.0 / wc_t
                    step = (g, local)
                    out_ref[step + (Q_KK,)] = kk * wc_base
                    out_ref[step + (Q_W,)] = wc_in
                    out_ref[step + (Q_B,)] = b * inv_wc
                    out_ref[step + (Q_K,)] = k * inv_wc
                    out_ref[step + (Q_R,)] = r * wc_t
                    out_ref[step + (Q_V,)] = v
                    wc[g] = wc_t
        carry_ref[...] = jnp.where(first, carry0(), prev_rot)
        shift_ref[...] = z
        for g in range(groups):
            wc_g = jnp.where(first, ones, wc[g])
            wc_ref[g] = wc_g
            wend_ref[g] = wc_g
            for q in range(3):
                prev_ref[g, q] = jnp.where(first, prev0(g, q), prev[g][q])

    @pl.when(i % 2 == 0)
    def _():
        project(za_ref)
        post(zb_ref)

    @pl.when(i % 2 == 1)
    def _():
        project(zb_ref)
        post(za_ref)


def _rwkv_proj(x, g_norm, wt, mul_t, c2_t, w2t, a2t, shift0_t, p_tiles, *, tt, heads_on_lanes, period):
    const = lambda shape: pl.BlockSpec(shape, lambda i: (0,) * len(shape))
    groups = 1 if heads_on_lanes else RWKV_HEADS
    if heads_on_lanes:
        nb, T, _ = x.shape
    else:
        T, nb, _ = x.shape
    n_tiles = T // tt
    proj_tile = lambda i: jnp.minimum(i, n_tiles - 1)
    post_tile = lambda i: jnp.maximum(i - 1, 0)
    scratch = [pltpu.VMEM((2 * LORA, LANES), f32), pltpu.VMEM((groups, 3, HEAD_DIM, LANES), f32),
               pltpu.VMEM((groups, HEAD_DIM, LANES), f32),
               pltpu.VMEM((SHIFT_DIM, tt * nb), f32), pltpu.VMEM((SHIFT_DIM, tt * nb), f32)]
    if heads_on_lanes:
        x_spec = pl.BlockSpec((nb, tt, D_MODEL), lambda i: (0, proj_tile(i), 0))
        scratch.append(pltpu.VMEM((D_MODEL // LANES, tt * nb, LANES), f32))
    else:
        x_spec = pl.BlockSpec((tt, nb, D_MODEL), lambda i: (proj_tile(i), 0, 0))
    assert period % tt == 0 or tt % period == 0
    kern = functools.partial(_rwkv_proj_kernel, tt=tt, nb=nb, heads_on_lanes=heads_on_lanes, period=period)
    return pl.pallas_call(
        kern,
        grid=(n_tiles + 1,),
        in_specs=[
            x_spec,
            const((1, D_MODEL)),
            const((SHIFT_DIM, D_MODEL)),
            const((2 * LORA, LANES)),
            const((2, RWKV_WIDTH, LANES)),
            const((RWKV_WIDTH, LORA)), const((RWKV_WIDTH, LORA)),
            const((SHIFT_DIM, LANES)),
            const((groups, NP, HEAD_DIM, LANES)),
        ],
        out_specs=[pl.BlockSpec((groups, tt, NQ, HEAD_DIM, LANES), lambda i: (0, post_tile(i), 0, 0, 0)),
                   const((SHIFT_DIM, LANES)), const((groups, HEAD_DIM, LANES))],
        out_shape=[jax.ShapeDtypeStruct((groups, T, NQ, HEAD_DIM, LANES), f32),
                   jax.ShapeDtypeStruct((SHIFT_DIM, LANES), f32),
                   jax.ShapeDtypeStruct((groups, HEAD_DIM, LANES), f32)],
        scratch_shapes=scratch,
        compiler_params=_cparams(1),
        name="rwkv_proj",
    )(x, g_norm, wt, mul_t, c2_t, w2t, a2t, shift0_t, p_tiles)


def _rope(x, c, a, b):
    return x * c + pltpu.roll(x, LANES - ROPE_DIM // 2, 1) * a + pltpu.roll(x, ROPE_DIM // 2, 1) * b


def _nat_proj_kernel(x_ref, g_ref, w_ref, rope_ref, sgr_ref, q_ref, k_ref, v_ref, sga_ref):
    u = _rmsnorm(x_ref[...], g_ref[...]).astype(bf16)
    z = _dot(u, w_ref[...])
    o_q = RWKV_WIDTH
    o_k = o_q + ATTN_WIDTH
    o_v = o_k + KV_WIDTH
    o_g = o_v + KV_WIDTH
    gr = z[:, :o_q]
    sgr_ref[...] = gr * _sigmoid(gr)
    rc, ra, rb = rope_ref[0], rope_ref[1], rope_ref[2]
    for j in range(ATTN_WIDTH // LANES):
        qj = z[:, o_q + j * LANES:o_q + (j + 1) * LANES]
        q_ref[:, j * LANES:(j + 1) * LANES] = _rope(qj, rc, ra, rb) * (HEAD_DIM ** -0.5)
    k_ref[...] = _rope(z[:, o_k:o_v], rc, ra, rb)
    v_ref[...] = z[:, o_v:o_g]
    ga = z[:, o_g:]
    sga_ref[...] = ga * _sigmoid(ga)


def _nat_proj(x2d, g_norm, w_nat, rope, *, tm):
    n_tok = x2d.shape[0]
    n_tab = rope.shape[1] // tm
    const = lambda shape: pl.BlockSpec(shape, lambda i: (0,) * len(shape))
    row = lambda w: pl.BlockSpec((tm, w), lambda i: (i, 0))
    tab = pl.BlockSpec((3, tm, LANES), lambda i: (0, i % n_tab, 0))
    return pl.pallas_call(
        _nat_proj_kernel,
        grid=(n_tok // tm,),
        in_specs=[row(D_MODEL), const((1, D_MODEL)), const((D_MODEL, NAT_DIM)), tab],
        out_specs=[row(RWKV_WIDTH), row(ATTN_WIDTH), row(KV_WIDTH), row(KV_WIDTH), row(ATTN_WIDTH)],
        out_shape=[jax.ShapeDtypeStruct((n_tok, w), f32)
                   for w in (RWKV_WIDTH, ATTN_WIDTH, KV_WIDTH, KV_WIDTH, ATTN_WIDTH)],
        compiler_params=_cparams(1),
        name="nat_proj",
    )(x2d, g_norm, w_nat, rope)


def _row_bcast(ref, idx, k):
    return jnp.broadcast_to(ref[idx + (pl.ds(k, 1), slice(None))], (HEAD_DIM, LANES))


def _wkv_step(s_ref, ref, at, at_next, sa, rk, lnw, lnb):
    vv = ref[at + (Q_V,)]
    y = jnp.zeros((HEAD_DIM, LANES), f32)
    sa_next = jnp.zeros((HEAD_DIM, LANES), f32)
    for k in range(HEAD_DIM):
        s_new = (s_ref[k] - sa * _row_bcast(ref, at + (Q_B,), k)
                 + vv * _row_bcast(ref, at + (Q_K,), k))
        s_ref[k] = s_new
        y = y + s_new * _row_bcast(ref, at + (Q_R,), k)
        sa_next = sa_next + s_new * _row_bcast(ref, at_next + (Q_KK,), k)
    mean = jnp.mean(y, axis=0, keepdims=True)
    d = y - mean
    var = jnp.mean(d * d, axis=0, keepdims=True)
    yn = d * lax.rsqrt(var + GN_EPS) * lnw + lnb
    rkk = jnp.sum(ref[at + (Q_R,)] * ref[at + (Q_K,)] * rk, axis=0, keepdims=True)
    return yn + rkk * vv, sa_next


def _wkv_scan_kernel(in_ref, s0_ref, wend_ref, pt_ref, o_ref, sout_ref, s_ref, *rest,
                     tc, period, natural_out):
    c = pl.program_id(1)

    @pl.when(c == 0)
    def _():
        s_ref[...] = s0_ref[0]

    obuf_ref = rest[0] if natural_out else None
    rk, lnw, lnb = pt_ref[0, P_RK], pt_ref[0, P_LNW], pt_ref[0, P_LNB]

    for t0 in range(0, tc, period):
        sa0 = jnp.zeros((HEAD_DIM, LANES), f32)
        for k in range(HEAD_DIM):
            s_k = s_ref[k] * _row_bcast(in_ref, (0, t0, Q_W), k)
            s_ref[k] = s_k
            sa0 = sa0 + s_k * _row_bcast(in_ref, (0, t0, Q_KK), k)

        def step(t, sa, last=t0 + period - 1):
            t_next = jnp.minimum(t + 1, last)
            o, sa_next = _wkv_step(s_ref, in_ref, (0, t), (0, t_next), sa, rk, lnw, lnb)
            if natural_out:
                obuf_ref[t] = o
            else:
                o_ref[0, t] = o
            return sa_next

        lax.fori_loop(t0, t0 + period, step, sa0)

    if natural_out:
        nb = LANES // RWKV_HEADS
        for j in range(tc // GROUP_T):
            per_h = _chunk_transpose([obuf_ref[j * GROUP_T + t] for t in range(GROUP_T)], nb)
            ot = jnp.concatenate(per_h, axis=0)
            o_ref[j * GROUP_T:(j + 1) * GROUP_T] = ot.T.reshape(GROUP_T, nb, RWKV_WIDTH)

    @pl.when(c == pl.num_programs(1) - 1)
    def _():
        for k in range(HEAD_DIM):
            sout_ref[0, k] = s_ref[k] * _row_bcast(wend_ref, (0,), k)


def _wkv_scan(scan_in, s0, wend, p_tiles, *, tc, period, natural_out):
    groups, T = scan_in.shape[:2]
    assert tc % period == 0
    tile = pl.BlockSpec((1, HEAD_DIM, LANES), lambda g, c: (g, 0, 0))
    state = pl.BlockSpec((1, HEAD_DIM, HEAD_DIM, LANES), lambda g, c: (g, 0, 0, 0))
    scratch = [pltpu.VMEM((HEAD_DIM, HEAD_DIM, LANES), f32)]
    if natural_out:
        nb = LANES // RWKV_HEADS
        o_spec = pl.BlockSpec((tc, nb, RWKV_WIDTH), lambda g, c: (c, 0, 0))
        o_shape = jax.ShapeDtypeStruct((T, nb, RWKV_WIDTH), f32)
        scratch.append(pltpu.VMEM((tc, HEAD_DIM, LANES), f32))
    else:
        o_spec = pl.BlockSpec((1, tc, HEAD_DIM, LANES), lambda g, c: (g, c, 0, 0))
        o_shape = jax.ShapeDtypeStruct((groups, T, HEAD_DIM, LANES), f32)
    return pl.pallas_call(
        functools.partial(_wkv_scan_kernel, tc=tc, period=period, natural_out=natural_out),
        grid=(groups, T // tc),
        in_specs=[
            pl.BlockSpec((1, tc, NQ, HEAD_DIM, LANES), lambda g, c: (g, c, 0, 0, 0)),
            state, tile, pl.BlockSpec((1, NP, HEAD_DIM, LANES), lambda g, c: (g, 0, 0, 0)),
        ],
        out_specs=[o_spec, state],
        out_shape=[o_shape, jax.ShapeDtypeStruct((groups, HEAD_DIM, HEAD_DIM, LANES), f32)],
        scratch_shapes=scratch,
        compiler_params=_cparams(2),
        name="wkv_scan",
    )(scan_in, s0, wend, p_tiles)


def _swa_attn_kernel(q_ref, kp_ref, kc_ref, vp_ref, vc_ref, sga_ref, sink_ref, o_ref, *,
                     bb, tq, first_block_has_no_prev):
    n = pl.program_id(1)
    nr = bb * tq
    half = LANES // 2
    tq_bits = tq.bit_length() - 1

    def key_mask(n_keys_per_batch, is_prev):
        rows = lax.broadcasted_iota(jnp.int32, (nr, bb * n_keys_per_batch), 0)
        cols = lax.broadcasted_iota(jnp.int32, (nr, bb * n_keys_per_batch), 1)
        i = rows & (tq - 1)
        j = cols & (n_keys_per_batch - 1)
        ok = (j > i) if is_prev else (j <= i)
        if bb > 1:
            same = (rows >> tq_bits) == (cols >> (n_keys_per_batch.bit_length() - 1))
            ok = jnp.logical_and(same, ok)
        if is_prev and first_block_has_no_prev:
            ok = jnp.logical_and(ok, n > 0)
        return jnp.tile(ok, (ATTN_GROUP, 1))

    mask_p = key_mask(WINDOW, True)
    mask_c = key_mask(tq, False)
    lane_q = lax.broadcasted_iota(jnp.int32, (nr, LANES), 1)
    lane_kp = lax.broadcasted_iota(jnp.int32, (bb * WINDOW, LANES), 1)

    kp = kp_ref[...].reshape(bb * WINDOW, LANES).astype(bf16)
    kc = kc_ref[...].reshape(nr, LANES).astype(bf16)
    vp = vp_ref[...].reshape(bb * WINDOW, LANES)
    vc = vc_ref[...].reshape(nr, LANES)

    res = []
    for g in range(ATTN_KV_HEADS):
        in_g = (lambda lane: lane < half) if g == 0 else (lambda lane: lane >= half)
        qs, sinks = [], []
        for hh in range(ATTN_GROUP):
            x = q_ref[:, :, hh * LANES:(hh + 1) * LANES].reshape(nr, LANES)
            qs.append(jnp.where(in_g(lane_q), x, 0.0).astype(bf16))
            sinks.append(jnp.broadcast_to(sink_ref[g * ATTN_GROUP + hh:g * ATTN_GROUP + hh + 1, :], (nr, LANES)))
        qg = jnp.concatenate(qs, axis=0)
        sink = jnp.concatenate(sinks, axis=0)[:, 0:1]
        sp = jnp.where(mask_p, _dot_nt(qg, kp), NEG_INF)
        sc = jnp.where(mask_c, _dot_nt(qg, kc), NEG_INF)
        m = jnp.maximum(jnp.maximum(jnp.max(sp, axis=-1, keepdims=True),
                                    jnp.max(sc, axis=-1, keepdims=True)), sink)
        pp = jnp.exp(sp - m).astype(bf16)
        pc = jnp.exp(sc - m).astype(bf16)
        e_sink = jnp.exp(sink - m)
        vpg = jnp.where(in_g(lane_kp), vp, 1.0).astype(bf16)
        vcg = jnp.where(in_g(lane_q), vc, 1.0).astype(bf16)
        pv = _dot(pp, vpg) + _dot(pc, vcg)
        res.append(pv * (1.0 / (pltpu.roll(pv, half, 1) + e_sink)))
    for hh in range(ATTN_GROUP):
        blk = slice(hh * nr, (hh + 1) * nr)
        cs = slice(hh * LANES, (hh + 1) * LANES)
        out = jnp.where(lane_q < half, res[0][blk], res[1][blk])
        o_ref[:, :, cs] = (out.reshape(bb, tq, LANES) * sga_ref[:, :, cs]).astype(o_ref.dtype)


def _attn_block_t(q_cols, k_all, v_all, ok, sink_ref):
    tq = WINDOW
    half = LANES // 2
    lane_q = lax.broadcasted_iota(jnp.int32, (tq, LANES), 1)
    lane_k = lax.broadcasted_iota(jnp.int32, (2 * WINDOW, LANES), 1)
    mask_t = jnp.tile(ok, (1, ATTN_GROUP))
    k_bf = k_all.astype(bf16)
    norm_t = []
    for g in range(ATTN_KV_HEADS):
        in_g = (lambda lane: lane < half) if g == 0 else (lambda lane: lane >= half)
        qs = [jnp.where(in_g(lane_q), x, jnp.zeros_like(x)).astype(bf16) for x in q_cols]
        sinks = [sink_ref[g * ATTN_GROUP + hh:g * ATTN_GROUP + hh + 1, :] for hh in range(ATTN_GROUP)]
        qg = jnp.concatenate(qs, axis=0)
        sink = jnp.concatenate(sinks, axis=1)
        st = jnp.where(mask_t, _dot_nt(k_bf, qg), NEG_INF)
        m = jnp.maximum(jnp.max(st, axis=0, keepdims=True), sink)
        p = jnp.exp(st - m).astype(bf16)
        e_sink = jnp.exp(sink - m)
        vg = jnp.where(in_g(lane_k), v_all, 1.0).astype(bf16)
        ot = lax.dot_general(vg, p, (((0,), (0,)), ((), ())), preferred_element_type=f32)
        lo, hi = ot[:half], ot[half:]
        num, den = (lo, hi) if g == 0 else (hi, lo)
        norm_t.append(num * (1.0 / (den + e_sink)))
    return [jnp.concatenate([norm_t[0][:, hh * tq:(hh + 1) * tq],
                             norm_t[1][:, hh * tq:(hh + 1) * tq]], axis=0).T for hh in range(ATTN_GROUP)]


def _nat_attn_kernel(x_ref, g_ref, w_ref, rope_ref, sink_ref,
                     sgr_ref, ma_ref, klast_ref, vlast_ref, kprev_ref, vprev_ref, *bufs, n_blk, tiles_per_seq):
    s = pl.program_id(0)
    tq = WINDOW
    buf_a, buf_b = bufs[:4], bufs[4:]

    @pl.when(s == 0)
    def _():
        kprev_ref[...] = jnp.zeros(kprev_ref.shape, f32)
        vprev_ref[...] = jnp.zeros(vprev_ref.shape, f32)
        for ref in buf_b:
            ref[...] = jnp.zeros(ref.shape, ref.dtype)

    def project(q_ref, k_ref, v_ref, sg_ref):
        u = _rmsnorm(x_ref[0], g_ref[...]).astype(bf16)
        z = _dot(u, w_ref[...])
        o_q = RWKV_WIDTH
        o_k = o_q + ATTN_WIDTH
        o_v = o_k + KV_WIDTH
        o_g = o_v + KV_WIDTH
        gr = z[:, :o_q]
        sgr_ref[0] = gr * _sigmoid(gr)
        rc, ra, rb = rope_ref[0], rope_ref[1], rope_ref[2]
        for j in range(ATTN_WIDTH // LANES):
            qj = z[:, o_q + j * LANES:o_q + (j + 1) * LANES]
            q_ref[j] = (_rope(qj, rc, ra, rb) * (HEAD_DIM ** -0.5)).astype(bf16)
        k = _rope(z[:, o_k:o_v], rc, ra, rb)
        v = z[:, o_v:o_g]
        k_ref[...] = k
        v_ref[...] = v
        ga = z[:, o_g:]
        sg_ref[...] = ga * _sigmoid(ga)
        last = slice((n_blk - 1) * tq, n_blk * tq)
        klast_ref[0] = k[last]
        vlast_ref[0] = v[last]

    def attend(q_ref, k_ref, v_ref, sg_ref):
        seq_start = (s - 1) % tiles_per_seq == 0
        keys = lax.broadcasted_iota(jnp.int32, (2 * WINDOW, tq), 0)
        qi = lax.broadcasted_iota(jnp.int32, (2 * WINDOW, tq), 1)
        prev_ok = jnp.logical_and(keys < WINDOW, keys > qi)
        cur_ok = jnp.logical_and(keys >= WINDOW, keys - WINDOW <= qi)
        for blk in range(n_blk):
            rows = slice(blk * tq, (blk + 1) * tq)
            if blk == 0:
                kp, vp = kprev_ref[...], vprev_ref[...]
                ok = jnp.logical_or(jnp.logical_and(prev_ok, jnp.logical_not(seq_start)), cur_ok)
            else:
                before = slice((blk - 1) * tq, blk * tq)
                kp, vp = k_ref[before, :], v_ref[before, :]
                ok = jnp.logical_or(prev_ok, cur_ok)
            outs = _attn_block_t([q_ref[j, rows, :] for j in range(ATTN_GROUP)],
                                 jnp.concatenate([kp, k_ref[rows, :]], axis=0),
                                 jnp.concatenate([vp, v_ref[rows, :]], axis=0), ok, sink_ref)
            for hh in range(ATTN_GROUP):
                cs = slice(hh * LANES, (hh + 1) * LANES)
                ma_ref[0, rows, cs] = (outs[hh] * sg_ref[rows, cs]).astype(ma_ref.dtype)
        last = slice((n_blk - 1) * tq, n_blk * tq)
        kprev_ref[...] = k_ref[last, :]
        vprev_ref[...] = v_ref[last, :]

    @pl.when(s % 2 == 0)
    def _():
        project(*buf_a)
        attend(*buf_b)

    @pl.when(s % 2 == 1)
    def _():
        project(*buf_b)
        attend(*buf_a)


def _nat_attn(x, g_norm, w_nat, rope, sink_t, *, n_blk):
    B, T, _ = x.shape
    tm = n_blk * WINDOW
    tps = T // tm
    n_tiles = B * tps
    proj_tile = lambda s: jnp.minimum(s, n_tiles - 1)
    attn_tile = lambda s: jnp.maximum(s - 1, 0)
    const = lambda shape: pl.BlockSpec(shape, lambda s: (0,) * len(shape))
    tab = pl.BlockSpec((3, tm, LANES), lambda s: (0, proj_tile(s) % tps, 0))
    last = pl.BlockSpec((1, WINDOW, KV_WIDTH), lambda s: (proj_tile(s) // tps, 0, 0))
    buf = [pltpu.VMEM((ATTN_WIDTH // LANES, tm, LANES), bf16), pltpu.VMEM((tm, KV_WIDTH), f32),
           pltpu.VMEM((tm, KV_WIDTH), f32), pltpu.VMEM((tm, ATTN_WIDTH), f32)]
    sgr, ma, k_last, v_last = pl.pallas_call(
        functools.partial(_nat_attn_kernel, n_blk=n_blk, tiles_per_seq=tps),
        grid=(n_tiles + 1,),
        in_specs=[pl.BlockSpec((1, tm, D_MODEL), lambda s: (proj_tile(s), 0, 0)),
                  const((1, D_MODEL)), const((D_MODEL, NAT_DIM)), tab,
                  const((ATTN_Q_HEADS, LANES))],
        out_specs=[pl.BlockSpec((1, tm, RWKV_WIDTH), lambda s: (proj_tile(s), 0, 0)),
                   pl.BlockSpec((1, tm, ATTN_WIDTH), lambda s: (attn_tile(s), 0, 0)), last, last],
        out_shape=[jax.ShapeDtypeStruct((n_tiles, tm, RWKV_WIDTH), f32),
                   jax.ShapeDtypeStruct((n_tiles, tm, ATTN_WIDTH), bf16),
                   jax.ShapeDtypeStruct((B, WINDOW, KV_WIDTH), f32),
                   jax.ShapeDtypeStruct((B, WINDOW, KV_WIDTH), f32)],
        scratch_shapes=[pltpu.VMEM((WINDOW, KV_WIDTH), f32), pltpu.VMEM((WINDOW, KV_WIDTH), f32)] + buf + buf,
        compiler_params=_cparams(1),
        name="nat_attn",
    )(x.reshape(n_tiles, tm, D_MODEL), g_norm, w_nat, rope, sink_t)
    return sgr.reshape(B, T, RWKV_WIDTH), ma.reshape(B, T, ATTN_WIDTH), k_last, v_last


def _swa_attn(q, k_prev, k_cur, v_prev, v_cur, sga, sink_t, *, bb, tq):
    B, T, _ = q.shape
    cur = lambda w: pl.BlockSpec((bb, tq, w), lambda bi, n: (bi, n, 0))
    prev = pl.BlockSpec((bb, WINDOW, KV_WIDTH), lambda bi, n: (bi, 0, 0))
    return pl.pallas_call(
        functools.partial(_swa_attn_kernel, bb=bb, tq=tq, first_block_has_no_prev=False),
        grid=(B // bb, T // tq),
        in_specs=[cur(ATTN_WIDTH), prev, cur(KV_WIDTH), prev, cur(KV_WIDTH), cur(ATTN_WIDTH),
                  pl.BlockSpec((ATTN_Q_HEADS, LANES), lambda bi, n: (0, 0))],
        out_specs=cur(ATTN_WIDTH),
        out_shape=jax.ShapeDtypeStruct((B, T, ATTN_WIDTH), f32),
        compiler_params=_cparams(2),
        name="swa_attn",
    )(q, k_prev, k_cur, v_prev, v_cur, sga, sink_t)


def _out_mix_kernel(x_ref, or_ref, sgr_ref, ma_ref, p_ref, wo_ref, gp_ref, wpg_ref, wpp_ref, gf_ref,
                    y_ref):
    mr = (or_ref[...] * sgr_ref[...]).astype(bf16)
    ma = ma_ref[...].astype(bf16)
    h = (x_ref[...] + _dot(mr, wo_ref[:RWKV_WIDTH, :]) + _dot(ma, wo_ref[RWKV_WIDTH:, :]))
    gate = _sigmoid(_dot(_rmsnorm(h, gp_ref[...]).astype(bf16), wpg_ref[...]))
    h = h + gate * _dot(p_ref[...].astype(bf16), wpp_ref[...])
    y_ref[...] = _rmsnorm(h, gf_ref[...])


def _out_mix(x2d, o_r, sgr, ma, p2d, w_out, g_ple, w_pg, w_pp, g_final, *, tm):
    n_tok = x2d.shape[0]
    const = lambda shape: pl.BlockSpec(shape, lambda i: (0,) * len(shape))
    row = lambda w: pl.BlockSpec((tm, w), lambda i: (i, 0))
    return pl.pallas_call(
        _out_mix_kernel,
        grid=(n_tok // tm,),
        in_specs=[row(D_MODEL), row(RWKV_WIDTH), row(RWKV_WIDTH), row(ATTN_WIDTH), row(PLE_DIM),
                  const((D_MODEL, D_MODEL)), const((1, D_MODEL)), const((D_MODEL, D_MODEL)),
                  const((PLE_DIM, D_MODEL)), const((1, D_MODEL))],
        out_specs=row(D_MODEL),
        out_shape=jax.ShapeDtypeStruct((n_tok, D_MODEL), f32),
        compiler_params=_cparams(1),
        name="out_mix",
    )(x2d, o_r, sgr, ma, p2d, w_out, g_ple, w_pg, w_pp, g_final)


def _rope_tables(pos):
    half = ROPE_DIM // 2
    inv = ROPE_THETA ** (-jnp.arange(half, dtype=f32) / half)
    ang = pos.astype(f32)[:, None] * inv[None, :]
    cos, sin = jnp.cos(ang), jnp.sin(ang)
    n = pos.shape[0]
    ones = jnp.ones((n, HEAD_DIM - ROPE_DIM), f32)
    zeros = jnp.zeros((n, HEAD_DIM - ROPE_DIM), f32)
    zh = jnp.zeros((n, half), f32)
    per_head = jnp.stack([jnp.concatenate([cos, cos, ones], axis=1),
                          jnp.concatenate([-sin, zh, zeros], axis=1),
                          jnp.concatenate([zh, sin, zeros], axis=1)])
    return jnp.tile(per_head, (1, 1, LANES // HEAD_DIM))


def _col_tile(vecs):
    return jnp.broadcast_to(vecs[:, :, None], vecs.shape + (LANES,))


def _lane_param_tiles(vecs, heads_on_lanes):
    ph = vecs.reshape(vecs.shape[0], RWKV_HEADS, HEAD_DIM)
    if heads_on_lanes:
        nb = LANES // RWKV_HEADS
        return jnp.repeat(jnp.swapaxes(ph, 1, 2), nb, axis=2)[None]
    return jnp.broadcast_to(jnp.swapaxes(ph, 0, 1)[:, :, :, None], (RWKV_HEADS, vecs.shape[0], HEAD_DIM, LANES))


def _layer(x, p, pos, s0, shift0, kbuf, vbuf, wts, *, heads_on_lanes, tt, tc, period, tm, att_bb, att_tq):
    B, T, _ = x.shape
    n_tok = B * T
    x2d = x.reshape(n_tok, D_MODEL)

    if shift0 is None:
        shift0_t = jnp.zeros((SHIFT_DIM, LANES), f32)
    else:
        shift0_t = shift0.T
    p_tiles = _lane_param_tiles(wts["rwkv_params"], heads_on_lanes)
    proj_args = (wts["g_norm"], wts["wt_rwkv"], wts["mul_t"], wts["c2_t"], wts["w2t"], wts["a2t"], shift0_t,
                 p_tiles)
    if heads_on_lanes:
        assert s0 is None and shift0 is None
        scan_in, shift_t, wend = _rwkv_proj(x, *proj_args, tt=tt, heads_on_lanes=True, period=period)
        shift_new = shift_t[:, LANES - B:].T
        s0_t = jnp.zeros((1, HEAD_DIM, HEAD_DIM, LANES), f32)
        o_tb, s_fin = _wkv_scan(scan_in, s0_t, wend, p_tiles, tc=tc, period=period, natural_out=True)
        o_r = jnp.swapaxes(o_tb, 0, 1).reshape(n_tok, RWKV_WIDTH)
        s_new = jnp.transpose(s_fin[0].reshape(HEAD_DIM, HEAD_DIM, RWKV_HEADS, B), (3, 2, 1, 0))
    else:
        scan_in, shift_t, wend = _rwkv_proj(jnp.swapaxes(x, 0, 1), *proj_args, tt=tt, heads_on_lanes=False,
                                            period=period)
        shift_new = shift_t.T
        s0_t = jnp.transpose(s0, (1, 3, 2, 0))
        o_scan, s_fin = _wkv_scan(scan_in, s0_t, wend, p_tiles, tc=tc, period=period, natural_out=False)
        o_r = jnp.transpose(o_scan, (3, 1, 0, 2)).reshape(n_tok, RWKV_WIDTH)
        s_new = jnp.transpose(s_fin, (3, 0, 2, 1))

    rope = _rope_tables(pos)
    if kbuf is None:
        sgr3, ma, k_last, v_last = _nat_attn(x, wts["g_norm"], wts["w_nat"], rope, wts["sink_t"],
                                             n_blk=att_tq // WINDOW)
        sgr = sgr3.reshape(n_tok, RWKV_WIDTH)
        k_new = k_last.reshape(B, WINDOW, ATTN_KV_HEADS, HEAD_DIM)
        v_new = v_last.reshape(B, WINDOW, ATTN_KV_HEADS, HEAD_DIM)
    else:
        sgr, q, k, v, sga = _nat_proj(x2d, wts["g_norm"], wts["w_nat"], jnp.tile(rope, (1, tm // T, 1)), tm=tm)
        k3 = k.reshape(B, T, KV_WIDTH)
        v3 = v.reshape(B, T, KV_WIDTH)
        kb = kbuf.reshape(B, WINDOW, KV_WIDTH)
        vb = vbuf.reshape(B, WINDOW, KV_WIDTH)
        ma = _swa_attn(q.reshape(B, T, ATTN_WIDTH), kb, k3, vb, v3, sga.reshape(B, T, ATTN_WIDTH),
                       wts["sink_t"], bb=att_bb, tq=att_tq)
        k_new = jnp.concatenate([kb, k3], axis=1)[:, -WINDOW:].reshape(B, WINDOW, ATTN_KV_HEADS, HEAD_DIM)
        v_new = jnp.concatenate([vb, v3], axis=1)[:, -WINDOW:].reshape(B, WINDOW, ATTN_KV_HEADS, HEAD_DIM)

    y = _out_mix(x2d, o_r, sgr, ma.reshape(n_tok, ATTN_WIDTH), p.reshape(n_tok, PLE_DIM),
                 wts["w_out"], wts["g_ple"], wts["w_pg"], wts["w_pp"], wts["g_final"], tm=tm)
    return y.reshape(B, T, D_MODEL), s_new, shift_new, k_new, v_new


def kernel(x_prompt, x_sample, state_rwkv_wkv, state_rwkv_shift, cache_swa_k, cache_swa_v,
           p_prompt, p_sample, g_norm, w_in, mu_shift, w0, w2, a0, a2, k_k, k_a, r_k,
           ln_w, ln_b, sinks, w_out, g_ple, w_ple_gate, w_ple_proj, g_final):
    assert w_in.shape[0] == 1, "single layer"
    w_in0 = w_in[0]
    head_order = [g * ATTN_GROUP + i for i in range(ATTN_GROUP) for g in range(ATTN_KV_HEADS)]
    cols = jnp.concatenate([jnp.arange(h * HEAD_DIM, (h + 1) * HEAD_DIM) for h in head_order])
    o_q = SHIFT_DIM + RWKV_WIDTH
    o_ga = o_q + ATTN_WIDTH + 2 * KV_WIDTH
    nat_cols = jnp.concatenate([jnp.arange(SHIFT_DIM, o_q), o_q + cols, jnp.arange(o_q + ATTN_WIDTH, o_ga),
                                o_ga + cols])
    out_rows = jnp.concatenate([jnp.arange(RWKV_WIDTH), RWKV_WIDTH + cols])
    mu = mu_shift[0]
    wts = {
        "g_norm": g_norm[0][None, :],
        "wt_rwkv": w_in0[:, :SHIFT_DIM].T.astype(bf16),
        "w_nat": w_in0[:, nat_cols].astype(bf16),
        "mul_t": _col_tile(mu[None, 3 * RWKV_WIDTH:])[0],
        "c2_t": _col_tile(jnp.stack([w0[0], a0[0]])),
        "w2t": w2[0].T.astype(bf16), "a2t": a2[0].T.astype(bf16),
        "rwkv_params": jnp.stack([mu[:RWKV_WIDTH], mu[RWKV_WIDTH:2 * RWKV_WIDTH],
                                  mu[2 * RWKV_WIDTH:3 * RWKV_WIDTH], k_k[0], k_a[0], r_k[0], ln_w[0], ln_b[0]]),
        "sink_t": _col_tile(sinks)[0],
        "w_out": w_out[0][out_rows].astype(bf16), "g_ple": g_ple[0][None, :],
        "w_pg": w_ple_gate[0].astype(bf16), "w_pp": w_ple_proj[0].astype(bf16),
        "g_final": g_final[None, :],
    }
    Bp, Tp, _ = x_prompt.shape
    Bs, Ts, _ = x_sample.shape
    assert Bp * RWKV_HEADS == LANES and Bs == LANES and Tp % WINDOW == 0 and Ts % SUBLANES == 0

    yp, s1, sh1, k1, v1 = _layer(x_prompt, p_prompt[0], jnp.arange(Tp), None, None, None, None, wts,
                                 heads_on_lanes=True, tt=16, tc=32, period=32, tm=1024, att_bb=1, att_tq=4 * WINDOW)
    ys, s2, sh2, k2, v2 = _layer(x_sample, p_sample[0], PAST_LEN + jnp.arange(Ts),
                                 state_rwkv_wkv[0], state_rwkv_shift[0], cache_swa_k[0], cache_swa_v[0],
                                 wts, heads_on_lanes=False, tt=2, tc=Ts, period=Ts, tm=Bs * Ts, att_bb=8, att_tq=Ts)
    return (yp, ys, s1[None], sh1[None], k1[None], v1[None], s2[None], sh2[None], k2[None], v2[None])
```

```python
import functools
import math

import jax
import jax.numpy as jnp
from jax import lax
from jax.experimental import pallas as pl
from jax.experimental.pallas import tpu as pltpu

D_MODEL = 1024
HEAD_DIM = 64
RWKV_WIDTH = 512
RWKV_HEADS = 8
ATTN_WIDTH = 512
ATTN_Q_HEADS = 8
ATTN_KV_HEADS = 2
ATTN_GROUP = 4
KV_WIDTH = 128
LORA = 64
WINDOW = 128
ROPE_THETA = 500000.0
ROPE_DIM = 16
PLE_DIM = 256
NORM_EPS = 1e-6
GN_EPS = 64e-5
NEG_INF = -1e30
PAST_LEN = 16384
SHIFT_DIM = 3 * RWKV_WIDTH + 2 * LORA
NAT_DIM = RWKV_WIDTH + ATTN_WIDTH + 2 * KV_WIDTH + ATTN_WIDTH

LANES = 128
SUBLANES = 8
VMEM_LIMIT = 56 * 1024 * 1024
DECAY_SCALE = math.exp(-0.5)
GROUP_T = 8

Q_KK, Q_W, Q_B, Q_K, Q_R, Q_V = range(6)
NQ = 6
P_MU_R, P_MU_K, P_MU_V, P_KK, P_KA, P_RK, P_LNW, P_LNB = range(8)
NP = 8

f32 = jnp.float32
bf16 = jnp.bfloat16


def _cparams(n_axes):
    return pltpu.CompilerParams(dimension_semantics=("arbitrary",) * n_axes,
                                vmem_limit_bytes=VMEM_LIMIT)


def _rmsnorm(x, g):
    ms = jnp.mean(x * x, axis=-1, keepdims=True)
    return x * lax.rsqrt(ms + NORM_EPS) * g


def _sigmoid(x):
    return 1.0 / (1.0 + jnp.exp(-x))


def _dot_nt(a, b):
    return lax.dot_general(a, b, (((1,), (1,)), ((), ())), preferred_element_type=f32)


def _dot(a, b):
    return jnp.dot(a, b, preferred_element_type=f32)


def _chunk_transpose(xs, chunk):
    lane = lax.broadcasted_iota(jnp.int32, xs[0].shape, 1)
    xs = list(xs)
    for d in (4, 2, 1):
        hi_lanes = (lane & (chunk * d)) != 0
        nxt = list(xs)
        for i in range(8):
            if i & d:
                continue
            lo, hi = xs[i], xs[i + d]
            if 2 * chunk * d == LANES:
                moved = pltpu.roll(jnp.where(hi_lanes, lo, hi), chunk * d, 1)
                nxt[i] = jnp.where(hi_lanes, moved, lo)
                nxt[i + d] = jnp.where(hi_lanes, hi, moved)
            else:
                nxt[i] = jnp.where(hi_lanes, pltpu.roll(hi, chunk * d, 1), lo)
                nxt[i + d] = jnp.where(hi_lanes, hi, pltpu.roll(lo, LANES - chunk * d, 1))
        xs = nxt
    return xs


def _rwkv_proj_kernel(x_ref, g_ref, wt_ref, mul_ref, c2_ref, w2t_ref, a2t_ref, shift0_ref,
                      pt_ref, out_ref, shift_ref, wend_ref,
                      carry_ref, prev_ref, wc_ref, za_ref, zb_ref, *rest, tt, nb, heads_on_lanes, period):
    i = pl.program_id(0)
    groups = 1 if heads_on_lanes else RWKV_HEADS
    lora0 = 3 * RWKV_WIDTH
    n = tt * nb
    first = i == 0

    def carry0():
        return shift0_ref[lora0:, :]

    def prev0(g, q):
        if heads_on_lanes:
            return jnp.zeros((HEAD_DIM, LANES), f32)
        r0 = q * RWKV_WIDTH + g * HEAD_DIM
        return shift0_ref[r0:r0 + HEAD_DIM, :]

    @pl.when(first)
    def _():
        zb_ref[...] = jnp.zeros(zb_ref.shape, f32)
        carry_ref[...] = carry0()
        wc_ref[...] = jnp.ones(wc_ref.shape, f32)
        for g in range(groups):
            for q in range(3):
                prev_ref[g, q] = prev0(g, q)

    def project(z_ref):
        if heads_on_lanes:
            u_ref = rest[0]
            n_col = D_MODEL // LANES
            for bi in range(nb):
                ub = _rmsnorm(x_ref[bi], g_ref[...])
                for ci in range(n_col):
                    u_ref[ci, pl.ds(bi, tt, stride=nb), :] = ub[:, ci * LANES:(ci + 1) * LANES]
            u = jnp.concatenate([u_ref[ci] for ci in range(n_col)], axis=1).astype(bf16)
        else:
            u = _rmsnorm(x_ref[...].reshape(n, D_MODEL), g_ref[...]).astype(bf16)
        z_ref[...] = _dot_nt(wt_ref[...], u)

    def post(z_ref):
        tile = i - 1
        lane = lax.broadcasted_iota(jnp.int32, (2 * LORA, LANES), 1)
        ones = jnp.ones((HEAD_DIM, LANES), f32)
        prev = [[prev_ref[g, q] for q in range(3)] for g in range(groups)]
        wc = [wc_ref[g] for g in range(groups)]
        prev_rot = carry_ref[...]
        steps_per_blk = LANES // nb if heads_on_lanes else 1
        for j in range(n // LANES):
            z = z_ref[:, j * LANES:(j + 1) * LANES]
            zl = z[lora0:]
            if nb % LANES == 0:
                prev_l, zl_rot = prev_rot, zl
            else:
                zl_rot = pltpu.roll(zl, nb, 1)
                prev_l = jnp.where(lane < nb, prev_rot, zl_rot)
            prev_rot = zl_rot
            zls = zl + mul_ref[...] * (prev_l - zl)
            w_pre = c2_ref[0] + _dot(w2t_ref[...], jnp.tanh(zls[:LORA]).astype(bf16))
            decay = jnp.exp(-DECAY_SCALE * _sigmoid(w_pre))
            a_all = _sigmoid(c2_ref[1] + _dot(a2t_ref[...], zls[LORA:].astype(bf16)))
            raw = [z[0:RWKV_WIDTH], z[RWKV_WIDTH:2 * RWKV_WIDTH], z[2 * RWKV_WIDTH:lora0], decay, a_all]
            per_head = [[val[h * HEAD_DIM:(h + 1) * HEAD_DIM] for h in range(RWKV_HEADS)] for val in raw]
            if heads_on_lanes:
                tiles = [_chunk_transpose(ph, nb) for ph in per_head]
            for ls in range(steps_per_blk):
                local = j * steps_per_blk + ls
                for g in range(groups):
                    idx = ls if heads_on_lanes else g
                    r_raw, kx_raw, v_raw, w_t, a_t = (tiles[q][idx] if heads_on_lanes else per_head[q][idx]
                                                      for q in range(5))
                    r = r_raw + pt_ref[g, P_MU_R] * (prev[g][0] - r_raw)
                    kx = kx_raw + pt_ref[g, P_MU_K] * (prev[g][1] - kx_raw)
                    v = v_raw + pt_ref[g, P_MU_V] * (prev[g][2] - v_raw)
                    prev[g] = [r_raw, kx_raw, v_raw]
                    kkr = kx * pt_ref[g, P_KK]
                    ss = jnp.sum(kkr * kkr, axis=0, keepdims=True)
                    kk = kkr * (1.0 / jnp.maximum(jnp.sqrt(ss), 1e-12))
                    k = kx * (1.0 + (a_t - 1.0) * pt_ref[g, P_KA])
                    b = kk * a_t
                    wc_in = wc[g]
                    if tt >= period:
                        wc_base = ones if local % period == 0 else wc_in
                    elif local == 0:
                        wc_base = jnp.where(tile % (period // tt) == 0, ones, wc_in)
                    else:
                        wc_base = wc_in
                    wc_t = wc_base * w_t
                    inv_wc = 1.0 / wc_t
                    step = (g, local)
                    out_ref[step + (Q_KK,)] = kk * wc_base
                    out_ref[step + (Q_W,)] = wc_in
                    out_ref[step + (Q_B,)] = b * inv_wc
                    out_ref[step + (Q_K,)] = k * inv_wc
                    out_ref[step + (Q_R,)] = r * wc_t
                    out_ref[step + (Q_V,)] = v
                    wc[g] = wc_t
        carry_ref[...] = jnp.where(first, carry0(), prev_rot)
        shift_ref[...] = z
        for g in range(groups):
            wc_g = jnp.where(first, ones, wc[g])
            wc_ref[g] = wc_g
            wend_ref[g] = wc_g
            for q in range(3):
                prev_ref[g, q] = jnp.where(first, prev0(g, q), prev[g][q])

    @pl.when(i % 2 == 0)
    def _():
        project(za_ref)
        post(zb_ref)

    @pl.when(i % 2 == 1)
    def _():
        project(zb_ref)
        post(za_ref)


def _rwkv_proj(x, g_norm, wt, mul_t, c2_t, w2t, a2t, shift0_t, p_tiles, *, tt, heads_on_lanes, period):
    const = lambda shape: pl.BlockSpec(shape, lambda i: (0,) * len(shape))
    groups = 1 if heads_on_lanes else RWKV_HEADS
    if heads_on_lanes:
        nb, T, _ = x.shape
    else:
        T, nb, _ = x.shape
    n_tiles = T // tt
    proj_tile = lambda i: jnp.minimum(i, n_tiles - 1)
    post_tile = lambda i: jnp.maximum(i - 1, 0)
    scratch = [pltpu.VMEM((2 * LORA, LANES), f32), pltpu.VMEM((groups, 3, HEAD_DIM, LANES), f32),
               pltpu.VMEM((groups, HEAD_DIM, LANES), f32),
               pltpu.VMEM((SHIFT_DIM, tt * nb), f32), pltpu.VMEM((SHIFT_DIM, tt * nb), f32)]
    if heads_on_lanes:
        x_spec = pl.BlockSpec((nb, tt, D_MODEL), lambda i: (0, proj_tile(i), 0))
        scratch.append(pltpu.VMEM((D_MODEL // LANES, tt * nb, LANES), f32))
    else:
        x_spec = pl.BlockSpec((tt, nb, D_MODEL), lambda i: (proj_tile(i), 0, 0))
    assert period % tt == 0 or tt % period == 0
    kern = functools.partial(_rwkv_proj_kernel, tt=tt, nb=nb, heads_on_lanes=heads_on_lanes, period=period)
    return pl.pallas_call(
        kern,
        grid=(n_tiles + 1,),
        in_specs=[
            x_spec,
            const((1, D_MODEL)),
            const((SHIFT_DIM, D_MODEL)),
            const((2 * LORA, LANES)),
            const((2, RWKV_WIDTH, LANES)),
            const((RWKV_WIDTH, LORA)), const((RWKV_WIDTH, LORA)),
            const((SHIFT_DIM, LANES)),
            const((groups, NP, HEAD_DIM, LANES)),
        ],
        out_specs=[pl.BlockSpec((groups, tt, NQ, HEAD_DIM, LANES), lambda i: (0, post_tile(i), 0, 0, 0)),
                   const((SHIFT_DIM, LANES)), const((groups, HEAD_DIM, LANES))],
        out_shape=[jax.ShapeDtypeStruct((groups, T, NQ, HEAD_DIM, LANES), f32),
                   jax.ShapeDtypeStruct((SHIFT_DIM, LANES), f32),
                   jax.ShapeDtypeStruct((groups, HEAD_DIM, LANES), f32)],
        scratch_shapes=scratch,
        compiler_params=_cparams(1),
        name="rwkv_proj",
    )(x, g_norm, wt, mul_t, c2_t, w2t, a2t, shift0_t, p_tiles)


def _rope(x, c, a, b):
    return x * c + pltpu.roll(x, LANES - ROPE_DIM // 2, 1) * a + pltpu.roll(x, ROPE_DIM // 2, 1) * b


def _nat_proj_kernel(x_ref, g_ref, w_ref, rope_ref, sgr_ref, q_ref, k_ref, v_ref, sga_ref):
    u = _rmsnorm(x_ref[...], g_ref[...]).astype(bf16)
    z = _dot(u, w_ref[...])
    o_q = RWKV_WIDTH
    o_k = o_q + ATTN_WIDTH
    o_v = o_k + KV_WIDTH
    o_g = o_v + KV_WIDTH
    gr = z[:, :o_q]
    sgr_ref[...] = gr * _sigmoid(gr)
    rc, ra, rb = rope_ref[0], rope_ref[1], rope_ref[2]
    for j in range(ATTN_WIDTH // LANES):
        qj = z[:, o_q + j * LANES:o_q + (j + 1) * LANES]
        q_ref[:, j * LANES:(j + 1) * LANES] = _rope(qj, rc, ra, rb) * (HEAD_DIM ** -0.5)
    k_ref[...] = _rope(z[:, o_k:o_v], rc, ra, rb)
    v_ref[...] = z[:, o_v:o_g]
    ga = z[:, o_g:]
    sga_ref[...] = ga * _sigmoid(ga)


def _nat_proj(x2d, g_norm, w_nat, rope, *, tm):
    n_tok = x2d.shape[0]
    n_tab = rope.shape[1] // tm
    const = lambda shape: pl.BlockSpec(shape, lambda i: (0,) * len(shape))
    row = lambda w: pl.BlockSpec((tm, w), lambda i: (i, 0))
    tab = pl.BlockSpec((3, tm, LANES), lambda i: (0, i % n_tab, 0))
    return pl.pallas_call(
        _nat_proj_kernel,
        grid=(n_tok // tm,),
        in_specs=[row(D_MODEL), const((1, D_MODEL)), const((D_MODEL, NAT_DIM)), tab],
        out_specs=[row(RWKV_WIDTH), row(ATTN_WIDTH), row(KV_WIDTH), row(KV_WIDTH), row(ATTN_WIDTH)],
        out_shape=[jax.ShapeDtypeStruct((n_tok, w), f32)
                   for w in (RWKV_WIDTH, ATTN_WIDTH, KV_WIDTH, KV_WIDTH, ATTN_WIDTH)],
        compiler_params=_cparams(1),
        name="nat_proj",
    )(x2d, g_norm, w_nat, rope)


def _row_bcast(ref, idx, k):
    return jnp.broadcast_to(ref[idx + (pl.ds(k, 1), slice(None))], (HEAD_DIM, LANES))


def _wkv_step(s_ref, ref, at, at_next, sa, rk, lnw, lnb):
    vv = ref[at + (Q_V,)]
    y = jnp.zeros((HEAD_DIM, LANES), f32)
    sa_next = jnp.zeros((HEAD_DIM, LANES), f32)
    for k in range(HEAD_DIM):
        s_new = (s_ref[k] - sa * _row_bcast(ref, at + (Q_B,), k)
                 + vv * _row_bcast(ref, at + (Q_K,), k))
        s_ref[k] = s_new
        y = y + s_new * _row_bcast(ref, at + (Q_R,), k)
        sa_next = sa_next + s_new * _row_bcast(ref, at_next + (Q_KK,), k)
    mean = jnp.mean(y, axis=0, keepdims=True)
    d = y - mean
    var = jnp.mean(d * d, axis=0, keepdims=True)
    yn = d * lax.rsqrt(var + GN_EPS) * lnw + lnb
    rkk = jnp.sum(ref[at + (Q_R,)] * ref[at + (Q_K,)] * rk, axis=0, keepdims=True)
    return yn + rkk * vv, sa_next


def _wkv_scan_kernel(in_ref, s0_ref, wend_ref, pt_ref, o_ref, sout_ref, s_ref, *rest,
                     tc, period, natural_out):
    c = pl.program_id(1)

    @pl.when(c == 0)
    def _():
        s_ref[...] = s0_ref[0]

    obuf_ref = rest[0] if natural_out else None
    rk, lnw, lnb = pt_ref[0, P_RK], pt_ref[0, P_LNW], pt_ref[0, P_LNB]

    for t0 in range(0, tc, period):
        sa0 = jnp.zeros((HEAD_DIM, LANES), f32)
        for k in range(HEAD_DIM):
            s_k = s_ref[k] * _row_bcast(in_ref, (0, t0, Q_W), k)
            s_ref[k] = s_k
            sa0 = sa0 + s_k * _row_bcast(in_ref, (0, t0, Q_KK), k)

        def step(t, sa, last=t0 + period - 1):
            t_next = jnp.minimum(t + 1, last)
            o, sa_next = _wkv_step(s_ref, in_ref, (0, t), (0, t_next), sa, rk, lnw, lnb)
            if natural_out:
                obuf_ref[t] = o
            else:
                o_ref[0, t] = o
            return sa_next

        lax.fori_loop(t0, t0 + period, step, sa0)

    if natural_out:
        nb = LANES // RWKV_HEADS
        for j in range(tc // GROUP_T):
            per_h = _chunk_transpose([obuf_ref[j * GROUP_T + t] for t in range(GROUP_T)], nb)
            ot = jnp.concatenate(per_h, axis=0)
            o_ref[j * GROUP_T:(j + 1) * GROUP_T] = ot.T.reshape(GROUP_T, nb, RWKV_WIDTH)

    @pl.when(c == pl.num_programs(1) - 1)
    def _():
        for k in range(HEAD_DIM):
            sout_ref[0, k] = s_ref[k] * _row_bcast(wend_ref, (0,), k)


def _wkv_scan(scan_in, s0, wend, p_tiles, *, tc, period, natural_out):
    groups, T = scan_in.shape[:2]
    assert tc % period == 0
    tile = pl.BlockSpec((1, HEAD_DIM, LANES), lambda g, c: (g, 0, 0))
    state = pl.BlockSpec((1, HEAD_DIM, HEAD_DIM, LANES), lambda g, c: (g, 0, 0, 0))
    scratch = [pltpu.VMEM((HEAD_DIM, HEAD_DIM, LANES), f32)]
    if natural_out:
        nb = LANES // RWKV_HEADS
        o_spec = pl.BlockSpec((tc, nb, RWKV_WIDTH), lambda g, c: (c, 0, 0))
        o_shape = jax.ShapeDtypeStruct((T, nb, RWKV_WIDTH), f32)
        scratch.append(pltpu.VMEM((tc, HEAD_DIM, LANES), f32))
    else:
        o_spec = pl.BlockSpec((1, tc, HEAD_DIM, LANES), lambda g, c: (g, c, 0, 0))
        o_shape = jax.ShapeDtypeStruct((groups, T, HEAD_DIM, LANES), f32)
    return pl.pallas_call(
        functools.partial(_wkv_scan_kernel, tc=tc, period=period, natural_out=natural_out),
        grid=(groups, T // tc),
        in_specs=[
            pl.BlockSpec((1, tc, NQ, HEAD_DIM, LANES), lambda g, c: (g, c, 0, 0, 0)),
            state, tile, pl.BlockSpec((1, NP, HEAD_DIM, LANES), lambda g, c: (g, 0, 0, 0)),
        ],
        out_specs=[o_spec, state],
        out_shape=[o_shape, jax.ShapeDtypeStruct((groups, HEAD_DIM, HEAD_DIM, LANES), f32)],
        scratch_shapes=scratch,
        compiler_params=_cparams(2),
        name="wkv_scan",
    )(scan_in, s0, wend, p_tiles)


def _swa_attn_kernel(q_ref, kp_ref, kc_ref, vp_ref, vc_ref, sga_ref, sink_ref, o_ref, *,
                     bb, tq, first_block_has_no_prev):
    n = pl.program_id(1)
    nr = bb * tq
    half = LANES // 2
    tq_bits = tq.bit_length() - 1

    def key_mask(n_keys_per_batch, is_prev):
        rows = lax.broadcasted_iota(jnp.int32, (nr, bb * n_keys_per_batch), 0)
        cols = lax.broadcasted_iota(jnp.int32, (nr, bb * n_keys_per_batch), 1)
        i = rows & (tq - 1)
        j = cols & (n_keys_per_batch - 1)
        ok = (j > i) if is_prev else (j <= i)
        if bb > 1:
            same = (rows >> tq_bits) == (cols >> (n_keys_per_batch.bit_length() - 1))
            ok = jnp.logical_and(same, ok)
        if is_prev and first_block_has_no_prev:
            ok = jnp.logical_and(ok, n > 0)
        return jnp.tile(ok, (ATTN_GROUP, 1))

    mask_p = key_mask(WINDOW, True)
    mask_c = key_mask(tq, False)
    lane_q = lax.broadcasted_iota(jnp.int32, (nr, LANES), 1)
    lane_kp = lax.broadcasted_iota(jnp.int32, (bb * WINDOW, LANES), 1)

    kp = kp_ref[...].reshape(bb * WINDOW, LANES).astype(bf16)
    kc = kc_ref[...].reshape(nr, LANES).astype(bf16)
    vp = vp_ref[...].reshape(bb * WINDOW, LANES)
    vc = vc_ref[...].reshape(nr, LANES)

    res = []
    for g in range(ATTN_KV_HEADS):
        in_g = (lambda lane: lane < half) if g == 0 else (lambda lane: lane >= half)
        qs, sinks = [], []
        for hh in range(ATTN_GROUP):
            x = q_ref[:, :, hh * LANES:(hh + 1) * LANES].reshape(nr, LANES)
            qs.append(jnp.where(in_g(lane_q), x, 0.0).astype(bf16))
            sinks.append(jnp.broadcast_to(sink_ref[g * ATTN_GROUP + hh:g * ATTN_GROUP + hh + 1, :], (nr, LANES)))
        qg = jnp.concatenate(qs, axis=0)
        sink = jnp.concatenate(sinks, axis=0)[:, 0:1]
        sp = jnp.where(mask_p, _dot_nt(qg, kp), NEG_INF)
        sc = jnp.where(mask_c, _dot_nt(qg, kc), NEG_INF)
        m = jnp.maximum(jnp.maximum(jnp.max(sp, axis=-1, keepdims=True),
                                    jnp.max(sc, axis=-1, keepdims=True)), sink)
        pp = jnp.exp(sp - m).astype(bf16)
        pc = jnp.exp(sc - m).astype(bf16)
        e_sink = jnp.exp(sink - m)
        vpg = jnp.where(in_g(lane_kp), vp, 1.0).astype(bf16)
        vcg = jnp.where(in_g(lane_q), vc, 1.0).astype(bf16)
        pv = _dot(pp, vpg) + _dot(pc, vcg)
        res.append(pv * (1.0 / (pltpu.roll(pv, half, 1) + e_sink)))
    for hh in range(ATTN_GROUP):
        blk = slice(hh * nr, (hh + 1) * nr)
        cs = slice(hh * LANES, (hh + 1) * LANES)
        out = jnp.where(lane_q < half, res[0][blk], res[1][blk])
        o_ref[:, :, cs] = (out.reshape(bb, tq, LANES) * sga_ref[:, :, cs]).astype(o_ref.dtype)


def _attn_block_t(q_cols, k_all, v_all, ok, sink_ref):
    tq = WINDOW
    half = LANES // 2
    lane_q = lax.broadcasted_iota(jnp.int32, (tq, LANES), 1)
    lane_k = lax.broadcasted_iota(jnp.int32, (2 * WINDOW, LANES), 1)
    mask_t = jnp.tile(ok, (1, ATTN_GROUP))
    k_bf = k_all.astype(bf16)
    norm_t = []
    for g in range(ATTN_KV_HEADS):
        in_g = (lambda lane: lane < half) if g == 0 else (lambda lane: lane >= half)
        qs = [jnp.where(in_g(lane_q), x, jnp.zeros_like(x)).astype(bf16) for x in q_cols]
        sinks = [sink_ref[g * ATTN_GROUP + hh:g * ATTN_GROUP + hh + 1, :] for hh in range(ATTN_GROUP)]
        qg = jnp.concatenate(qs, axis=0)
        sink = jnp.concatenate(sinks, axis=1)
        st = jnp.where(mask_t, _dot_nt(k_bf, qg), NEG_INF)
        m = jnp.maximum(jnp.max(st, axis=0, keepdims=True), sink)
        p = jnp.exp(st - m).astype(bf16)
        e_sink = jnp.exp(sink - m)
        vg = jnp.where(in_g(lane_k), v_all, 1.0).astype(bf16)
        ot = lax.dot_general(vg, p, (((0,), (0,)), ((), ())), preferred_element_type=f32)
        lo, hi = ot[:half], ot[half:]
        num, den = (lo, hi) if g == 0 else (hi, lo)
        norm_t.append(num * (1.0 / (den + e_sink)))
    return [jnp.concatenate([norm_t[0][:, hh * tq:(hh + 1) * tq],
                             norm_t[1][:, hh * tq:(hh + 1) * tq]], axis=0).T for hh in range(ATTN_GROUP)]


def _nat_attn_kernel(x_ref, g_ref, w_ref, rope_ref, sink_ref,
                     sgr_ref, ma_ref, klast_ref, vlast_ref, kprev_ref, vprev_ref, *bufs, n_blk, tiles_per_seq):
    s = pl.program_id(0)
    tq = WINDOW
    buf_a, buf_b = bufs[:4], bufs[4:]

    @pl.when(s == 0)
    def _():
        kprev_ref[...] = jnp.zeros(kprev_ref.shape, f32)
        vprev_ref[...] = jnp.zeros(vprev_ref.shape, f32)
        for ref in buf_b:
            ref[...] = jnp.zeros(ref.shape, ref.dtype)

    def project(q_ref, k_ref, v_ref, sg_ref):
        u = _rmsnorm(x_ref[0], g_ref[...]).astype(bf16)
        z = _dot(u, w_ref[...])
        o_q = RWKV_WIDTH
        o_k = o_q + ATTN_WIDTH
        o_v = o_k + KV_WIDTH
        o_g = o_v + KV_WIDTH
        gr = z[:, :o_q]
        sgr_ref[0] = gr * _sigmoid(gr)
        rc, ra, rb = rope_ref[0], rope_ref[1], rope_ref[2]
        for j in range(ATTN_WIDTH // LANES):
            qj = z[:, o_q + j * LANES:o_q + (j + 1) * LANES]
            q_ref[j] = (_rope(qj, rc, ra, rb) * (HEAD_DIM ** -0.5)).astype(bf16)
        k = _rope(z[:, o_k:o_v], rc, ra, rb)
        v = z[:, o_v:o_g]
        k_ref[...] = k
        v_ref[...] = v
        ga = z[:, o_g:]
        sg_ref[...] = ga * _sigmoid(ga)
        last = slice((n_blk - 1) * tq, n_blk * tq)
        klast_ref[0] = k[last]
        vlast_ref[0] = v[last]

    def attend(q_ref, k_ref, v_ref, sg_ref):
        seq_start = (s - 1) % tiles_per_seq == 0
        keys = lax.broadcasted_iota(jnp.int32, (2 * WINDOW, tq), 0)
        qi = lax.broadcasted_iota(jnp.int32, (2 * WINDOW, tq), 1)
        prev_ok = jnp.logical_and(keys < WINDOW, keys > qi)
        cur_ok = jnp.logical_and(keys >= WINDOW, keys - WINDOW <= qi)
        for blk in range(n_blk):
            rows = slice(blk * tq, (blk + 1) * tq)
            if blk == 0:
                kp, vp = kprev_ref[...], vprev_ref[...]
                ok = jnp.logical_or(jnp.logical_and(prev_ok, jnp.logical_not(seq_start)), cur_ok)
            else:
                before = slice((blk - 1) * tq, blk * tq)
                kp, vp = k_ref[before, :], v_ref[before, :]
                ok = jnp.logical_or(prev_ok, cur_ok)
            outs = _attn_block_t([q_ref[j, rows, :] for j in range(ATTN_GROUP)],
                                 jnp.concatenate([kp, k_ref[rows, :]], axis=0),
                                 jnp.concatenate([vp, v_ref[rows, :]], axis=0), ok, sink_ref)
            for hh in range(ATTN_GROUP):
                cs = slice(hh * LANES, (hh + 1) * LANES)
                ma_ref[0, rows, cs] = (outs[hh] * sg_ref[rows, cs]).astype(ma_ref.dtype)
        last = slice((n_blk - 1) * tq, n_blk * tq)
        kprev_ref[...] = k_ref[last, :]
        vprev_ref[...] = v_ref[last, :]

    @pl.when(s % 2 == 0)
    def _():
        project(*buf_a)
        attend(*buf_b)

    @pl.when(s % 2 == 1)
    def _():
        project(*buf_b)
        attend(*buf_a)


def _nat_attn(x, g_norm, w_nat, rope, sink_t, *, n_blk):
    B, T, _ = x.shape
    tm = n_blk * WINDOW
    tps = T // tm
    n_tiles = B * tps
    proj_tile = lambda s: jnp.minimum(s, n_tiles - 1)
    attn_tile = lambda s: jnp.maximum(s - 1, 0)
    const = lambda shape: pl.BlockSpec(shape, lambda s: (0,) * len(shape))
    tab = pl.BlockSpec((3, tm, LANES), lambda s: (0, proj_tile(s) % tps, 0))
    last = pl.BlockSpec((1, WINDOW, KV_WIDTH), lambda s: (proj_tile(s) // tps, 0, 0))
    buf = [pltpu.VMEM((ATTN_WIDTH // LANES, tm, LANES), bf16), pltpu.VMEM((tm, KV_WIDTH), f32),
           pltpu.VMEM((tm, KV_WIDTH), f32), pltpu.VMEM((tm, ATTN_WIDTH), f32)]
    sgr, ma, k_last, v_last = pl.pallas_call(
        functools.partial(_nat_attn_kernel, n_blk=n_blk, tiles_per_seq=tps),
        grid=(n_tiles + 1,),
        in_specs=[pl.BlockSpec((1, tm, D_MODEL), lambda s: (proj_tile(s), 0, 0)),
                  const((1, D_MODEL)), const((D_MODEL, NAT_DIM)), tab,
                  const((ATTN_Q_HEADS, LANES))],
        out_specs=[pl.BlockSpec((1, tm, RWKV_WIDTH), lambda s: (proj_tile(s), 0, 0)),
                   pl.BlockSpec((1, tm, ATTN_WIDTH), lambda s: (attn_tile(s), 0, 0)), last, last],
        out_shape=[jax.ShapeDtypeStruct((n_tiles, tm, RWKV_WIDTH), f32),
                   jax.ShapeDtypeStruct((n_tiles, tm, ATTN_WIDTH), bf16),
                   jax.ShapeDtypeStruct((B, WINDOW, KV_WIDTH), f32),
                   jax.ShapeDtypeStruct((B, WINDOW, KV_WIDTH), f32)],
        scratch_shapes=[pltpu.VMEM((WINDOW, KV_WIDTH), f32), pltpu.VMEM((WINDOW, KV_WIDTH), f32)] + buf + buf,
        compiler_params=_cparams(1),
        name="nat_attn",
    )(x.reshape(n_tiles, tm, D_MODEL), g_norm, w_nat, rope, sink_t)
    return sgr.reshape(B, T, RWKV_WIDTH), ma.reshape(B, T, ATTN_WIDTH), k_last, v_last


def _swa_attn(q, k_prev, k_cur, v_prev, v_cur, sga, sink_t, *, bb, tq):
    B, T, _ = q.shape
    cur = lambda w: pl.BlockSpec((bb, tq, w), lambda bi, n: (bi, n, 0))
    prev = pl.BlockSpec((bb, WINDOW, KV_WIDTH), lambda bi, n: (bi, 0, 0))
    return pl.pallas_call(
        functools.partial(_swa_attn_kernel, bb=bb, tq=tq, first_block_has_no_prev=False),
        grid=(B // bb, T // tq),
        in_specs=[cur(ATTN_WIDTH), prev, cur(KV_WIDTH), prev, cur(KV_WIDTH), cur(ATTN_WIDTH),
                  pl.BlockSpec((ATTN_Q_HEADS, LANES), lambda bi, n: (0, 0))],
        out_specs=cur(ATTN_WIDTH),
        out_shape=jax.ShapeDtypeStruct((B, T, ATTN_WIDTH), f32),
        compiler_params=_cparams(2),
        name="swa_attn",
    )(q, k_prev, k_cur, v_prev, v_cur, sga, sink_t)


def _out_mix_kernel(x_ref, or_ref, sgr_ref, ma_ref, p_ref, wo_ref, gp_ref, wpg_ref, wpp_ref, gf_ref,
                    y_ref):
    mr = (or_ref[...] * sgr_ref[...]).astype(bf16)
    ma = ma_ref[...].astype(bf16)
    h = (x_ref[...] + _dot(mr, wo_ref[:RWKV_WIDTH, :]) + _dot(ma, wo_ref[RWKV_WIDTH:, :]))
    gate = _sigmoid(_dot(_rmsnorm(h, gp_ref[...]).astype(bf16), wpg_ref[...]))
    h = h + gate * _dot(p_ref[...].astype(bf16), wpp_ref[...])
    y_ref[...] = _rmsnorm(h, gf_ref[...])


def _out_mix(x2d, o_r, sgr, ma, p2d, w_out, g_ple, w_pg, w_pp, g_final, *, tm):
    n_tok = x2d.shape[0]
    const = lambda shape: pl.BlockSpec(shape, lambda i: (0,) * len(shape))
    row = lambda w: pl.BlockSpec((tm, w), lambda i: (i, 0))
    return pl.pallas_call(
        _out_mix_kernel,
        grid=(n_tok // tm,),
        in_specs=[row(D_MODEL), row(RWKV_WIDTH), row(RWKV_WIDTH), row(ATTN_WIDTH), row(PLE_DIM),
                  const((D_MODEL, D_MODEL)), const((1, D_MODEL)), const((D_MODEL, D_MODEL)),
                  const((PLE_DIM, D_MODEL)), const((1, D_MODEL))],
        out_specs=row(D_MODEL),
        out_shape=jax.ShapeDtypeStruct((n_tok, D_MODEL), f32),
        compiler_params=_cparams(1),
        name="out_mix",
    )(x2d, o_r, sgr, ma, p2d, w_out, g_ple, w_pg, w_pp, g_final)


def _rope_tables(pos):
    half = ROPE_DIM // 2
    inv = ROPE_THETA ** (-jnp.arange(half, dtype=f32) / half)
    ang = pos.astype(f32)[:, None] * inv[None, :]
    cos, sin = jnp.cos(ang), jnp.sin(ang)
    n = pos.shape[0]
    ones = jnp.ones((n, HEAD_DIM - ROPE_DIM), f32)
    zeros = jnp.zeros((n, HEAD_DIM - ROPE_DIM), f32)
    zh = jnp.zeros((n, half), f32)
    per_head = jnp.stack([jnp.concatenate([cos, cos, ones], axis=1),
                          jnp.concatenate([-sin, zh, zeros], axis=1),
                          jnp.concatenate([zh, sin, zeros], axis=1)])
    return jnp.tile(per_head, (1, 1, LANES // HEAD_DIM))


def _col_tile(vecs):
    return jnp.broadcast_to(vecs[:, :, None], vecs.shape + (LANES,))


def _lane_param_tiles(vecs, heads_on_lanes):
    ph = vecs.reshape(vecs.shape[0], RWKV_HEADS, HEAD_DIM)
    if heads_on_lanes:
        nb = LANES // RWKV_HEADS
        return jnp.repeat(jnp.swapaxes(ph, 1, 2), nb, axis=2)[None]
    return jnp.broadcast_to(jnp.swapaxes(ph, 0, 1)[:, :, :, None], (RWKV_HEADS, vecs.shape[0], HEAD_DIM, LANES))


def _layer(x, p, pos, s0, shift0, kbuf, vbuf, wts, *, heads_on_lanes, tt, tc, period, tm, att_bb, att_tq):
    B, T, _ = x.shape
    n_tok = B * T
    x2d = x.reshape(n_tok, D_MODEL)

    if shift0 is None:
        shift0_t = jnp.zeros((SHIFT_DIM, LANES), f32)
    else:
        shift0_t = shift0.T
    p_tiles = _lane_param_tiles(wts["rwkv_params"], heads_on_lanes)
    proj_args = (wts["g_norm"], wts["wt_rwkv"], wts["mul_t"], wts["c2_t"], wts["w2t"], wts["a2t"], shift0_t,
                 p_tiles)
    if heads_on_lanes:
        assert s0 is None and shift0 is None
        scan_in, shift_t, wend = _rwkv_proj(x, *proj_args, tt=tt, heads_on_lanes=True, period=period)
        shift_new = shift_t[:, LANES - B:].T
        s0_t = jnp.zeros((1, HEAD_DIM, HEAD_DIM, LANES), f32)
        o_tb, s_fin = _wkv_scan(scan_in, s0_t, wend, p_tiles, tc=tc, period=period, natural_out=True)
        o_r = jnp.swapaxes(o_tb, 0, 1).reshape(n_tok, RWKV_WIDTH)
        s_new = jnp.transpose(s_fin[0].reshape(HEAD_DIM, HEAD_DIM, RWKV_HEADS, B), (3, 2, 1, 0))
    else:
        scan_in, shift_t, wend = _rwkv_proj(jnp.swapaxes(x, 0, 1), *proj_args, tt=tt, heads_on_lanes=False,
                                            period=period)
        shift_new = shift_t.T
        s0_t = jnp.transpose(s0, (1, 3, 2, 0))
        o_scan, s_fin = _wkv_scan(scan_in, s0_t, wend, p_tiles, tc=tc, period=period, natural_out=False)
        o_r = jnp.transpose(o_scan, (3, 1, 0, 2)).reshape(n_tok, RWKV_WIDTH)
        s_new = jnp.transpose(s_fin, (3, 0, 2, 1))

    rope = _rope_tables(pos)
    if kbuf is None:
        sgr3, ma, k_last, v_last = _nat_attn(x, wts["g_norm"], wts["w_nat"], rope, wts["sink_t"],
                                             n_blk=att_tq // WINDOW)
        sgr = sgr3.reshape(n_tok, RWKV_WIDTH)
        k_new = k_last.reshape(B, WINDOW, ATTN_KV_HEADS, HEAD_DIM)
        v_new = v_last.reshape(B, WINDOW, ATTN_KV_HEADS, HEAD_DIM)
    else:
        sgr, q, k, v, sga = _nat_proj(x2d, wts["g_norm"], wts["w_nat"], jnp.tile(rope, (1, tm // T, 1)), tm=tm)
        k3 = k.reshape(B, T, KV_WIDTH)
        v3 = v.reshape(B, T, KV_WIDTH)
        kb = kbuf.reshape(B, WINDOW, KV_WIDTH)
        vb = vbuf.reshape(B, WINDOW, KV_WIDTH)
        ma = _swa_attn(q.reshape(B, T, ATTN_WIDTH), kb, k3, vb, v3, sga.reshape(B, T, ATTN_WIDTH),
                       wts["sink_t"], bb=att_bb, tq=att_tq)
        k_new = jnp.concatenate([kb, k3], axis=1)[:, -WINDOW:].reshape(B, WINDOW, ATTN_KV_HEADS, HEAD_DIM)
        v_new = jnp.concatenate([vb, v3], axis=1)[:, -WINDOW:].reshape(B, WINDOW, ATTN_KV_HEADS, HEAD_DIM)

    y = _out_mix(x2d, o_r, sgr, ma.reshape(n_tok, ATTN_WIDTH), p.reshape(n_tok, PLE_DIM),
                 wts["w_out"], wts["g_ple"], wts["w_pg"], wts["w_pp"], wts["g_final"], tm=tm)
    return y.reshape(B, T, D_MODEL), s_new, shift_new, k_new, v_new


def kernel(x_prompt, x_sample, state_rwkv_wkv, state_rwkv_shift, cache_swa_k, cache_swa_v,
           p_prompt, p_sample, g_norm, w_in, mu_shift, w0, w2, a0, a2, k_k, k_a, r_k,
           ln_w, ln_b, sinks, w_out, g_ple, w_ple_gate, w_ple_proj, g_final):
    assert w_in.shape[0] == 1, "single layer"
    w_in0 = w_in[0]
    head_order = [g * ATTN_GROUP + i for i in range(ATTN_GROUP) for g in range(ATTN_KV_HEADS)]
    cols = jnp.concatenate([jnp.arange(h * HEAD_DIM, (h + 1) * HEAD_DIM) for h in head_order])
    o_q = SHIFT_DIM + RWKV_WIDTH
    o_ga = o_q + ATTN_WIDTH + 2 * KV_WIDTH
    nat_cols = jnp.concatenate([jnp.arange(SHIFT_DIM, o_q), o_q + cols, jnp.arange(o_q + ATTN_WIDTH, o_ga),
                                o_ga + cols])
    out_rows = jnp.concatenate([jnp.arange(RWKV_WIDTH), RWKV_WIDTH + cols])
    mu = mu_shift[0]
    wts = {
        "g_norm": g_norm[0][None, :],
        "wt_rwkv": w_in0[:, :SHIFT_DIM].T.astype(bf16),
        "w_nat": w_in0[:, nat_cols].astype(bf16),
        "mul_t": _col_tile(mu[None, 3 * RWKV_WIDTH:])[0],
        "c2_t": _col_tile(jnp.stack([w0[0], a0[0]])),
        "w2t": w2[0].T.astype(bf16), "a2t": a2[0].T.astype(bf16),
        "rwkv_params": jnp.stack([mu[:RWKV_WIDTH], mu[RWKV_WIDTH:2 * RWKV_WIDTH],
                                  mu[2 * RWKV_WIDTH:3 * RWKV_WIDTH], k_k[0], k_a[0], r_k[0], ln_w[0], ln_b[0]]),
        "sink_t": _col_tile(sinks)[0],
        "w_out": w_out[0][out_rows].astype(bf16), "g_ple": g_ple[0][None, :],
        "w_pg": w_ple_gate[0].astype(bf16), "w_pp": w_ple_proj[0].astype(bf16),
        "g_final": g_final[None, :],
    }
    Bp, Tp, _ = x_prompt.shape
    Bs, Ts, _ = x_sample.shape
    assert Bp * RWKV_HEADS == LANES and Bs == LANES and Tp % WINDOW == 0 and Ts % SUBLANES == 0

    yp, s1, sh1, k1, v1 = _layer(x_prompt, p_prompt[0], jnp.arange(Tp), None, None, None, None, wts,
                                 heads_on_lanes=True, tt=32, tc=64, period=32, tm=1024, att_bb=1, att_tq=8 * WINDOW)
    ys, s2, sh2, k2, v2 = _layer(x_sample, p_sample[0], PAST_LEN + jnp.arange(Ts),
                                 state_rwkv_wkv[0], state_rwkv_shift[0], cache_swa_k[0], cache_swa_v[0],
                                 wts, heads_on_lanes=False, tt=2, tc=Ts, period=Ts, tm=Bs * Ts, att_bb=8, att_tq=Ts)
    return (yp, ys, s1[None], sh1[None], k1[None], v1[None], s2[None], sh2[None], k2[None], v2[None])
```

```python
import functools
import math

import jax
import jax.numpy as jnp
from jax import lax
from jax.experimental import pallas as pl
from jax.experimental.pallas import tpu as pltpu

D_MODEL = 1024
HEAD_DIM = 64
RWKV_WIDTH = 512
RWKV_HEADS = 8
ATTN_WIDTH = 512
ATTN_Q_HEADS = 8
ATTN_KV_HEADS = 2
ATTN_GROUP = 4
KV_WIDTH = 128
LORA = 64
WINDOW = 128
ROPE_THETA = 500000.0
ROPE_DIM = 16
PLE_DIM = 256
NORM_EPS = 1e-6
GN_EPS = 64e-5
NEG_INF = -1e30
PAST_LEN = 16384
SHIFT_DIM = 3 * RWKV_WIDTH + 2 * LORA
NAT_DIM = RWKV_WIDTH + ATTN_WIDTH + 2 * KV_WIDTH + ATTN_WIDTH

LANES = 128
SUBLANES = 8
VMEM_LIMIT = 56 * 1024 * 1024
DECAY_SCALE = math.exp(-0.5)
GROUP_T = 8

Q_KK, Q_W, Q_B, Q_K, Q_R, Q_V = range(6)
NQ = 6
P_MU_R, P_MU_K, P_MU_V, P_KK, P_KA, P_RK, P_LNW, P_LNB = range(8)
NP = 8

f32 = jnp.float32
bf16 = jnp.bfloat16


def _cparams(n_axes):
    return pltpu.CompilerParams(dimension_semantics=("arbitrary",) * n_axes,
                                vmem_limit_bytes=VMEM_LIMIT)


def _rmsnorm(x, g):
    ms = jnp.mean(x * x, axis=-1, keepdims=True)
    return x * lax.rsqrt(ms + NORM_EPS) * g


def _sigmoid(x):
    return 1.0 / (1.0 + jnp.exp(-x))


def _dot_nt(a, b):
    return lax.dot_general(a, b, (((1,), (1,)), ((), ())), preferred_element_type=f32)


def _dot(a, b):
    return jnp.dot(a, b, preferred_element_type=f32)


def _chunk_transpose(xs, chunk):
    lane = lax.broadcasted_iota(jnp.int32, xs[0].shape, 1)
    xs = list(xs)
    for d in (4, 2, 1):
        hi_lanes = (lane & (chunk * d)) != 0
        nxt = list(xs)
        for i in range(8):
            if i & d:
                continue
            lo, hi = xs[i], xs[i + d]
            if 2 * chunk * d == LANES:
                moved = pltpu.roll(jnp.where(hi_lanes, lo, hi), chunk * d, 1)
                nxt[i] = jnp.where(hi_lanes, moved, lo)
                nxt[i + d] = jnp.where(hi_lanes, hi, moved)
            else:
                nxt[i] = jnp.where(hi_lanes, pltpu.roll(hi, chunk * d, 1), lo)
                nxt[i + d] = jnp.where(hi_lanes, hi, pltpu.roll(lo, LANES - chunk * d, 1))
        xs = nxt
    return xs


def _rwkv_proj_kernel(x_ref, g_ref, wt_ref, mul_ref, c2_ref, w2t_ref, a2t_ref, shift0_ref,
                      pt_ref, out_ref, shift_ref, wend_ref,
                      carry_ref, prev_ref, wc_ref, za_ref, zb_ref, *rest, tt, nb, heads_on_lanes, period):
    i = pl.program_id(0)
    groups = 1 if heads_on_lanes else RWKV_HEADS
    lora0 = 3 * RWKV_WIDTH
    n = tt * nb
    first = i == 0

    def carry0():
        return shift0_ref[lora0:, :]

    def prev0(g, q):
        if heads_on_lanes:
            return jnp.zeros((HEAD_DIM, LANES), f32)
        r0 = q * RWKV_WIDTH + g * HEAD_DIM
        return shift0_ref[r0:r0 + HEAD_DIM, :]

    @pl.when(first)
    def _():
        zb_ref[...] = jnp.zeros(zb_ref.shape, f32)
        carry_ref[...] = carry0()
        wc_ref[...] = jnp.ones(wc_ref.shape, f32)
        for g in range(groups):
            for q in range(3):
                prev_ref[g, q] = prev0(g, q)

    def project(z_ref):
        if heads_on_lanes:
            u_ref = rest[0]
            n_col = D_MODEL // LANES
            for bi in range(nb):
                ub = _rmsnorm(x_ref[bi], g_ref[...])
                for ci in range(n_col):
                    u_ref[ci, pl.ds(bi, tt, stride=nb), :] = ub[:, ci * LANES:(ci + 1) * LANES]
            u = jnp.concatenate([u_ref[ci] for ci in range(n_col)], axis=1).astype(bf16)
        else:
            u = _rmsnorm(x_ref[...].reshape(n, D_MODEL), g_ref[...]).astype(bf16)
        z_ref[...] = _dot_nt(wt_ref[...], u)

    def post(z_ref):
        tile = i - 1
        lane = lax.broadcasted_iota(jnp.int32, (2 * LORA, LANES), 1)
        ones = jnp.ones((HEAD_DIM, LANES), f32)
        prev = [[prev_ref[g, q] for q in range(3)] for g in range(groups)]
        wc = [wc_ref[g] for g in range(groups)]
        prev_rot = carry_ref[...]
        steps_per_blk = LANES // nb if heads_on_lanes else 1
        for j in range(n // LANES):
            z = z_ref[:, j * LANES:(j + 1) * LANES]
            zl = z[lora0:]
            if nb % LANES == 0:
                prev_l, zl_rot = prev_rot, zl
            else:
                zl_rot = pltpu.roll(zl, nb, 1)
                prev_l = jnp.where(lane < nb, prev_rot, zl_rot)
            prev_rot = zl_rot
            zls = zl + mul_ref[...] * (prev_l - zl)
            w_pre = c2_ref[0] + _dot(w2t_ref[...], jnp.tanh(zls[:LORA]).astype(bf16))
            decay = jnp.exp(-DECAY_SCALE * _sigmoid(w_pre))
            a_all = _sigmoid(c2_ref[1] + _dot(a2t_ref[...], zls[LORA:].astype(bf16)))
            raw = [z[0:RWKV_WIDTH], z[RWKV_WIDTH:2 * RWKV_WIDTH], z[2 * RWKV_WIDTH:lora0], decay, a_all]
            per_head = [[val[h * HEAD_DIM:(h + 1) * HEAD_DIM] for h in range(RWKV_HEADS)] for val in raw]
            if heads_on_lanes:
                tiles = [_chunk_transpose(ph, nb) for ph in per_head]
            for ls in range(steps_per_blk):
                local = j * steps_per_blk + ls
                for g in range(groups):
                    idx = ls if heads_on_lanes else g
                    r_raw, kx_raw, v_raw, w_t, a_t = (tiles[q][idx] if heads_on_lanes else per_head[q][idx]
                                                      for q in range(5))
                    r = r_raw + pt_ref[g, P_MU_R] * (prev[g][0] - r_raw)
                    kx = kx_raw + pt_ref[g, P_MU_K] * (prev[g][1] - kx_raw)
                    v = v_raw + pt_ref[g, P_MU_V] * (prev[g][2] - v_raw)
                    prev[g] = [r_raw, kx_raw, v_raw]
                    kkr = kx * pt_ref[g, P_KK]
                    ss = jnp.sum(kkr * kkr, axis=0, keepdims=True)
                    kk = kkr * (1.0 / jnp.maximum(jnp.sqrt(ss), 1e-12))
                    k = kx * (1.0 + (a_t - 1.0) * pt_ref[g, P_KA])
                    b = kk * a_t
                    wc_in = wc[g]
                    if tt >= period:
                        wc_base = ones if local % period == 0 else wc_in
                    elif local == 0:
                        wc_base = jnp.where(tile % (period // tt) == 0, ones, wc_in)
                    else:
                        wc_base = wc_in
                    wc_t = wc_base * w_t
                    inv_wc = 1.0 / wc_t
                    step = (g, local)
                    out_ref[step + (Q_KK,)] = kk * wc_base
                    out_ref[step + (Q_W,)] = wc_in
                    out_ref[step + (Q_B,)] = b * inv_wc
                    out_ref[step + (Q_K,)] = k * inv_wc
                    out_ref[step + (Q_R,)] = r * wc_t
                    out_ref[step + (Q_V,)] = v
                    wc[g] = wc_t
        carry_ref[...] = jnp.where(first, carry0(), prev_rot)
        shift_ref[...] = z
        for g in range(groups):
            wc_g = jnp.where(first, ones, wc[g])
            wc_ref[g] = wc_g
            wend_ref[g] = wc_g
            for q in range(3):
                prev_ref[g, q] = jnp.where(first, prev0(g, q), prev[g][q])

    @pl.when(i % 2 == 0)
    def _():
        project(za_ref)
        post(zb_ref)

    @pl.when(i % 2 == 1)
    def _():
        project(zb_ref)
        post(za_ref)


def _rwkv_proj(x, g_norm, wt, mul_t, c2_t, w2t, a2t, shift0_t, p_tiles, *, tt, heads_on_lanes, period):
    const = lambda shape: pl.BlockSpec(shape, lambda i: (0,) * len(shape))
    groups = 1 if heads_on_lanes else RWKV_HEADS
    if heads_on_lanes:
        nb, T, _ = x.shape
    else:
        T, nb, _ = x.shape
    n_tiles = T // tt
    proj_tile = lambda i: jnp.minimum(i, n_tiles - 1)
    post_tile = lambda i: jnp.maximum(i - 1, 0)
    scratch = [pltpu.VMEM((2 * LORA, LANES), f32), pltpu.VMEM((groups, 3, HEAD_DIM, LANES), f32),
               pltpu.VMEM((groups, HEAD_DIM, LANES), f32),
               pltpu.VMEM((SHIFT_DIM, tt * nb), f32), pltpu.VMEM((SHIFT_DIM, tt * nb), f32)]
    if heads_on_lanes:
        x_spec = pl.BlockSpec((nb, tt, D_MODEL), lambda i: (0, proj_tile(i), 0))
        scratch.append(pltpu.VMEM((D_MODEL // LANES, tt * nb, LANES), f32))
    else:
        x_spec = pl.BlockSpec((tt, nb, D_MODEL), lambda i: (proj_tile(i), 0, 0))
    assert period % tt == 0 or tt % period == 0
    kern = functools.partial(_rwkv_proj_kernel, tt=tt, nb=nb, heads_on_lanes=heads_on_lanes, period=period)
    return pl.pallas_call(
        kern,
        grid=(n_tiles + 1,),
        in_specs=[
            x_spec,
            const((1, D_MODEL)),
            const((SHIFT_DIM, D_MODEL)),
            const((2 * LORA, LANES)),
            const((2, RWKV_WIDTH, LANES)),
            const((RWKV_WIDTH, LORA)), const((RWKV_WIDTH, LORA)),
            const((SHIFT_DIM, LANES)),
            const((groups, NP, HEAD_DIM, LANES)),
        ],
        out_specs=[pl.BlockSpec((groups, tt, NQ, HEAD_DIM, LANES), lambda i: (0, post_tile(i), 0, 0, 0)),
                   const((SHIFT_DIM, LANES)), const((groups, HEAD_DIM, LANES))],
        out_shape=[jax.ShapeDtypeStruct((groups, T, NQ, HEAD_DIM, LANES), f32),
                   jax.ShapeDtypeStruct((SHIFT_DIM, LANES), f32),
                   jax.ShapeDtypeStruct((groups, HEAD_DIM, LANES), f32)],
        scratch_shapes=scratch,
        compiler_params=_cparams(1),
        name="rwkv_proj",
    )(x, g_norm, wt, mul_t, c2_t, w2t, a2t, shift0_t, p_tiles)


def _rope(x, c, a, b):
    return x * c + pltpu.roll(x, LANES - ROPE_DIM // 2, 1) * a + pltpu.roll(x, ROPE_DIM // 2, 1) * b


def _nat_proj_kernel(x_ref, g_ref, w_ref, rope_ref, sgr_ref, q_ref, k_ref, v_ref, sga_ref):
    u = _rmsnorm(x_ref[...], g_ref[...]).astype(bf16)
    z = _dot(u, w_ref[...])
    o_q = RWKV_WIDTH
    o_k = o_q + ATTN_WIDTH
    o_v = o_k + KV_WIDTH
    o_g = o_v + KV_WIDTH
    gr = z[:, :o_q]
    sgr_ref[...] = gr * _sigmoid(gr)
    rc, ra, rb = rope_ref[0], rope_ref[1], rope_ref[2]
    for j in range(ATTN_WIDTH // LANES):
        qj = z[:, o_q + j * LANES:o_q + (j + 1) * LANES]
        q_ref[:, j * LANES:(j + 1) * LANES] = _rope(qj, rc, ra, rb) * (HEAD_DIM ** -0.5)
    k_ref[...] = _rope(z[:, o_k:o_v], rc, ra, rb)
    v_ref[...] = z[:, o_v:o_g]
    ga = z[:, o_g:]
    sga_ref[...] = ga * _sigmoid(ga)


def _nat_proj(x2d, g_norm, w_nat, rope, *, tm):
    n_tok = x2d.shape[0]
    n_tab = rope.shape[1] // tm
    const = lambda shape: pl.BlockSpec(shape, lambda i: (0,) * len(shape))
    row = lambda w: pl.BlockSpec((tm, w), lambda i: (i, 0))
    tab = pl.BlockSpec((3, tm, LANES), lambda i: (0, i % n_tab, 0))
    return pl.pallas_call(
        _nat_proj_kernel,
        grid=(n_tok // tm,),
        in_specs=[row(D_MODEL), const((1, D_MODEL)), const((D_MODEL, NAT_DIM)), tab],
        out_specs=[row(RWKV_WIDTH), row(ATTN_WIDTH), row(KV_WIDTH), row(KV_WIDTH), row(ATTN_WIDTH)],
        out_shape=[jax.ShapeDtypeStruct((n_tok, w), f32)
                   for w in (RWKV_WIDTH, ATTN_WIDTH, KV_WIDTH, KV_WIDTH, ATTN_WIDTH)],
        compiler_params=_cparams(1),
        name="nat_proj",
    )(x2d, g_norm, w_nat, rope)


def _row_bcast(ref, idx, k):
    return jnp.broadcast_to(ref[idx + (pl.ds(k, 1), slice(None))], (HEAD_DIM, LANES))


def _wkv_step(s_ref, ref, at, at_next, sa, rk, lnw, lnb):
    vv = ref[at + (Q_V,)]
    y = jnp.zeros((HEAD_DIM, LANES), f32)
    sa_next = jnp.zeros((HEAD_DIM, LANES), f32)
    for k in range(HEAD_DIM):
        s_new = (s_ref[k] - sa * _row_bcast(ref, at + (Q_B,), k)
                 + vv * _row_bcast(ref, at + (Q_K,), k))
        s_ref[k] = s_new
        y = y + s_new * _row_bcast(ref, at + (Q_R,), k)
        sa_next = sa_next + s_new * _row_bcast(ref, at_next + (Q_KK,), k)
    mean = jnp.mean(y, axis=0, keepdims=True)
    d = y - mean
    var = jnp.mean(d * d, axis=0, keepdims=True)
    yn = d * lax.rsqrt(var + GN_EPS) * lnw + lnb
    rkk = jnp.sum(ref[at + (Q_R,)] * ref[at + (Q_K,)] * rk, axis=0, keepdims=True)
    return yn + rkk * vv, sa_next


def _wkv_scan_kernel(in_ref, s0_ref, wend_ref, pt_ref, o_ref, sout_ref, s_ref, *rest,
                     tc, period, natural_out):
    c = pl.program_id(1)

    @pl.when(c == 0)
    def _():
        s_ref[...] = s0_ref[0]

    obuf_ref = rest[0] if natural_out else None
    rk, lnw, lnb = pt_ref[0, P_RK], pt_ref[0, P_LNW], pt_ref[0, P_LNB]

    for t0 in range(0, tc, period):
        sa0 = jnp.zeros((HEAD_DIM, LANES), f32)
        for k in range(HEAD_DIM):
            s_k = s_ref[k] * _row_bcast(in_ref, (0, t0, Q_W), k)
            s_ref[k] = s_k
            sa0 = sa0 + s_k * _row_bcast(in_ref, (0, t0, Q_KK), k)

        def step(t, sa, last=t0 + period - 1):
            t_next = jnp.minimum(t + 1, last)
            o, sa_next = _wkv_step(s_ref, in_ref, (0, t), (0, t_next), sa, rk, lnw, lnb)
            if natural_out:
                obuf_ref[t] = o
            else:
                o_ref[0, t] = o
            return sa_next

        lax.fori_loop(t0, t0 + period, step, sa0)

    if natural_out:
        nb = LANES // RWKV_HEADS
        half = LANES // 2
        low = lax.broadcasted_iota(jnp.int32, (nb, LANES), 1) < half
        for t in range(0, tc, 2):
            both = jnp.concatenate([obuf_ref[t], obuf_ref[t + 1]], axis=0)
            both_t = both.T
            for pair in range(RWKV_HEADS // 2):
                h0 = both_t[(2 * pair) * nb:(2 * pair + 1) * nb]
                h1 = both_t[(2 * pair + 1) * nb:(2 * pair + 2) * nb]
                cs = slice(pair * LANES, (pair + 1) * LANES)
                o_ref[t, :, cs] = jnp.where(low, h0, pltpu.roll(h1, half, 1))
                o_ref[t + 1, :, cs] = jnp.where(low, pltpu.roll(h0, half, 1), h1)

    @pl.when(c == pl.num_programs(1) - 1)
    def _():
        for k in range(HEAD_DIM):
            sout_ref[0, k] = s_ref[k] * _row_bcast(wend_ref, (0,), k)


def _wkv_scan(scan_in, s0, wend, p_tiles, *, tc, period, natural_out):
    groups, T = scan_in.shape[:2]
    assert tc % period == 0
    tile = pl.BlockSpec((1, HEAD_DIM, LANES), lambda g, c: (g, 0, 0))
    state = pl.BlockSpec((1, HEAD_DIM, HEAD_DIM, LANES), lambda g, c: (g, 0, 0, 0))
    scratch = [pltpu.VMEM((HEAD_DIM, HEAD_DIM, LANES), f32)]
    if natural_out:
        nb = LANES // RWKV_HEADS
        o_spec = pl.BlockSpec((tc, nb, RWKV_WIDTH), lambda g, c: (c, 0, 0))
        o_shape = jax.ShapeDtypeStruct((T, nb, RWKV_WIDTH), f32)
        scratch.append(pltpu.VMEM((tc, HEAD_DIM, LANES), f32))
    else:
        o_spec = pl.BlockSpec((1, tc, HEAD_DIM, LANES), lambda g, c: (g, c, 0, 0))
        o_shape = jax.ShapeDtypeStruct((groups, T, HEAD_DIM, LANES), f32)
    return pl.pallas_call(
        functools.partial(_wkv_scan_kernel, tc=tc, period=period, natural_out=natural_out),
        grid=(groups, T // tc),
        in_specs=[
            pl.BlockSpec((1, tc, NQ, HEAD_DIM, LANES), lambda g, c: (g, c, 0, 0, 0)),
            state, tile, pl.BlockSpec((1, NP, HEAD_DIM, LANES), lambda g, c: (g, 0, 0, 0)),
        ],
        out_specs=[o_spec, state],
        out_shape=[o_shape, jax.ShapeDtypeStruct((groups, HEAD_DIM, HEAD_DIM, LANES), f32)],
        scratch_shapes=scratch,
        compiler_params=_cparams(2),
        name="wkv_scan",
    )(scan_in, s0, wend, p_tiles)


def _swa_attn_kernel(q_ref, kp_ref, kc_ref, vp_ref, vc_ref, sga_ref, sink_ref, o_ref, *,
                     bb, tq, first_block_has_no_prev):
    n = pl.program_id(1)
    nr = bb * tq
    half = LANES // 2
    tq_bits = tq.bit_length() - 1

    def key_mask(n_keys_per_batch, is_prev):
        rows = lax.broadcasted_iota(jnp.int32, (nr, bb * n_keys_per_batch), 0)
        cols = lax.broadcasted_iota(jnp.int32, (nr, bb * n_keys_per_batch), 1)
        i = rows & (tq - 1)
        j = cols & (n_keys_per_batch - 1)
        ok = (j > i) if is_prev else (j <= i)
        if bb > 1:
            same = (rows >> tq_bits) == (cols >> (n_keys_per_batch.bit_length() - 1))
            ok = jnp.logical_and(same, ok)
        if is_prev and first_block_has_no_prev:
            ok = jnp.logical_and(ok, n > 0)
        return jnp.tile(ok, (ATTN_GROUP, 1))

    mask_p = key_mask(WINDOW, True)
    mask_c = key_mask(tq, False)
    lane_q = lax.broadcasted_iota(jnp.int32, (nr, LANES), 1)
    lane_kp = lax.broadcasted_iota(jnp.int32, (bb * WINDOW, LANES), 1)

    kp = kp_ref[...].reshape(bb * WINDOW, LANES).astype(bf16)
    kc = kc_ref[...].reshape(nr, LANES).astype(bf16)
    vp = vp_ref[...].reshape(bb * WINDOW, LANES)
    vc = vc_ref[...].reshape(nr, LANES)

    res = []
    for g in range(ATTN_KV_HEADS):
        in_g = (lambda lane: lane < half) if g == 0 else (lambda lane: lane >= half)
        qs, sinks = [], []
        for hh in range(ATTN_GROUP):
            x = q_ref[:, :, hh * LANES:(hh + 1) * LANES].reshape(nr, LANES)
            qs.append(jnp.where(in_g(lane_q), x, 0.0).astype(bf16))
            sinks.append(jnp.broadcast_to(sink_ref[g * ATTN_GROUP + hh:g * ATTN_GROUP + hh + 1, :], (nr, LANES)))
        qg = jnp.concatenate(qs, axis=0)
        sink = jnp.concatenate(sinks, axis=0)[:, 0:1]
        sp = jnp.where(mask_p, _dot_nt(qg, kp), NEG_INF)
        sc = jnp.where(mask_c, _dot_nt(qg, kc), NEG_INF)
        m = jnp.maximum(jnp.maximum(jnp.max(sp, axis=-1, keepdims=True),
                                    jnp.max(sc, axis=-1, keepdims=True)), sink)
        pp = jnp.exp(sp - m).astype(bf16)
        pc = jnp.exp(sc - m).astype(bf16)
        e_sink = jnp.exp(sink - m)
        vpg = jnp.where(in_g(lane_kp), vp, 1.0).astype(bf16)
        vcg = jnp.where(in_g(lane_q), vc, 1.0).astype(bf16)
        pv = _dot(pp, vpg) + _dot(pc, vcg)
        res.append(pv * (1.0 / (pltpu.roll(pv, half, 1) + e_sink)))
    for hh in range(ATTN_GROUP):
        blk = slice(hh * nr, (hh + 1) * nr)
        cs = slice(hh * LANES, (hh + 1) * LANES)
        out = jnp.where(lane_q < half, res[0][blk], res[1][blk])
        o_ref[:, :, cs] = (out.reshape(bb, tq, LANES) * sga_ref[:, :, cs]).astype(o_ref.dtype)


def _attn_block_t(q_cols, k_all, v_all, ok, sink_ref):
    tq = WINDOW
    half = LANES // 2
    lane_q = lax.broadcasted_iota(jnp.int32, (tq, LANES), 1)
    lane_k = lax.broadcasted_iota(jnp.int32, (2 * WINDOW, LANES), 1)
    mask_t = jnp.tile(ok, (1, ATTN_GROUP))
    k_bf = k_all.astype(bf16)
    norm_t = []
    for g in range(ATTN_KV_HEADS):
        in_g = (lambda lane: lane < half) if g == 0 else (lambda lane: lane >= half)
        qs = [jnp.where(in_g(lane_q), x, jnp.zeros_like(x)).astype(bf16) for x in q_cols]
        sinks = [sink_ref[g * ATTN_GROUP + hh:g * ATTN_GROUP + hh + 1, :] for hh in range(ATTN_GROUP)]
        qg = jnp.concatenate(qs, axis=0)
        sink = jnp.concatenate(sinks, axis=1)
        st = jnp.where(mask_t, _dot_nt(k_bf, qg), NEG_INF)
        m = jnp.maximum(jnp.max(st, axis=0, keepdims=True), sink)
        p = jnp.exp(st - m).astype(bf16)
        e_sink = jnp.exp(sink - m)
        vg = jnp.where(in_g(lane_k), v_all, 1.0).astype(bf16)
        ot = lax.dot_general(vg, p, (((0,), (0,)), ((), ())), preferred_element_type=f32)
        lo, hi = ot[:half], ot[half:]
        num, den = (lo, hi) if g == 0 else (hi, lo)
        norm_t.append(num * (1.0 / (den + e_sink)))
    return [jnp.concatenate([norm_t[0][:, hh * tq:(hh + 1) * tq],
                             norm_t[1][:, hh * tq:(hh + 1) * tq]], axis=0).T for hh in range(ATTN_GROUP)]


def _nat_attn_kernel(x_ref, g_ref, w_ref, rope_ref, sink_ref,
                     sgr_ref, ma_ref, klast_ref, vlast_ref, kprev_ref, vprev_ref, *bufs, n_blk, tiles_per_seq):
    s = pl.program_id(0)
    tq = WINDOW
    buf_a, buf_b = bufs[:4], bufs[4:]

    @pl.when(s == 0)
    def _():
        kprev_ref[...] = jnp.zeros(kprev_ref.shape, f32)
        vprev_ref[...] = jnp.zeros(vprev_ref.shape, f32)
        for ref in buf_b:
            ref[...] = jnp.zeros(ref.shape, ref.dtype)

    def project(q_ref, k_ref, v_ref, sg_ref):
        u = _rmsnorm(x_ref[0], g_ref[...]).astype(bf16)
        z = _dot(u, w_ref[...])
        o_q = RWKV_WIDTH
        o_k = o_q + ATTN_WIDTH
        o_v = o_k + KV_WIDTH
        o_g = o_v + KV_WIDTH
        gr = z[:, :o_q]
        sgr_ref[0] = gr * _sigmoid(gr)
        rc, ra, rb = rope_ref[0], rope_ref[1], rope_ref[2]
        for j in range(ATTN_WIDTH // LANES):
            qj = z[:, o_q + j * LANES:o_q + (j + 1) * LANES]
            q_ref[j] = (_rope(qj, rc, ra, rb) * (HEAD_DIM ** -0.5)).astype(bf16)
        k = _rope(z[:, o_k:o_v], rc, ra, rb)
        v = z[:, o_v:o_g]
        k_ref[...] = k
        v_ref[...] = v
        ga = z[:, o_g:]
        sg_ref[...] = ga * _sigmoid(ga)
        last = slice((n_blk - 1) * tq, n_blk * tq)
        klast_ref[0] = k[last]
        vlast_ref[0] = v[last]

    def attend(q_ref, k_ref, v_ref, sg_ref):
        seq_start = (s - 1) % tiles_per_seq == 0
        keys = lax.broadcasted_iota(jnp.int32, (2 * WINDOW, tq), 0)
        qi = lax.broadcasted_iota(jnp.int32, (2 * WINDOW, tq), 1)
        prev_ok = jnp.logical_and(keys < WINDOW, keys > qi)
        cur_ok = jnp.logical_and(keys >= WINDOW, keys - WINDOW <= qi)
        for blk in range(n_blk):
            rows = slice(blk * tq, (blk + 1) * tq)
            if blk == 0:
                kp, vp = kprev_ref[...], vprev_ref[...]
                ok = jnp.logical_or(jnp.logical_and(prev_ok, jnp.logical_not(seq_start)), cur_ok)
            else:
                before = slice((blk - 1) * tq, blk * tq)
                kp, vp = k_ref[before, :], v_ref[before, :]
                ok = jnp.logical_or(prev_ok, cur_ok)
            outs = _attn_block_t([q_ref[j, rows, :] for j in range(ATTN_GROUP)],
                                 jnp.concatenate([kp, k_ref[rows, :]], axis=0),
                                 jnp.concatenate([vp, v_ref[rows, :]], axis=0), ok, sink_ref)
            for hh in range(ATTN_GROUP):
                cs = slice(hh * LANES, (hh + 1) * LANES)
                ma_ref[0, rows, cs] = (outs[hh] * sg_ref[rows, cs]).astype(ma_ref.dtype)
        last = slice((n_blk - 1) * tq, n_blk * tq)
        kprev_ref[...] = k_ref[last, :]
        vprev_ref[...] = v_ref[last, :]

    @pl.when(s % 2 == 0)
    def _():
        project(*buf_a)
        attend(*buf_b)

    @pl.when(s % 2 == 1)
    def _():
        project(*buf_b)
        attend(*buf_a)


def _nat_attn(x, g_norm, w_nat, rope, sink_t, *, n_blk):
    B, T, _ = x.shape
    tm = n_blk * WINDOW
    tps = T // tm
    n_tiles = B * tps
    proj_tile = lambda s: jnp.minimum(s, n_tiles - 1)
    attn_tile = lambda s: jnp.maximum(s - 1, 0)
    const = lambda shape: pl.BlockSpec(shape, lambda s: (0,) * len(shape))
    tab = pl.BlockSpec((3, tm, LANES), lambda s: (0, proj_tile(s) % tps, 0))
    last = pl.BlockSpec((1, WINDOW, KV_WIDTH), lambda s: (proj_tile(s) // tps, 0, 0))
    buf = [pltpu.VMEM((ATTN_WIDTH // LANES, tm, LANES), bf16), pltpu.VMEM((tm, KV_WIDTH), f32),
           pltpu.VMEM((tm, KV_WIDTH), f32), pltpu.VMEM((tm, ATTN_WIDTH), f32)]
    sgr, ma, k_last, v_last = pl.pallas_call(
        functools.partial(_nat_attn_kernel, n_blk=n_blk, tiles_per_seq=tps),
        grid=(n_tiles + 1,),
        in_specs=[pl.BlockSpec((1, tm, D_MODEL), lambda s: (proj_tile(s), 0, 0)),
                  const((1, D_MODEL)), const((D_MODEL, NAT_DIM)), tab,
                  const((ATTN_Q_HEADS, LANES))],
        out_specs=[pl.BlockSpec((1, tm, RWKV_WIDTH), lambda s: (proj_tile(s), 0, 0)),
                   pl.BlockSpec((1, tm, ATTN_WIDTH), lambda s: (attn_tile(s), 0, 0)), last, last],
        out_shape=[jax.ShapeDtypeStruct((n_tiles, tm, RWKV_WIDTH), f32),
                   jax.ShapeDtypeStruct((n_tiles, tm, ATTN_WIDTH), bf16),
                   jax.ShapeDtypeStruct((B, WINDOW, KV_WIDTH), f32),
                   jax.ShapeDtypeStruct((B, WINDOW, KV_WIDTH), f32)],
        scratch_shapes=[pltpu.VMEM((WINDOW, KV_WIDTH), f32), pltpu.VMEM((WINDOW, KV_WIDTH), f32)] + buf + buf,
        compiler_params=_cparams(1),
        name="nat_attn",
    )(x.reshape(n_tiles, tm, D_MODEL), g_norm, w_nat, rope, sink_t)
    return sgr.reshape(B, T, RWKV_WIDTH), ma.reshape(B, T, ATTN_WIDTH), k_last, v_last


def _swa_attn(q, k_prev, k_cur, v_prev, v_cur, sga, sink_t, *, bb, tq):
    B, T, _ = q.shape
    cur = lambda w: pl.BlockSpec((bb, tq, w), lambda bi, n: (bi, n, 0))
    prev = pl.BlockSpec((bb, WINDOW, KV_WIDTH), lambda bi, n: (bi, 0, 0))
    return pl.pallas_call(
        functools.partial(_swa_attn_kernel, bb=bb, tq=tq, first_block_has_no_prev=False),
        grid=(B // bb, T // tq),
        in_specs=[cur(ATTN_WIDTH), prev, cur(KV_WIDTH), prev, cur(KV_WIDTH), cur(ATTN_WIDTH),
                  pl.BlockSpec((ATTN_Q_HEADS, LANES), lambda bi, n: (0, 0))],
        out_specs=cur(ATTN_WIDTH),
        out_shape=jax.ShapeDtypeStruct((B, T, ATTN_WIDTH), f32),
        compiler_params=_cparams(2),
        name="swa_attn",
    )(q, k_prev, k_cur, v_prev, v_cur, sga, sink_t)


def _out_mix_kernel(x_ref, or_ref, sgr_ref, ma_ref, p_ref, wo_ref, gp_ref, wpg_ref, wpp_ref, gf_ref,
                    y_ref):
    mr = (or_ref[...] * sgr_ref[...]).astype(bf16)
    ma = ma_ref[...].astype(bf16)
    h = (x_ref[...] + _dot(mr, wo_ref[:RWKV_WIDTH, :]) + _dot(ma, wo_ref[RWKV_WIDTH:, :]))
    gate = _sigmoid(_dot(_rmsnorm(h, gp_ref[...]).astype(bf16), wpg_ref[...]))
    h = h + gate * _dot(p_ref[...].astype(bf16), wpp_ref[...])
    y_ref[...] = _rmsnorm(h, gf_ref[...])


def _out_mix(x2d, o_r, sgr, ma, p2d, w_out, g_ple, w_pg, w_pp, g_final, *, tm):
    n_tok = x2d.shape[0]
    const = lambda shape: pl.BlockSpec(shape, lambda i: (0,) * len(shape))
    row = lambda w: pl.BlockSpec((tm, w), lambda i: (i, 0))
    return pl.pallas_call(
        _out_mix_kernel,
        grid=(n_tok // tm,),
        in_specs=[row(D_MODEL), row(RWKV_WIDTH), row(RWKV_WIDTH), row(ATTN_WIDTH), row(PLE_DIM),
                  const((D_MODEL, D_MODEL)), const((1, D_MODEL)), const((D_MODEL, D_MODEL)),
                  const((PLE_DIM, D_MODEL)), const((1, D_MODEL))],
        out_specs=row(D_MODEL),
        out_shape=jax.ShapeDtypeStruct((n_tok, D_MODEL), f32),
        compiler_params=_cparams(1),
        name="out_mix",
    )(x2d, o_r, sgr, ma, p2d, w_out, g_ple, w_pg, w_pp, g_final)


def _rope_tables(pos):
    half = ROPE_DIM // 2
    inv = ROPE_THETA ** (-jnp.arange(half, dtype=f32) / half)
    ang = pos.astype(f32)[:, None] * inv[None, :]
    cos, sin = jnp.cos(ang), jnp.sin(ang)
    n = pos.shape[0]
    ones = jnp.ones((n, HEAD_DIM - ROPE_DIM), f32)
    zeros = jnp.zeros((n, HEAD_DIM - ROPE_DIM), f32)
    zh = jnp.zeros((n, half), f32)
    per_head = jnp.stack([jnp.concatenate([cos, cos, ones], axis=1),
                          jnp.concatenate([-sin, zh, zeros], axis=1),
                          jnp.concatenate([zh, sin, zeros], axis=1)])
    return jnp.tile(per_head, (1, 1, LANES // HEAD_DIM))


def _col_tile(vecs):
    return jnp.broadcast_to(vecs[:, :, None], vecs.shape + (LANES,))


def _lane_param_tiles(vecs, heads_on_lanes):
    ph = vecs.reshape(vecs.shape[0], RWKV_HEADS, HEAD_DIM)
    if heads_on_lanes:
        nb = LANES // RWKV_HEADS
        return jnp.repeat(jnp.swapaxes(ph, 1, 2), nb, axis=2)[None]
    return jnp.broadcast_to(jnp.swapaxes(ph, 0, 1)[:, :, :, None], (RWKV_HEADS, vecs.shape[0], HEAD_DIM, LANES))


def _layer(x, p, pos, s0, shift0, kbuf, vbuf, wts, *, heads_on_lanes, tt, tc, period, tm, att_bb, att_tq):
    B, T, _ = x.shape
    n_tok = B * T
    x2d = x.reshape(n_tok, D_MODEL)

    if shift0 is None:
        shift0_t = jnp.zeros((SHIFT_DIM, LANES), f32)
    else:
        shift0_t = shift0.T
    p_tiles = _lane_param_tiles(wts["rwkv_params"], heads_on_lanes)
    proj_args = (wts["g_norm"], wts["wt_rwkv"], wts["mul_t"], wts["c2_t"], wts["w2t"], wts["a2t"], shift0_t,
                 p_tiles)
    if heads_on_lanes:
        assert s0 is None and shift0 is None
        scan_in, shift_t, wend = _rwkv_proj(x, *proj_args, tt=tt, heads_on_lanes=True, period=period)
        shift_new = shift_t[:, LANES - B:].T
        s0_t = jnp.zeros((1, HEAD_DIM, HEAD_DIM, LANES), f32)
        o_tb, s_fin = _wkv_scan(scan_in, s0_t, wend, p_tiles, tc=tc, period=period, natural_out=True)
        o_r = jnp.swapaxes(o_tb, 0, 1).reshape(n_tok, RWKV_WIDTH)
        s_new = jnp.transpose(s_fin[0].reshape(HEAD_DIM, HEAD_DIM, RWKV_HEADS, B), (3, 2, 1, 0))
    else:
        scan_in, shift_t, wend = _rwkv_proj(jnp.swapaxes(x, 0, 1), *proj_args, tt=tt, heads_on_lanes=False,
                                            period=period)
        shift_new = shift_t.T
        s0_t = jnp.transpose(s0, (1, 3, 2, 0))
        o_scan, s_fin = _wkv_scan(scan_in, s0_t, wend, p_tiles, tc=tc, period=period, natural_out=False)
        o_r = jnp.transpose(o_scan, (3, 1, 0, 2)).reshape(n_tok, RWKV_WIDTH)
        s_new = jnp.transpose(s_fin, (3, 0, 2, 1))

    rope = _rope_tables(pos)
    if kbuf is None:
        sgr3, ma, k_last, v_last = _nat_attn(x, wts["g_norm"], wts["w_nat"], rope, wts["sink_t"],
                                             n_blk=att_tq // WINDOW)
        sgr = sgr3.reshape(n_tok, RWKV_WIDTH)
        k_new = k_last.reshape(B, WINDOW, ATTN_KV_HEADS, HEAD_DIM)
        v_new = v_last.reshape(B, WINDOW, ATTN_KV_HEADS, HEAD_DIM)
    else:
        sgr, q, k, v, sga = _nat_proj(x2d, wts["g_norm"], wts["w_nat"], jnp.tile(rope, (1, tm // T, 1)), tm=tm)
        k3 = k.reshape(B, T, KV_WIDTH)
        v3 = v.reshape(B, T, KV_WIDTH)
        kb = kbuf.reshape(B, WINDOW, KV_WIDTH)
        vb = vbuf.reshape(B, WINDOW, KV_WIDTH)
        ma = _swa_attn(q.reshape(B, T, ATTN_WIDTH), kb, k3, vb, v3, sga.reshape(B, T, ATTN_WIDTH),
                       wts["sink_t"], bb=att_bb, tq=att_tq)
        k_new = jnp.concatenate([kb, k3], axis=1)[:, -WINDOW:].reshape(B, WINDOW, ATTN_KV_HEADS, HEAD_DIM)
        v_new = jnp.concatenate([vb, v3], axis=1)[:, -WINDOW:].reshape(B, WINDOW, ATTN_KV_HEADS, HEAD_DIM)

    y = _out_mix(x2d, o_r, sgr, ma.reshape(n_tok, ATTN_WIDTH), p.reshape(n_tok, PLE_DIM),
                 wts["w_out"], wts["g_ple"], wts["w_pg"], wts["w_pp"], wts["g_final"], tm=tm)
    return y.reshape(B, T, D_MODEL), s_new, shift_new, k_new, v_new


def kernel(x_prompt, x_sample, state_rwkv_wkv, state_rwkv_shift, cache_swa_k, cache_swa_v,
           p_prompt, p_sample, g_norm, w_in, mu_shift, w0, w2, a0, a2, k_k, k_a, r_k,
           ln_w, ln_b, sinks, w_out, g_ple, w_ple_gate, w_ple_proj, g_final):
    assert w_in.shape[0] == 1, "single layer"
    w_in0 = w_in[0]
    head_order = [g * ATTN_GROUP + i for i in range(ATTN_GROUP) for g in range(ATTN_KV_HEADS)]
    cols = jnp.concatenate([jnp.arange(h * HEAD_DIM, (h + 1) * HEAD_DIM) for h in head_order])
    o_q = SHIFT_DIM + RWKV_WIDTH
    o_ga = o_q + ATTN_WIDTH + 2 * KV_WIDTH
    nat_cols = jnp.concatenate([jnp.arange(SHIFT_DIM, o_q), o_q + cols, jnp.arange(o_q + ATTN_WIDTH, o_ga),
                                o_ga + cols])
    out_rows = jnp.concatenate([jnp.arange(RWKV_WIDTH), RWKV_WIDTH + cols])
    mu = mu_shift[0]
    wts = {
        "g_norm": g_norm[0][None, :],
        "wt_rwkv": w_in0[:, :SHIFT_DIM].T.astype(bf16),
        "w_nat": w_in0[:, nat_cols].astype(bf16),
        "mul_t": _col_tile(mu[None, 3 * RWKV_WIDTH:])[0],
        "c2_t": _col_tile(jnp.stack([w0[0], a0[0]])),
        "w2t": w2[0].T.astype(bf16), "a2t": a2[0].T.astype(bf16),
        "rwkv_params": jnp.stack([mu[:RWKV_WIDTH], mu[RWKV_WIDTH:2 * RWKV_WIDTH],
                                  mu[2 * RWKV_WIDTH:3 * RWKV_WIDTH], k_k[0], k_a[0], r_k[0], ln_w[0], ln_b[0]]),
        "sink_t": _col_tile(sinks)[0],
        "w_out": w_out[0][out_rows].astype(bf16), "g_ple": g_ple[0][None, :],
        "w_pg": w_ple_gate[0].astype(bf16), "w_pp": w_ple_proj[0].astype(bf16),
        "g_final": g_final[None, :],
    }
    Bp, Tp, _ = x_prompt.shape
    Bs, Ts, _ = x_sample.shape
    assert Bp * RWKV_HEADS == LANES and Bs == LANES and Tp % WINDOW == 0 and Ts % SUBLANES == 0

    yp, s1, sh1, k1, v1 = _layer(x_prompt, p_prompt[0], jnp.arange(Tp), None, None, None, None, wts,
                                 heads_on_lanes=True, tt=32, tc=64, period=32, tm=1024, att_bb=1, att_tq=8 * WINDOW)
    ys, s2, sh2, k2, v2 = _layer(x_sample, p_sample[0], PAST_LEN + jnp.arange(Ts),
                                 state_rwkv_wkv[0], state_rwkv_shift[0], cache_swa_k[0], cache_swa_v[0],
                                 wts, heads_on_lanes=False, tt=2, tc=Ts, period=Ts, tm=Bs * Ts, att_bb=8, att_tq=Ts)
    return (yp, ys, s1[None], sh1[None], k1[None], v1[None], s2[None], sh2[None], k2[None], v2[None])
```

```python
import functools
import math

import jax
import jax.numpy as jnp
from jax import lax
from jax.experimental import pallas as pl
from jax.experimental.pallas import tpu as pltpu

D_MODEL = 1024
HEAD_DIM = 64
RWKV_WIDTH = 512
RWKV_HEADS = 8
ATTN_WIDTH = 512
ATTN_Q_HEADS = 8
ATTN_KV_HEADS = 2
ATTN_GROUP = 4
KV_WIDTH = 128
LORA = 64
WINDOW = 128
ROPE_THETA = 500000.0
ROPE_DIM = 16
PLE_DIM = 256
NORM_EPS = 1e-6
GN_EPS = 64e-5
NEG_INF = -1e30
PAST_LEN = 16384
SHIFT_DIM = 3 * RWKV_WIDTH + 2 * LORA
NAT_DIM = RWKV_WIDTH + ATTN_WIDTH + 2 * KV_WIDTH + ATTN_WIDTH

LANES = 128
VMEM_LIMIT = 56 * 1024 * 1024
DECAY_SCALE = math.exp(-0.5)

PROMPT_TILES = dict(tt=32, tc=64, period=32, tm=1024, att_bb=1, att_tq=8 * WINDOW)
SAMPLE_TILES = dict(tt=2, tc=8, period=8, tm=1024, att_bb=8, att_tq=8)

Q_KK, Q_W, Q_B, Q_K, Q_R, Q_V = range(6)
NQ = 6
P_MU_R, P_MU_K, P_MU_V, P_KK, P_KA, P_RK, P_LNW, P_LNB = range(8)
NP = 8

f32 = jnp.float32
bf16 = jnp.bfloat16


def _cparams(n_axes):
    return pltpu.CompilerParams(dimension_semantics=("arbitrary",) * n_axes,
                                vmem_limit_bytes=VMEM_LIMIT)


def _rmsnorm(x, g):
    ms = jnp.mean(x * x, axis=-1, keepdims=True)
    return x * lax.rsqrt(ms + NORM_EPS) * g


def _sigmoid(x):
    return 1.0 / (1.0 + jnp.exp(-x))


def _dot_nt(a, b):
    return lax.dot_general(a, b, (((1,), (1,)), ((), ())), preferred_element_type=f32)


def _dot(a, b):
    return jnp.dot(a, b, preferred_element_type=f32)


def _chunk_transpose(xs, chunk):
    lane = lax.broadcasted_iota(jnp.int32, xs[0].shape, 1)
    xs = list(xs)
    for d in (4, 2, 1):
        hi_lanes = (lane & (chunk * d)) != 0
        nxt = list(xs)
        for i in range(8):
            if i & d:
                continue
            lo, hi = xs[i], xs[i + d]
            if 2 * chunk * d == LANES:
                moved = pltpu.roll(jnp.where(hi_lanes, lo, hi), chunk * d, 1)
                nxt[i] = jnp.where(hi_lanes, moved, lo)
                nxt[i + d] = jnp.where(hi_lanes, hi, moved)
            else:
                nxt[i] = jnp.where(hi_lanes, pltpu.roll(hi, chunk * d, 1), lo)
                nxt[i + d] = jnp.where(hi_lanes, hi, pltpu.roll(lo, LANES - chunk * d, 1))
        xs = nxt
    return xs


def _rwkv_proj_kernel(x_ref, g_ref, wt_ref, mul_ref, c2_ref, w2t_ref, a2t_ref, shift0_ref,
                      pt_ref, out_ref, shift_ref, wend_ref,
                      carry_ref, prev_ref, wc_ref, za_ref, zb_ref, *rest, tt, nb, heads_on_lanes, period):
    i = pl.program_id(0)
    groups = 1 if heads_on_lanes else RWKV_HEADS
    lora0 = 3 * RWKV_WIDTH
    n = tt * nb
    first = i == 0

    def carry0():
        return shift0_ref[lora0:, :]

    def prev0(g, q):
        if heads_on_lanes:
            return jnp.zeros((HEAD_DIM, LANES), f32)
        r0 = q * RWKV_WIDTH + g * HEAD_DIM
        return shift0_ref[r0:r0 + HEAD_DIM, :]

    @pl.when(first)
    def _():
        zb_ref[...] = jnp.zeros(zb_ref.shape, f32)
        carry_ref[...] = carry0()
        wc_ref[...] = jnp.ones(wc_ref.shape, f32)
        for g in range(groups):
            for q in range(3):
                prev_ref[g, q] = prev0(g, q)

    def project(z_ref):
        if heads_on_lanes:
            u_ref = rest[0]
            n_col = D_MODEL // LANES
            for bi in range(nb):
                ub = _rmsnorm(x_ref[bi], g_ref[...])
                for ci in range(n_col):
                    u_ref[ci, pl.ds(bi, tt, stride=nb), :] = ub[:, ci * LANES:(ci + 1) * LANES]
            u = jnp.concatenate([u_ref[ci] for ci in range(n_col)], axis=1).astype(bf16)
        else:
            u = _rmsnorm(x_ref[...].reshape(n, D_MODEL), g_ref[...]).astype(bf16)
        z_ref[...] = _dot_nt(wt_ref[...], u)

    def post(z_ref):
        tile = i - 1
        lane = lax.broadcasted_iota(jnp.int32, (2 * LORA, LANES), 1)
        ones = jnp.ones((HEAD_DIM, LANES), f32)
        prev = [[prev_ref[g, q] for q in range(3)] for g in range(groups)]
        wc = [wc_ref[g] for g in range(groups)]
        prev_rot = carry_ref[...]
        steps_per_blk = LANES // nb if heads_on_lanes else 1
        for j in range(n // LANES):
            z = z_ref[:, j * LANES:(j + 1) * LANES]
            zl = z[lora0:]
            if nb % LANES == 0:
                prev_l, zl_rot = prev_rot, zl
            else:
                zl_rot = pltpu.roll(zl, nb, 1)
                prev_l = jnp.where(lane < nb, prev_rot, zl_rot)
            prev_rot = zl_rot
            zls = zl + mul_ref[...] * (prev_l - zl)
            w_pre = c2_ref[0] + _dot(w2t_ref[...], jnp.tanh(zls[:LORA]).astype(bf16))
            decay = jnp.exp(-DECAY_SCALE * _sigmoid(w_pre))
            a_all = _sigmoid(c2_ref[1] + _dot(a2t_ref[...], zls[LORA:].astype(bf16)))
            raw = [z[0:RWKV_WIDTH], z[RWKV_WIDTH:2 * RWKV_WIDTH], z[2 * RWKV_WIDTH:lora0], decay, a_all]
            per_head = [[val[h * HEAD_DIM:(h + 1) * HEAD_DIM] for h in range(RWKV_HEADS)] for val in raw]
            if heads_on_lanes:
                tiles = [_chunk_transpose(ph, nb) for ph in per_head]
            for ls in range(steps_per_blk):
                local = j * steps_per_blk + ls
                for g in range(groups):
                    idx = ls if heads_on_lanes else g
                    r_raw, kx_raw, v_raw, w_t, a_t = (tiles[q][idx] if heads_on_lanes else per_head[q][idx]
                                                      for q in range(5))
                    r = r_raw + pt_ref[g, P_MU_R] * (prev[g][0] - r_raw)
                    kx = kx_raw + pt_ref[g, P_MU_K] * (prev[g][1] - kx_raw)
                    v = v_raw + pt_ref[g, P_MU_V] * (prev[g][2] - v_raw)
                    prev[g] = [r_raw, kx_raw, v_raw]
                    kkr = kx * pt_ref[g, P_KK]
                    ss = jnp.sum(kkr * kkr, axis=0, keepdims=True)
                    kk = kkr * (1.0 / jnp.maximum(jnp.sqrt(ss), 1e-12))
                    k = kx * (1.0 + (a_t - 1.0) * pt_ref[g, P_KA])
                    b = kk * a_t
                    wc_in = wc[g]
                    if tt >= period:
                        wc_base = ones if local % period == 0 else wc_in
                    elif local == 0:
                        wc_base = jnp.where(tile % (period // tt) == 0, ones, wc_in)
                    else:
                        wc_base = wc_in
                    wc_t = wc_base * w_t
                    inv_wc = 1.0 / wc_t
                    step = (g, local)
                    out_ref[step + (Q_KK,)] = kk * wc_base
                    out_ref[step + (Q_W,)] = wc_in
                    out_ref[step + (Q_B,)] = b * inv_wc
                    out_ref[step + (Q_K,)] = k * inv_wc
                    out_ref[step + (Q_R,)] = r * wc_t
                    out_ref[step + (Q_V,)] = v
                    wc[g] = wc_t
        carry_ref[...] = jnp.where(first, carry0(), prev_rot)
        shift_ref[...] = z
        for g in range(groups):
            wc_g = jnp.where(first, ones, wc[g])
            wc_ref[g] = wc_g
            wend_ref[g] = wc_g
            for q in range(3):
                prev_ref[g, q] = jnp.where(first, prev0(g, q), prev[g][q])

    @pl.when(i % 2 == 0)
    def _():
        project(za_ref)
        post(zb_ref)

    @pl.when(i % 2 == 1)
    def _():
        project(zb_ref)
        post(za_ref)


def _rwkv_proj(x, g_norm, wt, mul_t, c2_t, w2t, a2t, shift0_t, p_tiles, *, tt, heads_on_lanes, period):
    const = lambda shape: pl.BlockSpec(shape, lambda i: (0,) * len(shape))
    groups = 1 if heads_on_lanes else RWKV_HEADS
    if heads_on_lanes:
        nb, T, _ = x.shape
    else:
        T, nb, _ = x.shape
    n_tiles = T // tt
    proj_tile = lambda i: jnp.minimum(i, n_tiles - 1)
    post_tile = lambda i: jnp.maximum(i - 1, 0)
    scratch = [pltpu.VMEM((2 * LORA, LANES), f32), pltpu.VMEM((groups, 3, HEAD_DIM, LANES), f32),
               pltpu.VMEM((groups, HEAD_DIM, LANES), f32),
               pltpu.VMEM((SHIFT_DIM, tt * nb), f32), pltpu.VMEM((SHIFT_DIM, tt * nb), f32)]
    if heads_on_lanes:
        x_spec = pl.BlockSpec((nb, tt, D_MODEL), lambda i: (0, proj_tile(i), 0))
        scratch.append(pltpu.VMEM((D_MODEL // LANES, tt * nb, LANES), f32))
    else:
        x_spec = pl.BlockSpec((tt, nb, D_MODEL), lambda i: (proj_tile(i), 0, 0))
    assert period % tt == 0 or tt % period == 0
    kern = functools.partial(_rwkv_proj_kernel, tt=tt, nb=nb, heads_on_lanes=heads_on_lanes, period=period)
    return pl.pallas_call(
        kern,
        grid=(n_tiles + 1,),
        in_specs=[
            x_spec,
            const((1, D_MODEL)),
            const((SHIFT_DIM, D_MODEL)),
            const((2 * LORA, LANES)),
            const((2, RWKV_WIDTH, LANES)),
            const((RWKV_WIDTH, LORA)), const((RWKV_WIDTH, LORA)),
            const((SHIFT_DIM, LANES)),
            const((groups, NP, HEAD_DIM, LANES)),
        ],
        out_specs=[pl.BlockSpec((groups, tt, NQ, HEAD_DIM, LANES), lambda i: (0, post_tile(i), 0, 0, 0)),
                   const((SHIFT_DIM, LANES)), const((groups, HEAD_DIM, LANES))],
        out_shape=[jax.ShapeDtypeStruct((groups, T, NQ, HEAD_DIM, LANES), f32),
                   jax.ShapeDtypeStruct((SHIFT_DIM, LANES), f32),
                   jax.ShapeDtypeStruct((groups, HEAD_DIM, LANES), f32)],
        scratch_shapes=scratch,
        compiler_params=_cparams(1),
        name="rwkv_proj",
    )(x, g_norm, wt, mul_t, c2_t, w2t, a2t, shift0_t, p_tiles)


def _rope(x, c, a, b):
    return x * c + pltpu.roll(x, LANES - ROPE_DIM // 2, 1) * a + pltpu.roll(x, ROPE_DIM // 2, 1) * b


def _nat_proj_kernel(x_ref, g_ref, w_ref, rope_ref, sgr_ref, q_ref, k_ref, v_ref, sga_ref):
    u = _rmsnorm(x_ref[...], g_ref[...]).astype(bf16)
    z = _dot(u, w_ref[...])
    o_q = RWKV_WIDTH
    o_k = o_q + ATTN_WIDTH
    o_v = o_k + KV_WIDTH
    o_g = o_v + KV_WIDTH
    gr = z[:, :o_q]
    sgr_ref[...] = gr * _sigmoid(gr)
    rc, ra, rb = rope_ref[0], rope_ref[1], rope_ref[2]
    for j in range(ATTN_WIDTH // LANES):
        qj = z[:, o_q + j * LANES:o_q + (j + 1) * LANES]
        q_ref[:, j * LANES:(j + 1) * LANES] = _rope(qj, rc, ra, rb) * (HEAD_DIM ** -0.5)
    k_ref[...] = _rope(z[:, o_k:o_v], rc, ra, rb)
    v_ref[...] = z[:, o_v:o_g]
    ga = z[:, o_g:]
    sga_ref[...] = ga * _sigmoid(ga)


def _nat_proj(x2d, g_norm, w_nat, rope, *, tm):
    n_tok = x2d.shape[0]
    n_tab = rope.shape[1] // tm
    const = lambda shape: pl.BlockSpec(shape, lambda i: (0,) * len(shape))
    row = lambda w: pl.BlockSpec((tm, w), lambda i: (i, 0))
    tab = pl.BlockSpec((3, tm, LANES), lambda i: (0, i % n_tab, 0))
    return pl.pallas_call(
        _nat_proj_kernel,
        grid=(n_tok // tm,),
        in_specs=[row(D_MODEL), const((1, D_MODEL)), const((D_MODEL, NAT_DIM)), tab],
        out_specs=[row(RWKV_WIDTH), row(ATTN_WIDTH), row(KV_WIDTH), row(KV_WIDTH), row(ATTN_WIDTH)],
        out_shape=[jax.ShapeDtypeStruct((n_tok, w), f32)
                   for w in (RWKV_WIDTH, ATTN_WIDTH, KV_WIDTH, KV_WIDTH, ATTN_WIDTH)],
        compiler_params=_cparams(1),
        name="nat_proj",
    )(x2d, g_norm, w_nat, rope)


def _row_bcast(ref, idx, k):
    return jnp.broadcast_to(ref[idx + (pl.ds(k, 1), slice(None))], (HEAD_DIM, LANES))


def _wkv_step(s_ref, ref, at, at_next, sa, rk, lnw, lnb):
    vv = ref[at + (Q_V,)]
    y = jnp.zeros((HEAD_DIM, LANES), f32)
    sa_next = jnp.zeros((HEAD_DIM, LANES), f32)
    for k in range(HEAD_DIM):
        s_new = (s_ref[k] - sa * _row_bcast(ref, at + (Q_B,), k)
                 + vv * _row_bcast(ref, at + (Q_K,), k))
        s_ref[k] = s_new
        y = y + s_new * _row_bcast(ref, at + (Q_R,), k)
        sa_next = sa_next + s_new * _row_bcast(ref, at_next + (Q_KK,), k)
    mean = jnp.mean(y, axis=0, keepdims=True)
    d = y - mean
    var = jnp.mean(d * d, axis=0, keepdims=True)
    yn = d * lax.rsqrt(var + GN_EPS) * lnw + lnb
    rkk = jnp.sum(ref[at + (Q_R,)] * ref[at + (Q_K,)] * rk, axis=0, keepdims=True)
    return yn + rkk * vv, sa_next


def _wkv_scan_kernel(in_ref, s0_ref, wend_ref, pt_ref, o_ref, sout_ref, s_ref, *rest,
                     tc, period, natural_out):
    c = pl.program_id(1)

    @pl.when(c == 0)
    def _():
        s_ref[...] = s0_ref[0]

    obuf_ref = rest[0] if natural_out else None
    rk, lnw, lnb = pt_ref[0, P_RK], pt_ref[0, P_LNW], pt_ref[0, P_LNB]

    for t0 in range(0, tc, period):
        sa0 = jnp.zeros((HEAD_DIM, LANES), f32)
        for k in range(HEAD_DIM):
            s_k = s_ref[k] * _row_bcast(in_ref, (0, t0, Q_W), k)
            s_ref[k] = s_k
            sa0 = sa0 + s_k * _row_bcast(in_ref, (0, t0, Q_KK), k)

        def step(t, sa, last=t0 + period - 1):
            t_next = jnp.minimum(t + 1, last)
            o, sa_next = _wkv_step(s_ref, in_ref, (0, t), (0, t_next), sa, rk, lnw, lnb)
            if natural_out:
                obuf_ref[t] = o
            else:
                o_ref[0, t] = o
            return sa_next

        lax.fori_loop(t0, t0 + period, step, sa0)

    if natural_out:
        nb = LANES // RWKV_HEADS
        half = LANES // 2
        low = lax.broadcasted_iota(jnp.int32, (nb, LANES), 1) < half
        for t in range(0, tc, 2):
            both = jnp.concatenate([obuf_ref[t], obuf_ref[t + 1]], axis=0)
            both_t = both.T
            for pair in range(RWKV_HEADS // 2):
                h0 = both_t[(2 * pair) * nb:(2 * pair + 1) * nb]
                h1 = both_t[(2 * pair + 1) * nb:(2 * pair + 2) * nb]
                cs = slice(pair * LANES, (pair + 1) * LANES)
                o_ref[t, :, cs] = jnp.where(low, h0, pltpu.roll(h1, half, 1))
                o_ref[t + 1, :, cs] = jnp.where(low, pltpu.roll(h0, half, 1), h1)

    @pl.when(c == pl.num_programs(1) - 1)
    def _():
        for k in range(HEAD_DIM):
            sout_ref[0, k] = s_ref[k] * _row_bcast(wend_ref, (0,), k)


def _wkv_scan(scan_in, s0, wend, p_tiles, *, tc, period, natural_out):
    groups, T = scan_in.shape[:2]
    assert tc % period == 0
    tile = pl.BlockSpec((1, HEAD_DIM, LANES), lambda g, c: (g, 0, 0))
    state = pl.BlockSpec((1, HEAD_DIM, HEAD_DIM, LANES), lambda g, c: (g, 0, 0, 0))
    scratch = [pltpu.VMEM((HEAD_DIM, HEAD_DIM, LANES), f32)]
    if natural_out:
        nb = LANES // RWKV_HEADS
        o_spec = pl.BlockSpec((tc, nb, RWKV_WIDTH), lambda g, c: (c, 0, 0))
        o_shape = jax.ShapeDtypeStruct((T, nb, RWKV_WIDTH), f32)
        scratch.append(pltpu.VMEM((tc, HEAD_DIM, LANES), f32))
    else:
        o_spec = pl.BlockSpec((1, tc, HEAD_DIM, LANES), lambda g, c: (g, c, 0, 0))
        o_shape = jax.ShapeDtypeStruct((groups, T, HEAD_DIM, LANES), f32)
    return pl.pallas_call(
        functools.partial(_wkv_scan_kernel, tc=tc, period=period, natural_out=natural_out),
        grid=(groups, T // tc),
        in_specs=[
            pl.BlockSpec((1, tc, NQ, HEAD_DIM, LANES), lambda g, c: (g, c, 0, 0, 0)),
            state, tile, pl.BlockSpec((1, NP, HEAD_DIM, LANES), lambda g, c: (g, 0, 0, 0)),
        ],
        out_specs=[o_spec, state],
        out_shape=[o_shape, jax.ShapeDtypeStruct((groups, HEAD_DIM, HEAD_DIM, LANES), f32)],
        scratch_shapes=scratch,
        compiler_params=_cparams(2),
        name="wkv_scan",
    )(scan_in, s0, wend, p_tiles)


def _swa_attn_kernel(q_ref, kp_ref, kc_ref, vp_ref, vc_ref, sga_ref, sink_ref, o_ref, *,
                     bb, tq):
    nr = bb * tq
    half = LANES // 2
    tq_bits = tq.bit_length() - 1

    def key_mask(n_keys_per_batch, is_prev):
        rows = lax.broadcasted_iota(jnp.int32, (nr, bb * n_keys_per_batch), 0)
        cols = lax.broadcasted_iota(jnp.int32, (nr, bb * n_keys_per_batch), 1)
        i = rows & (tq - 1)
        j = cols & (n_keys_per_batch - 1)
        ok = (j > i) if is_prev else (j <= i)
        if bb > 1:
            same = (rows >> tq_bits) == (cols >> (n_keys_per_batch.bit_length() - 1))
            ok = jnp.logical_and(same, ok)
        return jnp.tile(ok, (ATTN_GROUP, 1))

    mask_p = key_mask(WINDOW, True)
    mask_c = key_mask(tq, False)
    lane_q = lax.broadcasted_iota(jnp.int32, (nr, LANES), 1)
    lane_kp = lax.broadcasted_iota(jnp.int32, (bb * WINDOW, LANES), 1)

    kp = kp_ref[...].reshape(bb * WINDOW, LANES).astype(bf16)
    kc = kc_ref[...].reshape(nr, LANES).astype(bf16)
    vp = vp_ref[...].reshape(bb * WINDOW, LANES)
    vc = vc_ref[...].reshape(nr, LANES)

    res = []
    for g in range(ATTN_KV_HEADS):
        in_g = (lambda lane: lane < half) if g == 0 else (lambda lane: lane >= half)
        qs, sinks = [], []
        for hh in range(ATTN_GROUP):
            x = q_ref[:, :, hh * LANES:(hh + 1) * LANES].reshape(nr, LANES)
            qs.append(jnp.where(in_g(lane_q), x, 0.0).astype(bf16))
            sinks.append(jnp.broadcast_to(sink_ref[g * ATTN_GROUP + hh:g * ATTN_GROUP + hh + 1, :], (nr, LANES)))
        qg = jnp.concatenate(qs, axis=0)
        sink = jnp.concatenate(sinks, axis=0)[:, 0:1]
        sp = jnp.where(mask_p, _dot_nt(qg, kp), NEG_INF)
        sc = jnp.where(mask_c, _dot_nt(qg, kc), NEG_INF)
        m = jnp.maximum(jnp.maximum(jnp.max(sp, axis=-1, keepdims=True),
                                    jnp.max(sc, axis=-1, keepdims=True)), sink)
        pp = jnp.exp(sp - m).astype(bf16)
        pc = jnp.exp(sc - m).astype(bf16)
        e_sink = jnp.exp(sink - m)
        vpg = jnp.where(in_g(lane_kp), vp, 1.0).astype(bf16)
        vcg = jnp.where(in_g(lane_q), vc, 1.0).astype(bf16)
        pv = _dot(pp, vpg) + _dot(pc, vcg)
        res.append(pv * (1.0 / (pltpu.roll(pv, half, 1) + e_sink)))
    for hh in range(ATTN_GROUP):
        blk = slice(hh * nr, (hh + 1) * nr)
        cs = slice(hh * LANES, (hh + 1) * LANES)
        out = jnp.where(lane_q < half, res[0][blk], res[1][blk])
        o_ref[:, :, cs] = (out.reshape(bb, tq, LANES) * sga_ref[:, :, cs]).astype(o_ref.dtype)


def _attn_block_t(q_cols, k_all, v_all, ok, sink_ref):
    tq = WINDOW
    half = LANES // 2
    lane_q = lax.broadcasted_iota(jnp.int32, (tq, LANES), 1)
    lane_k = lax.broadcasted_iota(jnp.int32, (2 * WINDOW, LANES), 1)
    mask_t = jnp.tile(ok, (1, ATTN_GROUP))
    k_bf = k_all.astype(bf16)
    norm_t = []
    for g in range(ATTN_KV_HEADS):
        in_g = (lambda lane: lane < half) if g == 0 else (lambda lane: lane >= half)
        qs = [jnp.where(in_g(lane_q), x, jnp.zeros_like(x)).astype(bf16) for x in q_cols]
        sinks = [sink_ref[g * ATTN_GROUP + hh:g * ATTN_GROUP + hh + 1, :] for hh in range(ATTN_GROUP)]
        qg = jnp.concatenate(qs, axis=0)
        sink = jnp.concatenate(sinks, axis=1)
        st = jnp.where(mask_t, _dot_nt(k_bf, qg), NEG_INF)
        m = jnp.maximum(jnp.max(st, axis=0, keepdims=True), sink)
        p = jnp.exp(st - m).astype(bf16)
        e_sink = jnp.exp(sink - m)
        vg = jnp.where(in_g(lane_k), v_all, 1.0).astype(bf16)
        ot = lax.dot_general(vg, p, (((0,), (0,)), ((), ())), preferred_element_type=f32)
        lo, hi = ot[:half], ot[half:]
        num, den = (lo, hi) if g == 0 else (hi, lo)
        norm_t.append(num * (1.0 / (den + e_sink)))
    return [jnp.concatenate([norm_t[0][:, hh * tq:(hh + 1) * tq],
                             norm_t[1][:, hh * tq:(hh + 1) * tq]], axis=0).T for hh in range(ATTN_GROUP)]


def _nat_attn_kernel(x_ref, g_ref, w_ref, rope_ref, sink_ref,
                     sgr_ref, ma_ref, klast_ref, vlast_ref, kprev_ref, vprev_ref, *bufs, n_blk, tiles_per_seq):
    s = pl.program_id(0)
    tq = WINDOW
    buf_a, buf_b = bufs[:4], bufs[4:]

    @pl.when(s == 0)
    def _():
        kprev_ref[...] = jnp.zeros(kprev_ref.shape, f32)
        vprev_ref[...] = jnp.zeros(vprev_ref.shape, f32)
        for ref in buf_b:
            ref[...] = jnp.zeros(ref.shape, ref.dtype)

    def project(q_ref, k_ref, v_ref, sg_ref):
        u = _rmsnorm(x_ref[0], g_ref[...]).astype(bf16)
        z = _dot(u, w_ref[...])
        o_q = RWKV_WIDTH
        o_k = o_q + ATTN_WIDTH
        o_v = o_k + KV_WIDTH
        o_g = o_v + KV_WIDTH
        gr = z[:, :o_q]
        sgr_ref[0] = gr * _sigmoid(gr)
        rc, ra, rb = rope_ref[0], rope_ref[1], rope_ref[2]
        for j in range(ATTN_WIDTH // LANES):
            qj = z[:, o_q + j * LANES:o_q + (j + 1) * LANES]
            q_ref[j] = (_rope(qj, rc, ra, rb) * (HEAD_DIM ** -0.5)).astype(bf16)
        k = _rope(z[:, o_k:o_v], rc, ra, rb)
        v = z[:, o_v:o_g]
        k_ref[...] = k
        v_ref[...] = v
        ga = z[:, o_g:]
        sg_ref[...] = ga * _sigmoid(ga)
        last = slice((n_blk - 1) * tq, n_blk * tq)
        klast_ref[0] = k[last]
        vlast_ref[0] = v[last]

    def attend(q_ref, k_ref, v_ref, sg_ref):
        seq_start = (s - 1) % tiles_per_seq == 0
        keys = lax.broadcasted_iota(jnp.int32, (2 * WINDOW, tq), 0)
        qi = lax.broadcasted_iota(jnp.int32, (2 * WINDOW, tq), 1)
        prev_ok = jnp.logical_and(keys < WINDOW, keys > qi)
        cur_ok = jnp.logical_and(keys >= WINDOW, keys - WINDOW <= qi)
        for blk in range(n_blk):
            rows = slice(blk * tq, (blk + 1) * tq)
            if blk == 0:
                kp, vp = kprev_ref[...], vprev_ref[...]
                ok = jnp.logical_or(jnp.logical_and(prev_ok, jnp.logical_not(seq_start)), cur_ok)
            else:
                before = slice((blk - 1) * tq, blk * tq)
                kp, vp = k_ref[before, :], v_ref[before, :]
                ok = jnp.logical_or(prev_ok, cur_ok)
            outs = _attn_block_t([q_ref[j, rows, :] for j in range(ATTN_GROUP)],
                                 jnp.concatenate([kp, k_ref[rows, :]], axis=0),
                                 jnp.concatenate([vp, v_ref[rows, :]], axis=0), ok, sink_ref)
            for hh in range(ATTN_GROUP):
                cs = slice(hh * LANES, (hh + 1) * LANES)
                ma_ref[0, rows, cs] = (outs[hh] * sg_ref[rows, cs]).astype(ma_ref.dtype)
        last = slice((n_blk - 1) * tq, n_blk * tq)
        kprev_ref[...] = k_ref[last, :]
        vprev_ref[...] = v_ref[last, :]

    @pl.when(s % 2 == 0)
    def _():
        project(*buf_a)
        attend(*buf_b)

    @pl.when(s % 2 == 1)
    def _():
        project(*buf_b)
        attend(*buf_a)


def _nat_attn(x, g_norm, w_nat, rope, sink_t, *, n_blk):
    B, T, _ = x.shape
    tm = n_blk * WINDOW
    tps = T // tm
    n_tiles = B * tps
    proj_tile = lambda s: jnp.minimum(s, n_tiles - 1)
    attn_tile = lambda s: jnp.maximum(s - 1, 0)
    const = lambda shape: pl.BlockSpec(shape, lambda s: (0,) * len(shape))
    tab = pl.BlockSpec((3, tm, LANES), lambda s: (0, proj_tile(s) % tps, 0))
    last = pl.BlockSpec((1, WINDOW, KV_WIDTH), lambda s: (proj_tile(s) // tps, 0, 0))
    buf = [pltpu.VMEM((ATTN_WIDTH // LANES, tm, LANES), bf16), pltpu.VMEM((tm, KV_WIDTH), f32),
           pltpu.VMEM((tm, KV_WIDTH), f32), pltpu.VMEM((tm, ATTN_WIDTH), f32)]
    sgr, ma, k_last, v_last = pl.pallas_call(
        functools.partial(_nat_attn_kernel, n_blk=n_blk, tiles_per_seq=tps),
        grid=(n_tiles + 1,),
        in_specs=[pl.BlockSpec((1, tm, D_MODEL), lambda s: (proj_tile(s), 0, 0)),
                  const((1, D_MODEL)), const((D_MODEL, NAT_DIM)), tab,
                  const((ATTN_Q_HEADS, LANES))],
        out_specs=[pl.BlockSpec((1, tm, RWKV_WIDTH), lambda s: (proj_tile(s), 0, 0)),
                   pl.BlockSpec((1, tm, ATTN_WIDTH), lambda s: (attn_tile(s), 0, 0)), last, last],
        out_shape=[jax.ShapeDtypeStruct((n_tiles, tm, RWKV_WIDTH), f32),
                   jax.ShapeDtypeStruct((n_tiles, tm, ATTN_WIDTH), bf16),
                   jax.ShapeDtypeStruct((B, WINDOW, KV_WIDTH), f32),
                   jax.ShapeDtypeStruct((B, WINDOW, KV_WIDTH), f32)],
        scratch_shapes=[pltpu.VMEM((WINDOW, KV_WIDTH), f32), pltpu.VMEM((WINDOW, KV_WIDTH), f32)] + buf + buf,
        compiler_params=_cparams(1),
        name="nat_attn",
    )(x.reshape(n_tiles, tm, D_MODEL), g_norm, w_nat, rope, sink_t)
    return sgr.reshape(B, T, RWKV_WIDTH), ma.reshape(B, T, ATTN_WIDTH), k_last, v_last


def _swa_attn(q, k_prev, k_cur, v_prev, v_cur, sga, sink_t, *, bb, tq):
    B, T, _ = q.shape
    cur = lambda w: pl.BlockSpec((bb, tq, w), lambda bi, n: (bi, n, 0))
    prev = pl.BlockSpec((bb, WINDOW, KV_WIDTH), lambda bi, n: (bi, 0, 0))
    return pl.pallas_call(
        functools.partial(_swa_attn_kernel, bb=bb, tq=tq),
        grid=(B // bb, T // tq),
        in_specs=[cur(ATTN_WIDTH), prev, cur(KV_WIDTH), prev, cur(KV_WIDTH), cur(ATTN_WIDTH),
                  pl.BlockSpec((ATTN_Q_HEADS, LANES), lambda bi, n: (0, 0))],
        out_specs=cur(ATTN_WIDTH),
        out_shape=jax.ShapeDtypeStruct((B, T, ATTN_WIDTH), f32),
        compiler_params=_cparams(2),
        name="swa_attn",
    )(q, k_prev, k_cur, v_prev, v_cur, sga, sink_t)


def _out_mix_kernel(x_ref, or_ref, sgr_ref, ma_ref, p_ref, wo_ref, gp_ref, wpg_ref, wpp_ref, gf_ref,
                    y_ref):
    mr = (or_ref[...] * sgr_ref[...]).astype(bf16)
    ma = ma_ref[...].astype(bf16)
    h = (x_ref[...] + _dot(mr, wo_ref[:RWKV_WIDTH, :]) + _dot(ma, wo_ref[RWKV_WIDTH:, :]))
    gate = _sigmoid(_dot(_rmsnorm(h, gp_ref[...]).astype(bf16), wpg_ref[...]))
    h = h + gate * _dot(p_ref[...].astype(bf16), wpp_ref[...])
    y_ref[...] = _rmsnorm(h, gf_ref[...])


def _out_mix(x2d, o_r, sgr, ma, p2d, w_out, g_ple, w_pg, w_pp, g_final, *, tm):
    n_tok = x2d.shape[0]
    const = lambda shape: pl.BlockSpec(shape, lambda i: (0,) * len(shape))
    row = lambda w: pl.BlockSpec((tm, w), lambda i: (i, 0))
    return pl.pallas_call(
        _out_mix_kernel,
        grid=(n_tok // tm,),
        in_specs=[row(D_MODEL), row(RWKV_WIDTH), row(RWKV_WIDTH), row(ATTN_WIDTH), row(PLE_DIM),
                  const((D_MODEL, D_MODEL)), const((1, D_MODEL)), const((D_MODEL, D_MODEL)),
                  const((PLE_DIM, D_MODEL)), const((1, D_MODEL))],
        out_specs=row(D_MODEL),
        out_shape=jax.ShapeDtypeStruct((n_tok, D_MODEL), f32),
        compiler_params=_cparams(1),
        name="out_mix",
    )(x2d, o_r, sgr, ma, p2d, w_out, g_ple, w_pg, w_pp, g_final)


def _rope_tables(pos):
    half = ROPE_DIM // 2
    inv = ROPE_THETA ** (-jnp.arange(half, dtype=f32) / half)
    ang = pos.astype(f32)[:, None] * inv[None, :]
    cos, sin = jnp.cos(ang), jnp.sin(ang)
    n = pos.shape[0]
    ones = jnp.ones((n, HEAD_DIM - ROPE_DIM), f32)
    zeros = jnp.zeros((n, HEAD_DIM - ROPE_DIM), f32)
    zh = jnp.zeros((n, half), f32)
    per_head = jnp.stack([jnp.concatenate([cos, cos, ones], axis=1),
                          jnp.concatenate([-sin, zh, zeros], axis=1),
                          jnp.concatenate([zh, sin, zeros], axis=1)])
    return jnp.tile(per_head, (1, 1, LANES // HEAD_DIM))


def _col_tile(vecs):
    return jnp.broadcast_to(vecs[:, :, None], vecs.shape + (LANES,))


def _lane_param_tiles(vecs, heads_on_lanes):
    ph = vecs.reshape(vecs.shape[0], RWKV_HEADS, HEAD_DIM)
    if heads_on_lanes:
        nb = LANES // RWKV_HEADS
        return jnp.repeat(jnp.swapaxes(ph, 1, 2), nb, axis=2)[None]
    return jnp.broadcast_to(jnp.swapaxes(ph, 0, 1)[:, :, :, None], (RWKV_HEADS, vecs.shape[0], HEAD_DIM, LANES))


def _layer(x, p, pos, s0, shift0, kbuf, vbuf, wts, *, heads_on_lanes, tt, tc, period, tm, att_bb, att_tq):
    B, T, _ = x.shape
    n_tok = B * T
    x2d = x.reshape(n_tok, D_MODEL)

    if shift0 is None:
        shift0_t = jnp.zeros((SHIFT_DIM, LANES), f32)
    else:
        shift0_t = shift0.T
    p_tiles = _lane_param_tiles(wts["rwkv_params"], heads_on_lanes)
    proj_args = (wts["g_norm"], wts["wt_rwkv"], wts["mul_t"], wts["c2_t"], wts["w2t"], wts["a2t"], shift0_t,
                 p_tiles)
    if heads_on_lanes:
        assert s0 is None and shift0 is None
        scan_in, shift_t, wend = _rwkv_proj(x, *proj_args, tt=tt, heads_on_lanes=True, period=period)
        shift_new = shift_t[:, LANES - B:].T
        s0_t = jnp.zeros((1, HEAD_DIM, HEAD_DIM, LANES), f32)
        o_tb, s_fin = _wkv_scan(scan_in, s0_t, wend, p_tiles, tc=tc, period=period, natural_out=True)
        o_r = jnp.swapaxes(o_tb, 0, 1).reshape(n_tok, RWKV_WIDTH)
        s_new = jnp.transpose(s_fin[0].reshape(HEAD_DIM, HEAD_DIM, RWKV_HEADS, B), (3, 2, 1, 0))
    else:
        scan_in, shift_t, wend = _rwkv_proj(jnp.swapaxes(x, 0, 1), *proj_args, tt=tt, heads_on_lanes=False,
                                            period=period)
        shift_new = shift_t.T
        s0_t = jnp.transpose(s0, (1, 3, 2, 0))
        o_scan, s_fin = _wkv_scan(scan_in, s0_t, wend, p_tiles, tc=tc, period=period, natural_out=False)
        o_r = jnp.transpose(o_scan, (3, 1, 0, 2)).reshape(n_tok, RWKV_WIDTH)
        s_new = jnp.transpose(s_fin, (3, 0, 2, 1))

    rope = _rope_tables(pos)
    if kbuf is None:
        sgr3, ma, k_last, v_last = _nat_attn(x, wts["g_norm"], wts["w_nat"], rope, wts["sink_t"],
                                             n_blk=att_tq // WINDOW)
        sgr = sgr3.reshape(n_tok, RWKV_WIDTH)
        k_new = k_last.reshape(B, WINDOW, ATTN_KV_HEADS, HEAD_DIM)
        v_new = v_last.reshape(B, WINDOW, ATTN_KV_HEADS, HEAD_DIM)
    else:
        sgr, q, k, v, sga = _nat_proj(x2d, wts["g_norm"], wts["w_nat"], jnp.tile(rope, (1, tm // T, 1)), tm=tm)
        k3 = k.reshape(B, T, KV_WIDTH)
        v3 = v.reshape(B, T, KV_WIDTH)
        kb = kbuf.reshape(B, WINDOW, KV_WIDTH)
        vb = vbuf.reshape(B, WINDOW, KV_WIDTH)
        ma = _swa_attn(q.reshape(B, T, ATTN_WIDTH), kb, k3, vb, v3, sga.reshape(B, T, ATTN_WIDTH),
                       wts["sink_t"], bb=att_bb, tq=att_tq)
        k_new = jnp.concatenate([kb, k3], axis=1)[:, -WINDOW:].reshape(B, WINDOW, ATTN_KV_HEADS, HEAD_DIM)
        v_new = jnp.concatenate([vb, v3], axis=1)[:, -WINDOW:].reshape(B, WINDOW, ATTN_KV_HEADS, HEAD_DIM)

    y = _out_mix(x2d, o_r, sgr, ma.reshape(n_tok, ATTN_WIDTH), p.reshape(n_tok, PLE_DIM),
                 wts["w_out"], wts["g_ple"], wts["w_pg"], wts["w_pp"], wts["g_final"], tm=tm)
    return y.reshape(B, T, D_MODEL), s_new, shift_new, k_new, v_new


def kernel(x_prompt, x_sample, state_rwkv_wkv, state_rwkv_shift, cache_swa_k, cache_swa_v,
           p_prompt, p_sample, g_norm, w_in, mu_shift, w0, w2, a0, a2, k_k, k_a, r_k,
           ln_w, ln_b, sinks, w_out, g_ple, w_ple_gate, w_ple_proj, g_final):
    assert w_in.shape[0] == 1, "single layer"
    w_in0 = w_in[0]
    head_order = [g * ATTN_GROUP + i for i in range(ATTN_GROUP) for g in range(ATTN_KV_HEADS)]
    cols = jnp.concatenate([jnp.arange(h * HEAD_DIM, (h + 1) * HEAD_DIM) for h in head_order])
    o_q = SHIFT_DIM + RWKV_WIDTH
    o_ga = o_q + ATTN_WIDTH + 2 * KV_WIDTH
    nat_cols = jnp.concatenate([jnp.arange(SHIFT_DIM, o_q), o_q + cols, jnp.arange(o_q + ATTN_WIDTH, o_ga),
                                o_ga + cols])
    out_rows = jnp.concatenate([jnp.arange(RWKV_WIDTH), RWKV_WIDTH + cols])
    mu = mu_shift[0]
    wts = {
        "g_norm": g_norm[0][None, :],
        "wt_rwkv": w_in0[:, :SHIFT_DIM].T.astype(bf16),
        "w_nat": w_in0[:, nat_cols].astype(bf16),
        "mul_t": _col_tile(mu[None, 3 * RWKV_WIDTH:])[0],
        "c2_t": _col_tile(jnp.stack([w0[0], a0[0]])),
        "w2t": w2[0].T.astype(bf16), "a2t": a2[0].T.astype(bf16),
        "rwkv_params": jnp.stack([mu[:RWKV_WIDTH], mu[RWKV_WIDTH:2 * RWKV_WIDTH],
                                  mu[2 * RWKV_WIDTH:3 * RWKV_WIDTH], k_k[0], k_a[0], r_k[0], ln_w[0], ln_b[0]]),
        "sink_t": _col_tile(sinks)[0],
        "w_out": w_out[0][out_rows].astype(bf16), "g_ple": g_ple[0][None, :],
        "w_pg": w_ple_gate[0].astype(bf16), "w_pp": w_ple_proj[0].astype(bf16),
        "g_final": g_final[None, :],
    }
    Bp, Tp, _ = x_prompt.shape
    Bs, Ts, _ = x_sample.shape
    assert Bp * RWKV_HEADS == LANES and Tp % PROMPT_TILES["att_tq"] == 0 and Tp % PROMPT_TILES["tc"] == 0
    assert Bs == LANES and Ts == SAMPLE_TILES["tc"] and Bs * Ts == SAMPLE_TILES["tm"]

    yp, s1, sh1, k1, v1 = _layer(x_prompt, p_prompt[0], jnp.arange(Tp), None, None, None, None, wts,
                                 heads_on_lanes=True, **PROMPT_TILES)
    ys, s2, sh2, k2, v2 = _layer(x_sample, p_sample[0], PAST_LEN + jnp.arange(Ts),
                                 state_rwkv_wkv[0], state_rwkv_shift[0], cache_swa_k[0], cache_swa_v[0],
                                 wts, heads_on_lanes=False, **SAMPLE_TILES)
    return (yp, ys, s1[None], sh1[None], k1[None], v1[None], s2[None], sh2[None], k2[None], v2[None])
```

```python
import functools
import math

import jax
import jax.numpy as jnp
from jax import lax
from jax.experimental import pallas as pl
from jax.experimental.pallas import tpu as pltpu

D_MODEL = 1024
HEAD_DIM = 64
RWKV_WIDTH = 512
RWKV_HEADS = 8
ATTN_WIDTH = 512
ATTN_Q_HEADS = 8
ATTN_KV_HEADS = 2
ATTN_GROUP = 4
KV_WIDTH = 128
LORA = 64
WINDOW = 128
ROPE_THETA = 500000.0
ROPE_DIM = 16
PLE_DIM = 256
NORM_EPS = 1e-6
GN_EPS = 64e-5
NEG_INF = -1e30
PAST_LEN = 16384
SHIFT_DIM = 3 * RWKV_WIDTH + 2 * LORA
NAT_DIM = RWKV_WIDTH + ATTN_WIDTH + 2 * KV_WIDTH + ATTN_WIDTH

LANES = 128
VMEM_LIMIT = 56 * 1024 * 1024
DECAY_SCALE = math.exp(-0.5)

PROMPT_TILES = dict(tt=32, tc=64, period=32, tm=1024, att_bb=1, att_tq=8 * WINDOW)
SAMPLE_TILES = dict(tt=2, tc=8, period=8, tm=1024, att_bb=8, att_tq=8)

Q_KK, Q_W, Q_B, Q_K, Q_R, Q_V = range(6)
NQ = 6
P_MU_R, P_MU_K, P_MU_V, P_KK, P_KA, P_RK, P_LNW, P_LNB = range(8)
NP = 8

f32 = jnp.float32
bf16 = jnp.bfloat16


def _cparams(n_axes):
    return pltpu.CompilerParams(dimension_semantics=("arbitrary",) * n_axes,
                                vmem_limit_bytes=VMEM_LIMIT)


def _rmsnorm(x, g):
    ms = jnp.mean(x * x, axis=-1, keepdims=True)
    return x * lax.rsqrt(ms + NORM_EPS) * g


def _sigmoid(x):
    return 1.0 / (1.0 + jnp.exp(-x))


def _dot_nt(a, b):
    return lax.dot_general(a, b, (((1,), (1,)), ((), ())), preferred_element_type=f32)


def _dot(a, b):
    return jnp.dot(a, b, preferred_element_type=f32)


def _chunk_transpose(xs, chunk):
    lane = lax.broadcasted_iota(jnp.int32, xs[0].shape, 1)
    xs = list(xs)
    for d in (4, 2, 1):
        hi_lanes = (lane & (chunk * d)) != 0
        nxt = list(xs)
        for i in range(8):
            if i & d:
                continue
            lo, hi = xs[i], xs[i + d]
            if 2 * chunk * d == LANES:
                moved = pltpu.roll(jnp.where(hi_lanes, lo, hi), chunk * d, 1)
                nxt[i] = jnp.where(hi_lanes, moved, lo)
                nxt[i + d] = jnp.where(hi_lanes, hi, moved)
            else:
                nxt[i] = jnp.where(hi_lanes, pltpu.roll(hi, chunk * d, 1), lo)
                nxt[i + d] = jnp.where(hi_lanes, hi, pltpu.roll(lo, LANES - chunk * d, 1))
        xs = nxt
    return xs


def _rwkv_proj_kernel(x_ref, g_ref, wt_ref, mul_ref, c2_ref, w2t_ref, a2t_ref, shift0_ref,
                      pt_ref, *refs, tt, nb, heads_on_lanes, period):
    if heads_on_lanes:
        perm_ref, refs = refs[0], refs[1:]
    out_ref, shift_ref, wend_ref, carry_ref, prev_ref, wc_ref, za_ref, zb_ref = refs
    i = pl.program_id(0)
    groups = 1 if heads_on_lanes else RWKV_HEADS
    lora0 = 3 * RWKV_WIDTH
    n = tt * nb
    first = i == 0

    def carry0():
        return shift0_ref[lora0:, :]

    def prev0(g, q):
        if heads_on_lanes:
            return jnp.zeros((HEAD_DIM, LANES), f32)
        r0 = q * RWKV_WIDTH + g * HEAD_DIM
        return shift0_ref[r0:r0 + HEAD_DIM, :]

    @pl.when(first)
    def _():
        zb_ref[...] = jnp.zeros(zb_ref.shape, f32)
        carry_ref[...] = carry0()
        wc_ref[...] = jnp.ones(wc_ref.shape, f32)
        for g in range(groups):
            for q in range(3):
                prev_ref[g, q] = prev0(g, q)

    def project(z_ref):
        u = _rmsnorm(x_ref[...].reshape(n, D_MODEL), g_ref[...]).astype(bf16)
        if heads_on_lanes:
            u = _dot(perm_ref[...], u).astype(bf16)
        z_ref[...] = _dot_nt(wt_ref[...], u)

    def post(z_ref):
        tile = i - 1
        lane = lax.broadcasted_iota(jnp.int32, (2 * LORA, LANES), 1)
        ones = jnp.ones((HEAD_DIM, LANES), f32)
        prev = [[prev_ref[g, q] for q in range(3)] for g in range(groups)]
        wc = [wc_ref[g] for g in range(groups)]
        prev_rot = carry_ref[...]
        steps_per_blk = LANES // nb if heads_on_lanes else 1
        for j in range(n // LANES):
            z = z_ref[:, j * LANES:(j + 1) * LANES]
            zl = z[lora0:]
            if nb % LANES == 0:
                prev_l, zl_rot = prev_rot, zl
            else:
                zl_rot = pltpu.roll(zl, nb, 1)
                prev_l = jnp.where(lane < nb, prev_rot, zl_rot)
            prev_rot = zl_rot
            zls = zl + mul_ref[...] * (prev_l - zl)
            w_pre = c2_ref[0] + _dot(w2t_ref[...], jnp.tanh(zls[:LORA]).astype(bf16))
            decay = jnp.exp(-DECAY_SCALE * _sigmoid(w_pre))
            a_all = _sigmoid(c2_ref[1] + _dot(a2t_ref[...], zls[LORA:].astype(bf16)))
            raw = [z[0:RWKV_WIDTH], z[RWKV_WIDTH:2 * RWKV_WIDTH], z[2 * RWKV_WIDTH:lora0], decay, a_all]
            per_head = [[val[h * HEAD_DIM:(h + 1) * HEAD_DIM] for h in range(RWKV_HEADS)] for val in raw]
            if heads_on_lanes:
                tiles = [_chunk_transpose(ph, nb) for ph in per_head]
            for ls in range(steps_per_blk):
                local = j * steps_per_blk + ls
                for g in range(groups):
                    idx = ls if heads_on_lanes else g
                    r_raw, kx_raw, v_raw, w_t, a_t = (tiles[q][idx] if heads_on_lanes else per_head[q][idx]
                                                      for q in range(5))
                    r = r_raw + pt_ref[g, P_MU_R] * (prev[g][0] - r_raw)
                    kx = kx_raw + pt_ref[g, P_MU_K] * (prev[g][1] - kx_raw)
                    v = v_raw + pt_ref[g, P_MU_V] * (prev[g][2] - v_raw)
                    prev[g] = [r_raw, kx_raw, v_raw]
                    kkr = kx * pt_ref[g, P_KK]
                    ss = jnp.sum(kkr * kkr, axis=0, keepdims=True)
                    kk = kkr * (1.0 / jnp.maximum(jnp.sqrt(ss), 1e-12))
                    k = kx * (1.0 + (a_t - 1.0) * pt_ref[g, P_KA])
                    b = kk * a_t
                    wc_in = wc[g]
                    if tt >= period:
                        wc_base = ones if local % period == 0 else wc_in
                    elif local == 0:
                        wc_base = jnp.where(tile % (period // tt) == 0, ones, wc_in)
                    else:
                        wc_base = wc_in
                    wc_t = wc_base * w_t
                    inv_wc = 1.0 / wc_t
                    step = (g, local)
                    out_ref[step + (Q_KK,)] = kk * wc_base
                    out_ref[step + (Q_W,)] = wc_in
                    out_ref[step + (Q_B,)] = b * inv_wc
                    out_ref[step + (Q_K,)] = k * inv_wc
                    out_ref[step + (Q_R,)] = r * wc_t
                    out_ref[step + (Q_V,)] = v
                    wc[g] = wc_t
        carry_ref[...] = jnp.where(first, carry0(), prev_rot)
        shift_ref[...] = z
        for g in range(groups):
            wc_g = jnp.where(first, ones, wc[g])
            wc_ref[g] = wc_g
            wend_ref[g] = wc_g
            for q in range(3):
                prev_ref[g, q] = jnp.where(first, prev0(g, q), prev[g][q])

    @pl.when(i % 2 == 0)
    def _():
        project(za_ref)
        post(zb_ref)

    @pl.when(i % 2 == 1)
    def _():
        project(zb_ref)
        post(za_ref)


def _rwkv_proj(x, g_norm, wt, mul_t, c2_t, w2t, a2t, shift0_t, p_tiles, *, tt, heads_on_lanes, period):
    const = lambda shape: pl.BlockSpec(shape, lambda i: (0,) * len(shape))
    groups = 1 if heads_on_lanes else RWKV_HEADS
    if heads_on_lanes:
        nb, T, _ = x.shape
    else:
        T, nb, _ = x.shape
    n_tiles = T // tt
    proj_tile = lambda i: jnp.minimum(i, n_tiles - 1)
    post_tile = lambda i: jnp.maximum(i - 1, 0)
    scratch = [pltpu.VMEM((2 * LORA, LANES), f32), pltpu.VMEM((groups, 3, HEAD_DIM, LANES), f32),
               pltpu.VMEM((groups, HEAD_DIM, LANES), f32),
               pltpu.VMEM((SHIFT_DIM, tt * nb), f32), pltpu.VMEM((SHIFT_DIM, tt * nb), f32)]
    extra_in, extra_specs = [], []
    if heads_on_lanes:
        x_spec = pl.BlockSpec((nb, tt, D_MODEL), lambda i: (0, proj_tile(i), 0))
        src = (jnp.arange(tt * nb) % nb) * tt + jnp.arange(tt * nb) // nb
        extra_in = [jax.nn.one_hot(src, tt * nb, dtype=bf16)]
        extra_specs = [const((tt * nb, tt * nb))]
    else:
        x_spec = pl.BlockSpec((tt, nb, D_MODEL), lambda i: (proj_tile(i), 0, 0))
    assert period % tt == 0 or tt % period == 0
    kern = functools.partial(_rwkv_proj_kernel, tt=tt, nb=nb, heads_on_lanes=heads_on_lanes, period=period)
    return pl.pallas_call(
        kern,
        grid=(n_tiles + 1,),
        in_specs=[
            x_spec,
            const((1, D_MODEL)),
            const((SHIFT_DIM, D_MODEL)),
            const((2 * LORA, LANES)),
            const((2, RWKV_WIDTH, LANES)),
            const((RWKV_WIDTH, LORA)), const((RWKV_WIDTH, LORA)),
            const((SHIFT_DIM, LANES)),
            const((groups, NP, HEAD_DIM, LANES)),
        ] + extra_specs,
        out_specs=[pl.BlockSpec((groups, tt, NQ, HEAD_DIM, LANES), lambda i: (0, post_tile(i), 0, 0, 0)),
                   const((SHIFT_DIM, LANES)), const((groups, HEAD_DIM, LANES))],
        out_shape=[jax.ShapeDtypeStruct((groups, T, NQ, HEAD_DIM, LANES), f32),
                   jax.ShapeDtypeStruct((SHIFT_DIM, LANES), f32),
                   jax.ShapeDtypeStruct((groups, HEAD_DIM, LANES), f32)],
        scratch_shapes=scratch,
        compiler_params=_cparams(1),
        name="rwkv_proj",
    )(x, g_norm, wt, mul_t, c2_t, w2t, a2t, shift0_t, p_tiles, *extra_in)


def _rope(x, c, a, b):
    return x * c + pltpu.roll(x, LANES - ROPE_DIM // 2, 1) * a + pltpu.roll(x, ROPE_DIM // 2, 1) * b


def _nat_proj_kernel(x_ref, g_ref, w_ref, rope_ref, sgr_ref, q_ref, k_ref, v_ref, sga_ref):
    u = _rmsnorm(x_ref[...], g_ref[...]).astype(bf16)
    z = _dot(u, w_ref[...])
    o_q = RWKV_WIDTH
    o_k = o_q + ATTN_WIDTH
    o_v = o_k + KV_WIDTH
    o_g = o_v + KV_WIDTH
    gr = z[:, :o_q]
    sgr_ref[...] = gr * _sigmoid(gr)
    rc, ra, rb = rope_ref[0], rope_ref[1], rope_ref[2]
    for j in range(ATTN_WIDTH // LANES):
        qj = z[:, o_q + j * LANES:o_q + (j + 1) * LANES]
        q_ref[:, j * LANES:(j + 1) * LANES] = _rope(qj, rc, ra, rb) * (HEAD_DIM ** -0.5)
    k_ref[...] = _rope(z[:, o_k:o_v], rc, ra, rb)
    v_ref[...] = z[:, o_v:o_g]
    ga = z[:, o_g:]
    sga_ref[...] = ga * _sigmoid(ga)


def _nat_proj(x2d, g_norm, w_nat, rope, *, tm):
    n_tok = x2d.shape[0]
    n_tab = rope.shape[1] // tm
    const = lambda shape: pl.BlockSpec(shape, lambda i: (0,) * len(shape))
    row = lambda w: pl.BlockSpec((tm, w), lambda i: (i, 0))
    tab = pl.BlockSpec((3, tm, LANES), lambda i: (0, i % n_tab, 0))
    return pl.pallas_call(
        _nat_proj_kernel,
        grid=(n_tok // tm,),
        in_specs=[row(D_MODEL), const((1, D_MODEL)), const((D_MODEL, NAT_DIM)), tab],
        out_specs=[row(RWKV_WIDTH), row(ATTN_WIDTH), row(KV_WIDTH), row(KV_WIDTH), row(ATTN_WIDTH)],
        out_shape=[jax.ShapeDtypeStruct((n_tok, w), f32)
                   for w in (RWKV_WIDTH, ATTN_WIDTH, KV_WIDTH, KV_WIDTH, ATTN_WIDTH)],
        compiler_params=_cparams(1),
        name="nat_proj",
    )(x2d, g_norm, w_nat, rope)


def _row_bcast(ref, idx, k):
    return jnp.broadcast_to(ref[idx + (pl.ds(k, 1), slice(None))], (HEAD_DIM, LANES))


def _wkv_step(s_ref, ref, at, at_next, sa, rk, lnw, lnb):
    vv = ref[at + (Q_V,)]
    y = jnp.zeros((HEAD_DIM, LANES), f32)
    sa_next = jnp.zeros((HEAD_DIM, LANES), f32)
    for k in range(HEAD_DIM):
        s_new = (s_ref[k] - sa * _row_bcast(ref, at + (Q_B,), k)
                 + vv * _row_bcast(ref, at + (Q_K,), k))
        s_ref[k] = s_new
        y = y + s_new * _row_bcast(ref, at + (Q_R,), k)
        sa_next = sa_next + s_new * _row_bcast(ref, at_next + (Q_KK,), k)
    mean = jnp.mean(y, axis=0, keepdims=True)
    d = y - mean
    var = jnp.mean(d * d, axis=0, keepdims=True)
    yn = d * lax.rsqrt(var + GN_EPS) * lnw + lnb
    rkk = jnp.sum(ref[at + (Q_R,)] * ref[at + (Q_K,)] * rk, axis=0, keepdims=True)
    return yn + rkk * vv, sa_next


def _wkv_scan_kernel(in_ref, s0_ref, wend_ref, pt_ref, o_ref, sout_ref, s_ref, *rest,
                     tc, period, natural_out):
    c = pl.program_id(1)

    @pl.when(c == 0)
    def _():
        s_ref[...] = s0_ref[0]

    obuf_ref = rest[0] if natural_out else None
    rk, lnw, lnb = pt_ref[0, P_RK], pt_ref[0, P_LNW], pt_ref[0, P_LNB]

    for t0 in range(0, tc, period):
        sa0 = jnp.zeros((HEAD_DIM, LANES), f32)
        for k in range(HEAD_DIM):
            s_k = s_ref[k] * _row_bcast(in_ref, (0, t0, Q_W), k)
            s_ref[k] = s_k
            sa0 = sa0 + s_k * _row_bcast(in_ref, (0, t0, Q_KK), k)

        def step(t, sa, last=t0 + period - 1):
            t_next = jnp.minimum(t + 1, last)
            o, sa_next = _wkv_step(s_ref, in_ref, (0, t), (0, t_next), sa, rk, lnw, lnb)
            if natural_out:
                obuf_ref[t] = o
            else:
                o_ref[0, t] = o
            return sa_next

        lax.fori_loop(t0, t0 + period, step, sa0)

    if natural_out:
        nb = LANES // RWKV_HEADS
        half = LANES // 2
        low = lax.broadcasted_iota(jnp.int32, (nb, LANES), 1) < half
        for t in range(0, tc, 2):
            both = jnp.concatenate([obuf_ref[t], obuf_ref[t + 1]], axis=0)
            both_t = both.T
            for pair in range(RWKV_HEADS // 2):
                h0 = both_t[(2 * pair) * nb:(2 * pair + 1) * nb]
                h1 = both_t[(2 * pair + 1) * nb:(2 * pair + 2) * nb]
                cs = slice(pair * LANES, (pair + 1) * LANES)
                o_ref[t, :, cs] = jnp.where(low, h0, pltpu.roll(h1, half, 1))
                o_ref[t + 1, :, cs] = jnp.where(low, pltpu.roll(h0, half, 1), h1)

    @pl.when(c == pl.num_programs(1) - 1)
    def _():
        for k in range(HEAD_DIM):
            sout_ref[0, k] = s_ref[k] * _row_bcast(wend_ref, (0,), k)


def _wkv_scan(scan_in, s0, wend, p_tiles, *, tc, period, natural_out):
    groups, T = scan_in.shape[:2]
    assert tc % period == 0
    tile = pl.BlockSpec((1, HEAD_DIM, LANES), lambda g, c: (g, 0, 0))
    state = pl.BlockSpec((1, HEAD_DIM, HEAD_DIM, LANES), lambda g, c: (g, 0, 0, 0))
    scratch = [pltpu.VMEM((HEAD_DIM, HEAD_DIM, LANES), f32)]
    if natural_out:
        nb = LANES // RWKV_HEADS
        o_spec = pl.BlockSpec((tc, nb, RWKV_WIDTH), lambda g, c: (c, 0, 0))
        o_shape = jax.ShapeDtypeStruct((T, nb, RWKV_WIDTH), f32)
        scratch.append(pltpu.VMEM((tc, HEAD_DIM, LANES), f32))
    else:
        o_spec = pl.BlockSpec((1, tc, HEAD_DIM, LANES), lambda g, c: (g, c, 0, 0))
        o_shape = jax.ShapeDtypeStruct((groups, T, HEAD_DIM, LANES), f32)
    return pl.pallas_call(
        functools.partial(_wkv_scan_kernel, tc=tc, period=period, natural_out=natural_out),
        grid=(groups, T // tc),
        in_specs=[
            pl.BlockSpec((1, tc, NQ, HEAD_DIM, LANES), lambda g, c: (g, c, 0, 0, 0)),
            state, tile, pl.BlockSpec((1, NP, HEAD_DIM, LANES), lambda g, c: (g, 0, 0, 0)),
        ],
        out_specs=[o_spec, state],
        out_shape=[o_shape, jax.ShapeDtypeStruct((groups, HEAD_DIM, HEAD_DIM, LANES), f32)],
        scratch_shapes=scratch,
        compiler_params=_cparams(2),
        name="wkv_scan",
    )(scan_in, s0, wend, p_tiles)


def _swa_attn_kernel(q_ref, kp_ref, kc_ref, vp_ref, vc_ref, sga_ref, sink_ref, o_ref, *,
                     bb, tq):
    nr = bb * tq
    half = LANES // 2
    tq_bits = tq.bit_length() - 1

    def key_mask(n_keys_per_batch, is_prev):
        rows = lax.broadcasted_iota(jnp.int32, (nr, bb * n_keys_per_batch), 0)
        cols = lax.broadcasted_iota(jnp.int32, (nr, bb * n_keys_per_batch), 1)
        i = rows & (tq - 1)
        j = cols & (n_keys_per_batch - 1)
        ok = (j > i) if is_prev else (j <= i)
        if bb > 1:
            same = (rows >> tq_bits) == (cols >> (n_keys_per_batch.bit_length() - 1))
            ok = jnp.logical_and(same, ok)
        return jnp.tile(ok, (ATTN_GROUP, 1))

    mask_p = key_mask(WINDOW, True)
    mask_c = key_mask(tq, False)
    lane_q = lax.broadcasted_iota(jnp.int32, (nr, LANES), 1)
    lane_kp = lax.broadcasted_iota(jnp.int32, (bb * WINDOW, LANES), 1)

    kp = kp_ref[...].reshape(bb * WINDOW, LANES).astype(bf16)
    kc = kc_ref[...].reshape(nr, LANES).astype(bf16)
    vp = vp_ref[...].reshape(bb * WINDOW, LANES)
    vc = vc_ref[...].reshape(nr, LANES)

    res = []
    for g in range(ATTN_KV_HEADS):
        in_g = (lambda lane: lane < half) if g == 0 else (lambda lane: lane >= half)
        qs, sinks = [], []
        for hh in range(ATTN_GROUP):
            x = q_ref[:, :, hh * LANES:(hh + 1) * LANES].reshape(nr, LANES)
            qs.append(jnp.where(in_g(lane_q), x, 0.0).astype(bf16))
            sinks.append(jnp.broadcast_to(sink_ref[g * ATTN_GROUP + hh:g * ATTN_GROUP + hh + 1, :], (nr, LANES)))
        qg = jnp.concatenate(qs, axis=0)
        sink = jnp.concatenate(sinks, axis=0)[:, 0:1]
        sp = jnp.where(mask_p, _dot_nt(qg, kp), NEG_INF)
        sc = jnp.where(mask_c, _dot_nt(qg, kc), NEG_INF)
        m = jnp.maximum(jnp.maximum(jnp.max(sp, axis=-1, keepdims=True),
                                    jnp.max(sc, axis=-1, keepdims=True)), sink)
        pp = jnp.exp(sp - m).astype(bf16)
        pc = jnp.exp(sc - m).astype(bf16)
        e_sink = jnp.exp(sink - m)
        vpg = jnp.where(in_g(lane_kp), vp, 1.0).astype(bf16)
        vcg = jnp.where(in_g(lane_q), vc, 1.0).astype(bf16)
        pv = _dot(pp, vpg) + _dot(pc, vcg)
        res.append(pv * (1.0 / (pltpu.roll(pv, half, 1) + e_sink)))
    for hh in range(ATTN_GROUP):
        blk = slice(hh * nr, (hh + 1) * nr)
        cs = slice(hh * LANES, (hh + 1) * LANES)
        out = jnp.where(lane_q < half, res[0][blk], res[1][blk])
        o_ref[:, :, cs] = (out.reshape(bb, tq, LANES) * sga_ref[:, :, cs]).astype(o_ref.dtype)


def _attn_block_t(q_cols, k_all, v_all, ok, sink_ref):
    tq = WINDOW
    half = LANES // 2
    lane_q = lax.broadcasted_iota(jnp.int32, (tq, LANES), 1)
    lane_k = lax.broadcasted_iota(jnp.int32, (2 * WINDOW, LANES), 1)
    mask_t = jnp.tile(ok, (1, ATTN_GROUP))
    k_bf = k_all.astype(bf16)
    norm_t = []
    for g in range(ATTN_KV_HEADS):
        in_g = (lambda lane: lane < half) if g == 0 else (lambda lane: lane >= half)
        qs = [jnp.where(in_g(lane_q), x, jnp.zeros_like(x)).astype(bf16) for x in q_cols]
        sinks = [sink_ref[g * ATTN_GROUP + hh:g * ATTN_GROUP + hh + 1, :] for hh in range(ATTN_GROUP)]
        qg = jnp.concatenate(qs, axis=0)
        sink = jnp.concatenate(sinks, axis=1)
        st = jnp.where(mask_t, _dot_nt(k_bf, qg), NEG_INF)
        m = jnp.maximum(jnp.max(st, axis=0, keepdims=True), sink)
        p = jnp.exp(st - m).astype(bf16)
        e_sink = jnp.exp(sink - m)
        vg = jnp.where(in_g(lane_k), v_all, 1.0).astype(bf16)
        ot = lax.dot_general(vg, p, (((0,), (0,)), ((), ())), preferred_element_type=f32)
        lo, hi = ot[:half], ot[half:]
        num, den = (lo, hi) if g == 0 else (hi, lo)
        norm_t.append(num * (1.0 / (den + e_sink)))
    return [jnp.concatenate([norm_t[0][:, hh * tq:(hh + 1) * tq],
                             norm_t[1][:, hh * tq:(hh + 1) * tq]], axis=0).T for hh in range(ATTN_GROUP)]


def _nat_attn_kernel(x_ref, g_ref, w_ref, rope_ref, sink_ref,
                     sgr_ref, ma_ref, klast_ref, vlast_ref, kprev_ref, vprev_ref, *bufs, n_blk, tiles_per_seq):
    s = pl.program_id(0)
    tq = WINDOW
    buf_a, buf_b = bufs[:4], bufs[4:]

    @pl.when(s == 0)
    def _():
        kprev_ref[...] = jnp.zeros(kprev_ref.shape, f32)
        vprev_ref[...] = jnp.zeros(vprev_ref.shape, f32)
        for ref in buf_b:
            ref[...] = jnp.zeros(ref.shape, ref.dtype)

    def project(q_ref, k_ref, v_ref, sg_ref):
        u = _rmsnorm(x_ref[0], g_ref[...]).astype(bf16)
        z = _dot(u, w_ref[...])
        o_q = RWKV_WIDTH
        o_k = o_q + ATTN_WIDTH
        o_v = o_k + KV_WIDTH
        o_g = o_v + KV_WIDTH
        gr = z[:, :o_q]
        sgr_ref[0] = gr * _sigmoid(gr)
        rc, ra, rb = rope_ref[0], rope_ref[1], rope_ref[2]
        for j in range(ATTN_WIDTH // LANES):
            qj = z[:, o_q + j * LANES:o_q + (j + 1) * LANES]
            q_ref[j] = (_rope(qj, rc, ra, rb) * (HEAD_DIM ** -0.5)).astype(bf16)
        k = _rope(z[:, o_k:o_v], rc, ra, rb)
        v = z[:, o_v:o_g]
        k_ref[...] = k
        v_ref[...] = v
        ga = z[:, o_g:]
        sg_ref[...] = ga * _sigmoid(ga)
        last = slice((n_blk - 1) * tq, n_blk * tq)
        klast_ref[0] = k[last]
        vlast_ref[0] = v[last]

    def attend(q_ref, k_ref, v_ref, sg_ref):
        seq_start = (s - 1) % tiles_per_seq == 0
        keys = lax.broadcasted_iota(jnp.int32, (2 * WINDOW, tq), 0)
        qi = lax.broadcasted_iota(jnp.int32, (2 * WINDOW, tq), 1)
        prev_ok = jnp.logical_and(keys < WINDOW, keys > qi)
        cur_ok = jnp.logical_and(keys >= WINDOW, keys - WINDOW <= qi)
        for blk in range(n_blk):
            rows = slice(blk * tq, (blk + 1) * tq)
            if blk == 0:
                kp, vp = kprev_ref[...], vprev_ref[...]
                ok = jnp.logical_or(jnp.logical_and(prev_ok, jnp.logical_not(seq_start)), cur_ok)
            else:
                before = slice((blk - 1) * tq, blk * tq)
                kp, vp = k_ref[before, :], v_ref[before, :]
                ok = jnp.logical_or(prev_ok, cur_ok)
            outs = _attn_block_t([q_ref[j, rows, :] for j in range(ATTN_GROUP)],
                                 jnp.concatenate([kp, k_ref[rows, :]], axis=0),
                                 jnp.concatenate([vp, v_ref[rows, :]], axis=0), ok, sink_ref)
            for hh in range(ATTN_GROUP):
                cs = slice(hh * LANES, (hh + 1) * LANES)
                ma_ref[0, rows, cs] = (outs[hh] * sg_ref[rows, cs]).astype(ma_ref.dtype)
        last = slice((n_blk - 1) * tq, n_blk * tq)
        kprev_ref[...] = k_ref[last, :]
        vprev_ref[...] = v_ref[last, :]

    @pl.when(s % 2 == 0)
    def _():
        project(*buf_a)
        attend(*buf_b)

    @pl.when(s % 2 == 1)
    def _():
        project(*buf_b)
        attend(*buf_a)


def _nat_attn(x, g_norm, w_nat, rope, sink_t, *, n_blk):
    B, T, _ = x.shape
    tm = n_blk * WINDOW
    tps = T // tm
    n_tiles = B * tps
    proj_tile = lambda s: jnp.minimum(s, n_tiles - 1)
    attn_tile = lambda s: jnp.maximum(s - 1, 0)
    const = lambda shape: pl.BlockSpec(shape, lambda s: (0,) * len(shape))
    tab = pl.BlockSpec((3, tm, LANES), lambda s: (0, proj_tile(s) % tps, 0))
    last = pl.BlockSpec((1, WINDOW, KV_WIDTH), lambda s: (proj_tile(s) // tps, 0, 0))
    buf = [pltpu.VMEM((ATTN_WIDTH // LANES, tm, LANES), bf16), pltpu.VMEM((tm, KV_WIDTH), f32),
           pltpu.VMEM((tm, KV_WIDTH), f32), pltpu.VMEM((tm, ATTN_WIDTH), f32)]
    sgr, ma, k_last, v_last = pl.pallas_call(
        functools.partial(_nat_attn_kernel, n_blk=n_blk, tiles_per_seq=tps),
        grid=(n_tiles + 1,),
        in_specs=[pl.BlockSpec((1, tm, D_MODEL), lambda s: (proj_tile(s), 0, 0)),
                  const((1, D_MODEL)), const((D_MODEL, NAT_DIM)), tab,
                  const((ATTN_Q_HEADS, LANES))],
        out_specs=[pl.BlockSpec((1, tm, RWKV_WIDTH), lambda s: (proj_tile(s), 0, 0)),
                   pl.BlockSpec((1, tm, ATTN_WIDTH), lambda s: (attn_tile(s), 0, 0)), last, last],
        out_shape=[jax.ShapeDtypeStruct((n_tiles, tm, RWKV_WIDTH), f32),
                   jax.ShapeDtypeStruct((n_tiles, tm, ATTN_WIDTH), bf16),
                   jax.ShapeDtypeStruct((B, WINDOW, KV_WIDTH), f32),
                   jax.ShapeDtypeStruct((B, WINDOW, KV_WIDTH), f32)],
        scratch_shapes=[pltpu.VMEM((WINDOW, KV_WIDTH), f32), pltpu.VMEM((WINDOW, KV_WIDTH), f32)] + buf + buf,
        compiler_params=_cparams(1),
        name="nat_attn",
    )(x.reshape(n_tiles, tm, D_MODEL), g_norm, w_nat, rope, sink_t)
    return sgr.reshape(B, T, RWKV_WIDTH), ma.reshape(B, T, ATTN_WIDTH), k_last, v_last


def _swa_attn(q, k_prev, k_cur, v_prev, v_cur, sga, sink_t, *, bb, tq):
    B, T, _ = q.shape
    cur = lambda w: pl.BlockSpec((bb, tq, w), lambda bi, n: (bi, n, 0))
    prev = pl.BlockSpec((bb, WINDOW, KV_WIDTH), lambda bi, n: (bi, 0, 0))
    return pl.pallas_call(
        functools.partial(_swa_attn_kernel, bb=bb, tq=tq),
        grid=(B // bb, T // tq),
        in_specs=[cur(ATTN_WIDTH), prev, cur(KV_WIDTH), prev, cur(KV_WIDTH), cur(ATTN_WIDTH),
                  pl.BlockSpec((ATTN_Q_HEADS, LANES), lambda bi, n: (0, 0))],
        out_specs=cur(ATTN_WIDTH),
        out_shape=jax.ShapeDtypeStruct((B, T, ATTN_WIDTH), f32),
        compiler_params=_cparams(2),
        name="swa_attn",
    )(q, k_prev, k_cur, v_prev, v_cur, sga, sink_t)


def _out_mix_kernel(x_ref, or_ref, sgr_ref, ma_ref, p_ref, wo_ref, gp_ref, wpg_ref, wpp_ref, gf_ref,
                    y_ref):
    mr = (or_ref[...] * sgr_ref[...]).astype(bf16)
    ma = ma_ref[...].astype(bf16)
    h = (x_ref[...] + _dot(mr, wo_ref[:RWKV_WIDTH, :]) + _dot(ma, wo_ref[RWKV_WIDTH:, :]))
    gate = _sigmoid(_dot(_rmsnorm(h, gp_ref[...]).astype(bf16), wpg_ref[...]))
    h = h + gate * _dot(p_ref[...].astype(bf16), wpp_ref[...])
    y_ref[...] = _rmsnorm(h, gf_ref[...])


def _out_mix(x2d, o_r, sgr, ma, p2d, w_out, g_ple, w_pg, w_pp, g_final, *, tm):
    n_tok = x2d.shape[0]
    const = lambda shape: pl.BlockSpec(shape, lambda i: (0,) * len(shape))
    row = lambda w: pl.BlockSpec((tm, w), lambda i: (i, 0))
    return pl.pallas_call(
        _out_mix_kernel,
        grid=(n_tok // tm,),
        in_specs=[row(D_MODEL), row(RWKV_WIDTH), row(RWKV_WIDTH), row(ATTN_WIDTH), row(PLE_DIM),
                  const((D_MODEL, D_MODEL)), const((1, D_MODEL)), const((D_MODEL, D_MODEL)),
                  const((PLE_DIM, D_MODEL)), const((1, D_MODEL))],
        out_specs=row(D_MODEL),
        out_shape=jax.ShapeDtypeStruct((n_tok, D_MODEL), f32),
        compiler_params=_cparams(1),
        name="out_mix",
    )(x2d, o_r, sgr, ma, p2d, w_out, g_ple, w_pg, w_pp, g_final)


def _rope_tables(pos):
    half = ROPE_DIM // 2
    inv = ROPE_THETA ** (-jnp.arange(half, dtype=f32) / half)
    ang = pos.astype(f32)[:, None] * inv[None, :]
    cos, sin = jnp.cos(ang), jnp.sin(ang)
    n = pos.shape[0]
    ones = jnp.ones((n, HEAD_DIM - ROPE_DIM), f32)
    zeros = jnp.zeros((n, HEAD_DIM - ROPE_DIM), f32)
    zh = jnp.zeros((n, half), f32)
    per_head = jnp.stack([jnp.concatenate([cos, cos, ones], axis=1),
                          jnp.concatenate([-sin, zh, zeros], axis=1),
                          jnp.concatenate([zh, sin, zeros], axis=1)])
    return jnp.tile(per_head, (1, 1, LANES // HEAD_DIM))


def _col_tile(vecs):
    return jnp.broadcast_to(vecs[:, :, None], vecs.shape + (LANES,))


def _lane_param_tiles(vecs, heads_on_lanes):
    ph = vecs.reshape(vecs.shape[0], RWKV_HEADS, HEAD_DIM)
    if heads_on_lanes:
        nb = LANES // RWKV_HEADS
        return jnp.repeat(jnp.swapaxes(ph, 1, 2), nb, axis=2)[None]
    return jnp.broadcast_to(jnp.swapaxes(ph, 0, 1)[:, :, :, None], (RWKV_HEADS, vecs.shape[0], HEAD_DIM, LANES))


def _layer(x, p, pos, s0, shift0, kbuf, vbuf, wts, *, heads_on_lanes, tt, tc, period, tm, att_bb, att_tq):
    B, T, _ = x.shape
    n_tok = B * T
    x2d = x.reshape(n_tok, D_MODEL)

    if shift0 is None:
        shift0_t = jnp.zeros((SHIFT_DIM, LANES), f32)
    else:
        shift0_t = shift0.T
    p_tiles = _lane_param_tiles(wts["rwkv_params"], heads_on_lanes)
    proj_args = (wts["g_norm"], wts["wt_rwkv"], wts["mul_t"], wts["c2_t"], wts["w2t"], wts["a2t"], shift0_t,
                 p_tiles)
    if heads_on_lanes:
        assert s0 is None and shift0 is None
        scan_in, shift_t, wend = _rwkv_proj(x, *proj_args, tt=tt, heads_on_lanes=True, period=period)
        shift_new = shift_t[:, LANES - B:].T
        s0_t = jnp.zeros((1, HEAD_DIM, HEAD_DIM, LANES), f32)
        o_tb, s_fin = _wkv_scan(scan_in, s0_t, wend, p_tiles, tc=tc, period=period, natural_out=True)
        o_r = jnp.swapaxes(o_tb, 0, 1).reshape(n_tok, RWKV_WIDTH)
        s_new = jnp.transpose(s_fin[0].reshape(HEAD_DIM, HEAD_DIM, RWKV_HEADS, B), (3, 2, 1, 0))
    else:
        scan_in, shift_t, wend = _rwkv_proj(jnp.swapaxes(x, 0, 1), *proj_args, tt=tt, heads_on_lanes=False,
                                            period=period)
        shift_new = shift_t.T
        s0_t = jnp.transpose(s0, (1, 3, 2, 0))
        o_scan, s_fin = _wkv_scan(scan_in, s0_t, wend, p_tiles, tc=tc, period=period, natural_out=False)
        o_r = jnp.transpose(o_scan, (3, 1, 0, 2)).reshape(n_tok, RWKV_WIDTH)
        s_new = jnp.transpose(s_fin, (3, 0, 2, 1))

    rope = _rope_tables(pos)
    if kbuf is None:
        sgr3, ma, k_last, v_last = _nat_attn(x, wts["g_norm"], wts["w_nat"], rope, wts["sink_t"],
                                             n_blk=att_tq // WINDOW)
        sgr = sgr3.reshape(n_tok, RWKV_WIDTH)
        k_new = k_last.reshape(B, WINDOW, ATTN_KV_HEADS, HEAD_DIM)
        v_new = v_last.reshape(B, WINDOW, ATTN_KV_HEADS, HEAD_DIM)
    else:
        sgr, q, k, v, sga = _nat_proj(x2d, wts["g_norm"], wts["w_nat"], jnp.tile(rope, (1, tm // T, 1)), tm=tm)
        k3 = k.reshape(B, T, KV_WIDTH)
        v3 = v.reshape(B, T, KV_WIDTH)
        kb = kbuf.reshape(B, WINDOW, KV_WIDTH)
        vb = vbuf.reshape(B, WINDOW, KV_WIDTH)
        ma = _swa_attn(q.reshape(B, T, ATTN_WIDTH), kb, k3, vb, v3, sga.reshape(B, T, ATTN_WIDTH),
                       wts["sink_t"], bb=att_bb, tq=att_tq)
        k_new = jnp.concatenate([kb, k3], axis=1)[:, -WINDOW:].reshape(B, WINDOW, ATTN_KV_HEADS, HEAD_DIM)
        v_new = jnp.concatenate([vb, v3], axis=1)[:, -WINDOW:].reshape(B, WINDOW, ATTN_KV_HEADS, HEAD_DIM)

    y = _out_mix(x2d, o_r, sgr, ma.reshape(n_tok, ATTN_WIDTH), p.reshape(n_tok, PLE_DIM),
                 wts["w_out"], wts["g_ple"], wts["w_pg"], wts["w_pp"], wts["g_final"], tm=tm)
    return y.reshape(B, T, D_MODEL), s_new, shift_new, k_new, v_new


def kernel(x_prompt, x_sample, state_rwkv_wkv, state_rwkv_shift, cache_swa_k, cache_swa_v,
           p_prompt, p_sample, g_norm, w_in, mu_shift, w0, w2, a0, a2, k_k, k_a, r_k,
           ln_w, ln_b, sinks, w_out, g_ple, w_ple_gate, w_ple_proj, g_final):
    assert w_in.shape[0] == 1, "single layer"
    w_in0 = w_in[0]
    head_order = [g * ATTN_GROUP + i for i in range(ATTN_GROUP) for g in range(ATTN_KV_HEADS)]
    cols = jnp.concatenate([jnp.arange(h * HEAD_DIM, (h + 1) * HEAD_DIM) for h in head_order])
    o_q = SHIFT_DIM + RWKV_WIDTH
    o_ga = o_q + ATTN_WIDTH + 2 * KV_WIDTH
    nat_cols = jnp.concatenate([jnp.arange(SHIFT_DIM, o_q), o_q + cols, jnp.arange(o_q + ATTN_WIDTH, o_ga),
                                o_ga + cols])
    out_rows = jnp.concatenate([jnp.arange(RWKV_WIDTH), RWKV_WIDTH + cols])
    mu = mu_shift[0]
    wts = {
        "g_norm": g_norm[0][None, :],
        "wt_rwkv": w_in0[:, :SHIFT_DIM].T.astype(bf16),
        "w_nat": w_in0[:, nat_cols].astype(bf16),
        "mul_t": _col_tile(mu[None, 3 * RWKV_WIDTH:])[0],
        "c2_t": _col_tile(jnp.stack([w0[0], a0[0]])),
        "w2t": w2[0].T.astype(bf16), "a2t": a2[0].T.astype(bf16),
        "rwkv_params": jnp.stack([mu[:RWKV_WIDTH], mu[RWKV_WIDTH:2 * RWKV_WIDTH],
                                  mu[2 * RWKV_WIDTH:3 * RWKV_WIDTH], k_k[0], k_a[0], r_k[0], ln_w[0], ln_b[0]]),
        "sink_t": _col_tile(sinks)[0],
        "w_out": w_out[0][out_rows].astype(bf16), "g_ple": g_ple[0][None, :],
        "w_pg": w_ple_gate[0].astype(bf16), "w_pp": w_ple_proj[0].astype(bf16),
        "g_final": g_final[None, :],
    }
    Bp, Tp, _ = x_prompt.shape
    Bs, Ts, _ = x_sample.shape
    assert Bp * RWKV_HEADS == LANES and Tp % PROMPT_TILES["att_tq"] == 0 and Tp % PROMPT_TILES["tc"] == 0
    assert Bs == LANES and Ts == SAMPLE_TILES["tc"] and Bs * Ts == SAMPLE_TILES["tm"]

    yp, s1, sh1, k1, v1 = _layer(x_prompt, p_prompt[0], jnp.arange(Tp), None, None, None, None, wts,
                                 heads_on_lanes=True, **PROMPT_TILES)
    ys, s2, sh2, k2, v2 = _layer(x_sample, p_sample[0], PAST_LEN + jnp.arange(Ts),
                                 state_rwkv_wkv[0], state_rwkv_shift[0], cache_swa_k[0], cache_swa_v[0],
                                 wts, heads_on_lanes=False, **SAMPLE_TILES)
    return (yp, ys, s1[None], sh1[None], k1[None], v1[None], s2[None], sh2[None], k2[None], v2[None])
```

```python
import functools
import math

import jax
import jax.numpy as jnp
from jax import lax
from jax.experimental import pallas as pl
from jax.experimental.pallas import tpu as pltpu

D_MODEL = 1024
HEAD_DIM = 64
RWKV_WIDTH = 512
RWKV_HEADS = 8
ATTN_WIDTH = 512
ATTN_Q_HEADS = 8
ATTN_KV_HEADS = 2
ATTN_GROUP = 4
KV_WIDTH = 128
LORA = 64
WINDOW = 128
ROPE_THETA = 500000.0
ROPE_DIM = 16
PLE_DIM = 256
NORM_EPS = 1e-6
GN_EPS = 64e-5
NEG_INF = -1e30
PAST_LEN = 16384
SHIFT_DIM = 3 * RWKV_WIDTH + 2 * LORA
NAT_DIM = RWKV_WIDTH + ATTN_WIDTH + 2 * KV_WIDTH + ATTN_WIDTH

LANES = 128
VMEM_LIMIT = 56 * 1024 * 1024
DECAY_SCALE = math.exp(-0.5)

PROMPT_TILES = dict(tt=32, tc=64, period=32, tm=1024, att_bb=1, att_tq=8 * WINDOW)
SAMPLE_TILES = dict(tt=2, tc=8, period=8, tm=1024, att_bb=8, att_tq=8)

Q_KK, Q_W, Q_B, Q_K, Q_R, Q_V = range(6)
NQ = 6
P_MU_R, P_MU_K, P_MU_V, P_KK, P_KA, P_RK, P_LNW, P_LNB = range(8)
NP = 8

f32 = jnp.float32
bf16 = jnp.bfloat16


def _cparams(n_axes):
    return pltpu.CompilerParams(dimension_semantics=("arbitrary",) * n_axes,
                                vmem_limit_bytes=VMEM_LIMIT)


def _rmsnorm(x, g):
    ms = jnp.mean(x * x, axis=-1, keepdims=True)
    return x * lax.rsqrt(ms + NORM_EPS) * g


def _sigmoid(x):
    return 1.0 / (1.0 + jnp.exp(-x))


def _dot_nt(a, b):
    return lax.dot_general(a, b, (((1,), (1,)), ((), ())), preferred_element_type=f32)


def _dot(a, b):
    return jnp.dot(a, b, preferred_element_type=f32)


def _chunk_transpose(xs, chunk):
    lane = lax.broadcasted_iota(jnp.int32, xs[0].shape, 1)
    xs = list(xs)
    for d in (4, 2, 1):
        hi_lanes = (lane & (chunk * d)) != 0
        nxt = list(xs)
        for i in range(8):
            if i & d:
                continue
            lo, hi = xs[i], xs[i + d]
            if 2 * chunk * d == LANES:
                moved = pltpu.roll(jnp.where(hi_lanes, lo, hi), chunk * d, 1)
                nxt[i] = jnp.where(hi_lanes, moved, lo)
                nxt[i + d] = jnp.where(hi_lanes, hi, moved)
            else:
                nxt[i] = jnp.where(hi_lanes, pltpu.roll(hi, chunk * d, 1), lo)
                nxt[i + d] = jnp.where(hi_lanes, hi, pltpu.roll(lo, LANES - chunk * d, 1))
        xs = nxt
    return xs


def _rwkv_proj_kernel(x_ref, g_ref, wt_ref, mul_ref, c2_ref, w2t_ref, a2t_ref, shift0_ref,
                      pt_ref, *refs, tt, nb, heads_on_lanes, period):
    if heads_on_lanes:
        perm_ref, refs = refs[0], refs[1:]
    out_ref, shift_ref, wend_ref, carry_ref, prev_ref, wc_ref, za_ref, zb_ref = refs
    i = pl.program_id(0)
    groups = 1 if heads_on_lanes else RWKV_HEADS
    lora0 = 3 * RWKV_WIDTH
    n = tt * nb
    first = i == 0

    def carry0():
        return shift0_ref[lora0:, :]

    def prev0(g, q):
        if heads_on_lanes:
            return jnp.zeros((HEAD_DIM, LANES), f32)
        r0 = q * RWKV_WIDTH + g * HEAD_DIM
        return shift0_ref[r0:r0 + HEAD_DIM, :]

    @pl.when(first)
    def _():
        zb_ref[...] = jnp.zeros(zb_ref.shape, f32)
        carry_ref[...] = carry0()
        wc_ref[...] = jnp.ones(wc_ref.shape, f32)
        for g in range(groups):
            for q in range(3):
                prev_ref[g, q] = prev0(g, q)

    def project(z_ref):
        u = _rmsnorm(x_ref[...].reshape(n, D_MODEL), g_ref[...]).astype(bf16)
        if heads_on_lanes:
            u = _dot(perm_ref[...], u).astype(bf16)
        z_ref[...] = _dot_nt(wt_ref[...], u)

    def post(z_ref):
        tile = i - 1
        lane = lax.broadcasted_iota(jnp.int32, (2 * LORA, LANES), 1)
        ones = jnp.ones((HEAD_DIM, LANES), f32)
        prev = [[prev_ref[g, q] for q in range(3)] for g in range(groups)]
        wc = [wc_ref[g] for g in range(groups)]
        prev_rot = carry_ref[...]
        steps_per_blk = LANES // nb if heads_on_lanes else 1
        for j in range(n // LANES):
            z = z_ref[:, j * LANES:(j + 1) * LANES]
            zl = z[lora0:]
            if nb % LANES == 0:
                prev_l, zl_rot = prev_rot, zl
            else:
                zl_rot = pltpu.roll(zl, nb, 1)
                prev_l = jnp.where(lane < nb, prev_rot, zl_rot)
            prev_rot = zl_rot
            zls = zl + mul_ref[...] * (prev_l - zl)
            w_pre = c2_ref[0] + _dot(w2t_ref[...], jnp.tanh(zls[:LORA]).astype(bf16))
            decay = jnp.exp(-DECAY_SCALE * _sigmoid(w_pre))
            a_all = _sigmoid(c2_ref[1] + _dot(a2t_ref[...], zls[LORA:].astype(bf16)))
            raw = [z[0:RWKV_WIDTH], z[RWKV_WIDTH:2 * RWKV_WIDTH], z[2 * RWKV_WIDTH:lora0], decay, a_all]
            per_head = [[val[h * HEAD_DIM:(h + 1) * HEAD_DIM] for h in range(RWKV_HEADS)] for val in raw]
            if heads_on_lanes:
                tiles = [_chunk_transpose(ph, nb) for ph in per_head]
            for ls in range(steps_per_blk):
                local = j * steps_per_blk + ls
                for g in range(groups):
                    idx = ls if heads_on_lanes else g
                    r_raw, kx_raw, v_raw, w_t, a_t = (tiles[q][idx] if heads_on_lanes else per_head[q][idx]
                                                      for q in range(5))
                    r = r_raw + pt_ref[g, P_MU_R] * (prev[g][0] - r_raw)
                    kx = kx_raw + pt_ref[g, P_MU_K] * (prev[g][1] - kx_raw)
                    v = v_raw + pt_ref[g, P_MU_V] * (prev[g][2] - v_raw)
                    prev[g] = [r_raw, kx_raw, v_raw]
                    kkr = kx * pt_ref[g, P_KK]
                    ss = jnp.sum(kkr * kkr, axis=0, keepdims=True)
                    kk = kkr * (1.0 / jnp.maximum(jnp.sqrt(ss), 1e-12))
                    k = kx * (1.0 + (a_t - 1.0) * pt_ref[g, P_KA])
                    b = kk * a_t
                    wc_in = wc[g]
                    if tt >= period:
                        wc_base = ones if local % period == 0 else wc_in
                    elif local == 0:
                        wc_base = jnp.where(tile % (period // tt) == 0, ones, wc_in)
                    else:
                        wc_base = wc_in
                    wc_t = wc_base * w_t
                    inv_wc = 1.0 / wc_t
                    step = (g, local)
                    out_ref[step + (Q_KK,)] = kk * wc_base
                    out_ref[step + (Q_W,)] = wc_in
                    out_ref[step + (Q_B,)] = b * inv_wc
                    out_ref[step + (Q_K,)] = k * inv_wc
                    out_ref[step + (Q_R,)] = r * wc_t
                    out_ref[step + (Q_V,)] = v
                    wc[g] = wc_t
        carry_ref[...] = jnp.where(first, carry0(), prev_rot)
        shift_ref[...] = z
        for g in range(groups):
            wc_g = jnp.where(first, ones, wc[g])
            wc_ref[g] = wc_g
            wend_ref[g] = wc_g
            for q in range(3):
                prev_ref[g, q] = jnp.where(first, prev0(g, q), prev[g][q])

    @pl.when(i % 2 == 0)
    def _():
        project(za_ref)
        post(zb_ref)

    @pl.when(i % 2 == 1)
    def _():
        project(zb_ref)
        post(za_ref)


def _rwkv_proj(x, g_norm, wt, mul_t, c2_t, w2t, a2t, shift0_t, p_tiles, *, tt, heads_on_lanes, period):
    const = lambda shape: pl.BlockSpec(shape, lambda i: (0,) * len(shape))
    groups = 1 if heads_on_lanes else RWKV_HEADS
    if heads_on_lanes:
        nb, T, _ = x.shape
    else:
        T, nb, _ = x.shape
    n_tiles = T // tt
    proj_tile = lambda i: jnp.minimum(i, n_tiles - 1)
    post_tile = lambda i: jnp.maximum(i - 1, 0)
    scratch = [pltpu.VMEM((2 * LORA, LANES), f32), pltpu.VMEM((groups, 3, HEAD_DIM, LANES), f32),
               pltpu.VMEM((groups, HEAD_DIM, LANES), f32),
               pltpu.VMEM((SHIFT_DIM, tt * nb), f32), pltpu.VMEM((SHIFT_DIM, tt * nb), f32)]
    extra_in, extra_specs = [], []
    if heads_on_lanes:
        x_spec = pl.BlockSpec((nb, tt, D_MODEL), lambda i: (0, proj_tile(i), 0))
        src = (jnp.arange(tt * nb) % nb) * tt + jnp.arange(tt * nb) // nb
        extra_in = [jax.nn.one_hot(src, tt * nb, dtype=bf16)]
        extra_specs = [const((tt * nb, tt * nb))]
    else:
        x_spec = pl.BlockSpec((tt, nb, D_MODEL), lambda i: (proj_tile(i), 0, 0))
    assert period % tt == 0 or tt % period == 0
    kern = functools.partial(_rwkv_proj_kernel, tt=tt, nb=nb, heads_on_lanes=heads_on_lanes, period=period)
    return pl.pallas_call(
        kern,
        grid=(n_tiles + 1,),
        in_specs=[
            x_spec,
            const((1, D_MODEL)),
            const((SHIFT_DIM, D_MODEL)),
            const((2 * LORA, LANES)),
            const((2, RWKV_WIDTH, LANES)),
            const((RWKV_WIDTH, LORA)), const((RWKV_WIDTH, LORA)),
            const((SHIFT_DIM, LANES)),
            const((groups, NP, HEAD_DIM, LANES)),
        ] + extra_specs,
        out_specs=[pl.BlockSpec((groups, tt, NQ, HEAD_DIM, LANES), lambda i: (0, post_tile(i), 0, 0, 0)),
                   const((SHIFT_DIM, LANES)), const((groups, HEAD_DIM, LANES))],
        out_shape=[jax.ShapeDtypeStruct((groups, T, NQ, HEAD_DIM, LANES), f32),
                   jax.ShapeDtypeStruct((SHIFT_DIM, LANES), f32),
                   jax.ShapeDtypeStruct((groups, HEAD_DIM, LANES), f32)],
        scratch_shapes=scratch,
        compiler_params=_cparams(1),
        name="rwkv_proj",
    )(x, g_norm, wt, mul_t, c2_t, w2t, a2t, shift0_t, p_tiles, *extra_in)


def _rope(x, c, a, b):
    return x * c + pltpu.roll(x, LANES - ROPE_DIM // 2, 1) * a + pltpu.roll(x, ROPE_DIM // 2, 1) * b


def _nat_proj_kernel(x_ref, g_ref, w_ref, rope_ref, sgr_ref, q_ref, k_ref, v_ref, sga_ref):
    u = _rmsnorm(x_ref[...], g_ref[...]).astype(bf16)
    z = _dot(u, w_ref[...])
    o_q = RWKV_WIDTH
    o_k = o_q + ATTN_WIDTH
    o_v = o_k + KV_WIDTH
    o_g = o_v + KV_WIDTH
    gr = z[:, :o_q]
    sgr_ref[...] = gr * _sigmoid(gr)
    rc, ra, rb = rope_ref[0], rope_ref[1], rope_ref[2]
    for j in range(ATTN_WIDTH // LANES):
        qj = z[:, o_q + j * LANES:o_q + (j + 1) * LANES]
        q_ref[:, j * LANES:(j + 1) * LANES] = _rope(qj, rc, ra, rb) * (HEAD_DIM ** -0.5)
    k_ref[...] = _rope(z[:, o_k:o_v], rc, ra, rb)
    v_ref[...] = z[:, o_v:o_g]
    ga = z[:, o_g:]
    sga_ref[...] = ga * _sigmoid(ga)


def _nat_proj(x2d, g_norm, w_nat, rope, *, tm):
    n_tok = x2d.shape[0]
    n_tab = rope.shape[1] // tm
    const = lambda shape: pl.BlockSpec(shape, lambda i: (0,) * len(shape))
    row = lambda w: pl.BlockSpec((tm, w), lambda i: (i, 0))
    tab = pl.BlockSpec((3, tm, LANES), lambda i: (0, i % n_tab, 0))
    return pl.pallas_call(
        _nat_proj_kernel,
        grid=(n_tok // tm,),
        in_specs=[row(D_MODEL), const((1, D_MODEL)), const((D_MODEL, NAT_DIM)), tab],
        out_specs=[row(RWKV_WIDTH), row(ATTN_WIDTH), row(KV_WIDTH), row(KV_WIDTH), row(ATTN_WIDTH)],
        out_shape=[jax.ShapeDtypeStruct((n_tok, w), f32)
                   for w in (RWKV_WIDTH, ATTN_WIDTH, KV_WIDTH, KV_WIDTH, ATTN_WIDTH)],
        compiler_params=_cparams(1),
        name="nat_proj",
    )(x2d, g_norm, w_nat, rope)


def _row_bcast(ref, idx, k):
    return jnp.broadcast_to(ref[idx + (pl.ds(k, 1), slice(None))], (HEAD_DIM, LANES))


def _wkv_step(s_ref, ref, at, at_next, sa, rk, lnw, lnb):
    vv = ref[at + (Q_V,)]
    y = jnp.zeros((HEAD_DIM, LANES), f32)
    sa_next = jnp.zeros((HEAD_DIM, LANES), f32)
    for k in range(HEAD_DIM):
        s_new = (s_ref[k] - sa * _row_bcast(ref, at + (Q_B,), k)
                 + vv * _row_bcast(ref, at + (Q_K,), k))
        s_ref[k] = s_new
        y = y + s_new * _row_bcast(ref, at + (Q_R,), k)
        sa_next = sa_next + s_new * _row_bcast(ref, at_next + (Q_KK,), k)
    mean = jnp.mean(y, axis=0, keepdims=True)
    d = y - mean
    var = jnp.mean(d * d, axis=0, keepdims=True)
    yn = d * lax.rsqrt(var + GN_EPS) * lnw + lnb
    rkk = jnp.sum(ref[at + (Q_R,)] * ref[at + (Q_K,)] * rk, axis=0, keepdims=True)
    return yn + rkk * vv, sa_next


def _wkv_scan_kernel(in_ref, s0_ref, wend_ref, pt_ref, o_ref, sout_ref, s_ref, *rest,
                     tc, period, natural_out):
    c = pl.program_id(1)

    @pl.when(c == 0)
    def _():
        s_ref[...] = s0_ref[0]

    obuf_ref = rest[0] if natural_out else None
    rk, lnw, lnb = pt_ref[0, P_RK], pt_ref[0, P_LNW], pt_ref[0, P_LNB]

    for t0 in range(0, tc, period):
        sa0 = jnp.zeros((HEAD_DIM, LANES), f32)
        for k in range(HEAD_DIM):
            s_k = s_ref[k] * _row_bcast(in_ref, (0, t0, Q_W), k)
            s_ref[k] = s_k
            sa0 = sa0 + s_k * _row_bcast(in_ref, (0, t0, Q_KK), k)

        def step(t, sa, last=t0 + period - 1):
            t_next = jnp.minimum(t + 1, last)
            o, sa_next = _wkv_step(s_ref, in_ref, (0, t), (0, t_next), sa, rk, lnw, lnb)
            if natural_out:
                obuf_ref[t] = o
            else:
                o_ref[0, t] = o
            return sa_next

        lax.fori_loop(t0, t0 + period, step, sa0)

    if natural_out:
        nb = LANES // RWKV_HEADS
        half = LANES // 2
        low = lax.broadcasted_iota(jnp.int32, (nb, LANES), 1) < half
        for t in range(0, tc, 2):
            both = jnp.concatenate([obuf_ref[t], obuf_ref[t + 1]], axis=0)
            both_t = both.T
            for pair in range(RWKV_HEADS // 2):
                h0 = both_t[(2 * pair) * nb:(2 * pair + 1) * nb]
                h1 = both_t[(2 * pair + 1) * nb:(2 * pair + 2) * nb]
                cs = slice(pair * LANES, (pair + 1) * LANES)
                o_ref[t, :, cs] = jnp.where(low, h0, pltpu.roll(h1, half, 1))
                o_ref[t + 1, :, cs] = jnp.where(low, pltpu.roll(h0, half, 1), h1)

    @pl.when(c == pl.num_programs(1) - 1)
    def _():
        for k in range(HEAD_DIM):
            sout_ref[0, k] = s_ref[k] * _row_bcast(wend_ref, (0,), k)


def _wkv_scan(scan_in, s0, wend, p_tiles, *, tc, period, natural_out):
    groups, T = scan_in.shape[:2]
    assert tc % period == 0
    tile = pl.BlockSpec((1, HEAD_DIM, LANES), lambda g, c: (g, 0, 0))
    state = pl.BlockSpec((1, HEAD_DIM, HEAD_DIM, LANES), lambda g, c: (g, 0, 0, 0))
    scratch = [pltpu.VMEM((HEAD_DIM, HEAD_DIM, LANES), f32)]
    if natural_out:
        nb = LANES // RWKV_HEADS
        o_spec = pl.BlockSpec((tc, nb, RWKV_WIDTH), lambda g, c: (c, 0, 0))
        o_shape = jax.ShapeDtypeStruct((T, nb, RWKV_WIDTH), f32)
        scratch.append(pltpu.VMEM((tc, HEAD_DIM, LANES), f32))
    else:
        o_spec = pl.BlockSpec((1, tc, HEAD_DIM, LANES), lambda g, c: (g, c, 0, 0))
        o_shape = jax.ShapeDtypeStruct((groups, T, HEAD_DIM, LANES), f32)
    return pl.pallas_call(
        functools.partial(_wkv_scan_kernel, tc=tc, period=period, natural_out=natural_out),
        grid=(groups, T // tc),
        in_specs=[
            pl.BlockSpec((1, tc, NQ, HEAD_DIM, LANES), lambda g, c: (g, c, 0, 0, 0)),
            state, tile, pl.BlockSpec((1, NP, HEAD_DIM, LANES), lambda g, c: (g, 0, 0, 0)),
        ],
        out_specs=[o_spec, state],
        out_shape=[o_shape, jax.ShapeDtypeStruct((groups, HEAD_DIM, HEAD_DIM, LANES), f32)],
        scratch_shapes=scratch,
        compiler_params=_cparams(2),
        name="wkv_scan",
    )(scan_in, s0, wend, p_tiles)


def _swa_attn_kernel(q_ref, kp_ref, kc_ref, vp_ref, vc_ref, sga_ref, sink_ref, o_ref, knew_ref, vnew_ref, *,
                     bb, tq):
    nr = bb * tq
    half = LANES // 2
    tq_bits = tq.bit_length() - 1

    def key_mask(n_keys_per_batch, is_prev):
        rows = lax.broadcasted_iota(jnp.int32, (nr, bb * n_keys_per_batch), 0)
        cols = lax.broadcasted_iota(jnp.int32, (nr, bb * n_keys_per_batch), 1)
        i = rows & (tq - 1)
        j = cols & (n_keys_per_batch - 1)
        ok = (j > i) if is_prev else (j <= i)
        if bb > 1:
            same = (rows >> tq_bits) == (cols >> (n_keys_per_batch.bit_length() - 1))
            ok = jnp.logical_and(same, ok)
        return jnp.tile(ok, (ATTN_GROUP, 1))

    mask_p = key_mask(WINDOW, True)
    mask_c = key_mask(tq, False)
    lane_q = lax.broadcasted_iota(jnp.int32, (nr, LANES), 1)
    lane_kp = lax.broadcasted_iota(jnp.int32, (bb * WINDOW, LANES), 1)

    kp = kp_ref[...].reshape(bb * WINDOW, LANES).astype(bf16)
    kc = kc_ref[...].reshape(nr, LANES).astype(bf16)
    vp = vp_ref[...].reshape(bb * WINDOW, LANES)
    vc = vc_ref[...].reshape(nr, LANES)

    res = []
    for g in range(ATTN_KV_HEADS):
        in_g = (lambda lane: lane < half) if g == 0 else (lambda lane: lane >= half)
        qs, sinks = [], []
        for hh in range(ATTN_GROUP):
            x = q_ref[:, :, hh * LANES:(hh + 1) * LANES].reshape(nr, LANES)
            qs.append(jnp.where(in_g(lane_q), x, 0.0).astype(bf16))
            sinks.append(jnp.broadcast_to(sink_ref[g * ATTN_GROUP + hh:g * ATTN_GROUP + hh + 1, :], (nr, LANES)))
        qg = jnp.concatenate(qs, axis=0)
        sink = jnp.concatenate(sinks, axis=0)[:, 0:1]
        sp = jnp.where(mask_p, _dot_nt(qg, kp), NEG_INF)
        sc = jnp.where(mask_c, _dot_nt(qg, kc), NEG_INF)
        m = jnp.maximum(jnp.maximum(jnp.max(sp, axis=-1, keepdims=True),
                                    jnp.max(sc, axis=-1, keepdims=True)), sink)
        pp = jnp.exp(sp - m).astype(bf16)
        pc = jnp.exp(sc - m).astype(bf16)
        e_sink = jnp.exp(sink - m)
        vpg = jnp.where(in_g(lane_kp), vp, 1.0).astype(bf16)
        vcg = jnp.where(in_g(lane_q), vc, 1.0).astype(bf16)
        pv = _dot(pp, vpg) + _dot(pc, vcg)
        res.append(pv * (1.0 / (pltpu.roll(pv, half, 1) + e_sink)))
    for hh in range(ATTN_GROUP):
        blk = slice(hh * nr, (hh + 1) * nr)
        cs = slice(hh * LANES, (hh + 1) * LANES)
        out = jnp.where(lane_q < half, res[0][blk], res[1][blk])
        o_ref[:, :, cs] = (out.reshape(bb, tq, LANES) * sga_ref[:, :, cs]).astype(o_ref.dtype)
    for new_ref, old_ref, cur_ref in ((knew_ref, kp_ref, kc_ref), (vnew_ref, vp_ref, vc_ref)):
        new_ref[:, :WINDOW - tq, :] = old_ref[:, tq:, :]
        new_ref[:, WINDOW - tq:, :] = cur_ref[...]


def _attn_block_t(q_cols, k_all, v_all, ok, sink_ref):
    tq = WINDOW
    half = LANES // 2
    lane_q = lax.broadcasted_iota(jnp.int32, (tq, LANES), 1)
    lane_k = lax.broadcasted_iota(jnp.int32, (2 * WINDOW, LANES), 1)
    mask_t = jnp.tile(ok, (1, ATTN_GROUP))
    k_bf = k_all.astype(bf16)
    norm_t = []
    for g in range(ATTN_KV_HEADS):
        in_g = (lambda lane: lane < half) if g == 0 else (lambda lane: lane >= half)
        qs = [jnp.where(in_g(lane_q), x, jnp.zeros_like(x)).astype(bf16) for x in q_cols]
        sinks = [sink_ref[g * ATTN_GROUP + hh:g * ATTN_GROUP + hh + 1, :] for hh in range(ATTN_GROUP)]
        qg = jnp.concatenate(qs, axis=0)
        sink = jnp.concatenate(sinks, axis=1)
        st = jnp.where(mask_t, _dot_nt(k_bf, qg), NEG_INF)
        m = jnp.maximum(jnp.max(st, axis=0, keepdims=True), sink)
        p = jnp.exp(st - m).astype(bf16)
        e_sink = jnp.exp(sink - m)
        vg = jnp.where(in_g(lane_k), v_all, 1.0).astype(bf16)
        ot = lax.dot_general(vg, p, (((0,), (0,)), ((), ())), preferred_element_type=f32)
        lo, hi = ot[:half], ot[half:]
        num, den = (lo, hi) if g == 0 else (hi, lo)
        norm_t.append(num * (1.0 / (den + e_sink)))
    return [jnp.concatenate([norm_t[0][:, hh * tq:(hh + 1) * tq],
                             norm_t[1][:, hh * tq:(hh + 1) * tq]], axis=0).T for hh in range(ATTN_GROUP)]


def _nat_attn_kernel(x_ref, g_ref, w_ref, rope_ref, sink_ref,
                     sgr_ref, ma_ref, klast_ref, vlast_ref, kprev_ref, vprev_ref, *bufs, n_blk, tiles_per_seq):
    s = pl.program_id(0)
    tq = WINDOW
    buf_a, buf_b = bufs[:4], bufs[4:]

    @pl.when(s == 0)
    def _():
        kprev_ref[...] = jnp.zeros(kprev_ref.shape, f32)
        vprev_ref[...] = jnp.zeros(vprev_ref.shape, f32)
        for ref in buf_b:
            ref[...] = jnp.zeros(ref.shape, ref.dtype)

    def project(q_ref, k_ref, v_ref, sg_ref):
        u = _rmsnorm(x_ref[0], g_ref[...]).astype(bf16)
        z = _dot(u, w_ref[...])
        o_q = RWKV_WIDTH
        o_k = o_q + ATTN_WIDTH
        o_v = o_k + KV_WIDTH
        o_g = o_v + KV_WIDTH
        gr = z[:, :o_q]
        sgr_ref[0] = gr * _sigmoid(gr)
        rc, ra, rb = rope_ref[0], rope_ref[1], rope_ref[2]
        for j in range(ATTN_WIDTH // LANES):
            qj = z[:, o_q + j * LANES:o_q + (j + 1) * LANES]
            q_ref[j] = (_rope(qj, rc, ra, rb) * (HEAD_DIM ** -0.5)).astype(bf16)
        k = _rope(z[:, o_k:o_v], rc, ra, rb)
        v = z[:, o_v:o_g]
        k_ref[...] = k
        v_ref[...] = v
        ga = z[:, o_g:]
        sg_ref[...] = ga * _sigmoid(ga)
        last = slice((n_blk - 1) * tq, n_blk * tq)
        klast_ref[0] = k[last]
        vlast_ref[0] = v[last]

    def attend(q_ref, k_ref, v_ref, sg_ref):
        seq_start = (s - 1) % tiles_per_seq == 0
        keys = lax.broadcasted_iota(jnp.int32, (2 * WINDOW, tq), 0)
        qi = lax.broadcasted_iota(jnp.int32, (2 * WINDOW, tq), 1)
        prev_ok = jnp.logical_and(keys < WINDOW, keys > qi)
        cur_ok = jnp.logical_and(keys >= WINDOW, keys - WINDOW <= qi)
        for blk in range(n_blk):
            rows = slice(blk * tq, (blk + 1) * tq)
            if blk == 0:
                kp, vp = kprev_ref[...], vprev_ref[...]
                ok = jnp.logical_or(jnp.logical_and(prev_ok, jnp.logical_not(seq_start)), cur_ok)
            else:
                before = slice((blk - 1) * tq, blk * tq)
                kp, vp = k_ref[before, :], v_ref[before, :]
                ok = jnp.logical_or(prev_ok, cur_ok)
            outs = _attn_block_t([q_ref[j, rows, :] for j in range(ATTN_GROUP)],
                                 jnp.concatenate([kp, k_ref[rows, :]], axis=0),
                                 jnp.concatenate([vp, v_ref[rows, :]], axis=0), ok, sink_ref)
            for hh in range(ATTN_GROUP):
                cs = slice(hh * LANES, (hh + 1) * LANES)
                ma_ref[0, rows, cs] = (outs[hh] * sg_ref[rows, cs]).astype(ma_ref.dtype)
        last = slice((n_blk - 1) * tq, n_blk * tq)
        kprev_ref[...] = k_ref[last, :]
        vprev_ref[...] = v_ref[last, :]

    @pl.when(s % 2 == 0)
    def _():
        project(*buf_a)
        attend(*buf_b)

    @pl.when(s % 2 == 1)
    def _():
        project(*buf_b)
        attend(*buf_a)


def _nat_attn(x, g_norm, w_nat, rope, sink_t, *, n_blk):
    B, T, _ = x.shape
    tm = n_blk * WINDOW
    tps = T // tm
    n_tiles = B * tps
    proj_tile = lambda s: jnp.minimum(s, n_tiles - 1)
    attn_tile = lambda s: jnp.maximum(s - 1, 0)
    const = lambda shape: pl.BlockSpec(shape, lambda s: (0,) * len(shape))
    tab = pl.BlockSpec((3, tm, LANES), lambda s: (0, proj_tile(s) % tps, 0))
    last = pl.BlockSpec((1, WINDOW, KV_WIDTH), lambda s: (proj_tile(s) // tps, 0, 0))
    buf = [pltpu.VMEM((ATTN_WIDTH // LANES, tm, LANES), bf16), pltpu.VMEM((tm, KV_WIDTH), f32),
           pltpu.VMEM((tm, KV_WIDTH), f32), pltpu.VMEM((tm, ATTN_WIDTH), f32)]
    sgr, ma, k_last, v_last = pl.pallas_call(
        functools.partial(_nat_attn_kernel, n_blk=n_blk, tiles_per_seq=tps),
        grid=(n_tiles + 1,),
        in_specs=[pl.BlockSpec((1, tm, D_MODEL), lambda s: (proj_tile(s), 0, 0)),
                  const((1, D_MODEL)), const((D_MODEL, NAT_DIM)), tab,
                  const((ATTN_Q_HEADS, LANES))],
        out_specs=[pl.BlockSpec((1, tm, RWKV_WIDTH), lambda s: (proj_tile(s), 0, 0)),
                   pl.BlockSpec((1, tm, ATTN_WIDTH), lambda s: (attn_tile(s), 0, 0)), last, last],
        out_shape=[jax.ShapeDtypeStruct((n_tiles, tm, RWKV_WIDTH), f32),
                   jax.ShapeDtypeStruct((n_tiles, tm, ATTN_WIDTH), bf16),
                   jax.ShapeDtypeStruct((B, WINDOW, KV_WIDTH), f32),
                   jax.ShapeDtypeStruct((B, WINDOW, KV_WIDTH), f32)],
        scratch_shapes=[pltpu.VMEM((WINDOW, KV_WIDTH), f32), pltpu.VMEM((WINDOW, KV_WIDTH), f32)] + buf + buf,
        compiler_params=_cparams(1),
        name="nat_attn",
    )(x.reshape(n_tiles, tm, D_MODEL), g_norm, w_nat, rope, sink_t)
    return sgr.reshape(B, T, RWKV_WIDTH), ma.reshape(B, T, ATTN_WIDTH), k_last, v_last


def _swa_attn(q, k_prev, k_cur, v_prev, v_cur, sga, sink_t, *, bb, tq):
    B, T, _ = q.shape
    assert T == tq
    cur = lambda w: pl.BlockSpec((bb, tq, w), lambda bi, n: (bi, n, 0))
    prev = pl.BlockSpec((bb, WINDOW, KV_WIDTH), lambda bi, n: (bi, 0, 0))
    cache = jax.ShapeDtypeStruct((B, WINDOW, KV_WIDTH), f32)
    return pl.pallas_call(
        functools.partial(_swa_attn_kernel, bb=bb, tq=tq),
        grid=(B // bb, T // tq),
        in_specs=[cur(ATTN_WIDTH), prev, cur(KV_WIDTH), prev, cur(KV_WIDTH), cur(ATTN_WIDTH),
                  pl.BlockSpec((ATTN_Q_HEADS, LANES), lambda bi, n: (0, 0))],
        out_specs=[cur(ATTN_WIDTH), prev, prev],
        out_shape=[jax.ShapeDtypeStruct((B, T, ATTN_WIDTH), f32), cache, cache],
        compiler_params=_cparams(2),
        name="swa_attn",
    )(q, k_prev, k_cur, v_prev, v_cur, sga, sink_t)


def _out_mix_kernel(x_ref, or_ref, sgr_ref, ma_ref, p_ref, wo_ref, gp_ref, wpg_ref, wpp_ref, gf_ref,
                    y_ref):
    mr = (or_ref[...] * sgr_ref[...]).astype(bf16)
    ma = ma_ref[...].astype(bf16)
    h = (x_ref[...] + _dot(mr, wo_ref[:RWKV_WIDTH, :]) + _dot(ma, wo_ref[RWKV_WIDTH:, :]))
    gate = _sigmoid(_dot(_rmsnorm(h, gp_ref[...]).astype(bf16), wpg_ref[...]))
    h = h + gate * _dot(p_ref[...].astype(bf16), wpp_ref[...])
    y_ref[...] = _rmsnorm(h, gf_ref[...])


def _out_mix(x2d, o_r, sgr, ma, p2d, w_out, g_ple, w_pg, w_pp, g_final, *, tm):
    n_tok = x2d.shape[0]
    const = lambda shape: pl.BlockSpec(shape, lambda i: (0,) * len(shape))
    row = lambda w: pl.BlockSpec((tm, w), lambda i: (i, 0))
    return pl.pallas_call(
        _out_mix_kernel,
        grid=(n_tok // tm,),
        in_specs=[row(D_MODEL), row(RWKV_WIDTH), row(RWKV_WIDTH), row(ATTN_WIDTH), row(PLE_DIM),
                  const((D_MODEL, D_MODEL)), const((1, D_MODEL)), const((D_MODEL, D_MODEL)),
                  const((PLE_DIM, D_MODEL)), const((1, D_MODEL))],
        out_specs=row(D_MODEL),
        out_shape=jax.ShapeDtypeStruct((n_tok, D_MODEL), f32),
        compiler_params=_cparams(1),
        name="out_mix",
    )(x2d, o_r, sgr, ma, p2d, w_out, g_ple, w_pg, w_pp, g_final)


def _rope_tables(pos):
    half = ROPE_DIM // 2
    inv = ROPE_THETA ** (-jnp.arange(half, dtype=f32) / half)
    ang = pos.astype(f32)[:, None] * inv[None, :]
    cos, sin = jnp.cos(ang), jnp.sin(ang)
    n = pos.shape[0]
    ones = jnp.ones((n, HEAD_DIM - ROPE_DIM), f32)
    zeros = jnp.zeros((n, HEAD_DIM - ROPE_DIM), f32)
    zh = jnp.zeros((n, half), f32)
    per_head = jnp.stack([jnp.concatenate([cos, cos, ones], axis=1),
                          jnp.concatenate([-sin, zh, zeros], axis=1),
                          jnp.concatenate([zh, sin, zeros], axis=1)])
    return jnp.tile(per_head, (1, 1, LANES // HEAD_DIM))


def _col_tile(vecs):
    return jnp.broadcast_to(vecs[:, :, None], vecs.shape + (LANES,))


def _lane_param_tiles(vecs, heads_on_lanes):
    ph = vecs.reshape(vecs.shape[0], RWKV_HEADS, HEAD_DIM)
    if heads_on_lanes:
        nb = LANES // RWKV_HEADS
        return jnp.repeat(jnp.swapaxes(ph, 1, 2), nb, axis=2)[None]
    return jnp.broadcast_to(jnp.swapaxes(ph, 0, 1)[:, :, :, None], (RWKV_HEADS, vecs.shape[0], HEAD_DIM, LANES))


def _layer(x, p, pos, s0, shift0, kbuf, vbuf, wts, *, heads_on_lanes, tt, tc, period, tm, att_bb, att_tq):
    B, T, _ = x.shape
    n_tok = B * T
    x2d = x.reshape(n_tok, D_MODEL)

    if shift0 is None:
        shift0_t = jnp.zeros((SHIFT_DIM, LANES), f32)
    else:
        shift0_t = shift0.T
    p_tiles = _lane_param_tiles(wts["rwkv_params"], heads_on_lanes)
    proj_args = (wts["g_norm"], wts["wt_rwkv"], wts["mul_t"], wts["c2_t"], wts["w2t"], wts["a2t"], shift0_t,
                 p_tiles)
    if heads_on_lanes:
        assert s0 is None and shift0 is None
        scan_in, shift_t, wend = _rwkv_proj(x, *proj_args, tt=tt, heads_on_lanes=True, period=period)
        shift_new = shift_t[:, LANES - B:].T
        s0_t = jnp.zeros((1, HEAD_DIM, HEAD_DIM, LANES), f32)
        o_tb, s_fin = _wkv_scan(scan_in, s0_t, wend, p_tiles, tc=tc, period=period, natural_out=True)
        o_r = jnp.swapaxes(o_tb, 0, 1).reshape(n_tok, RWKV_WIDTH)
        s_new = jnp.transpose(s_fin[0].reshape(HEAD_DIM, HEAD_DIM, RWKV_HEADS, B), (3, 2, 1, 0))
    else:
        scan_in, shift_t, wend = _rwkv_proj(jnp.swapaxes(x, 0, 1), *proj_args, tt=tt, heads_on_lanes=False,
                                            period=period)
        shift_new = shift_t.T
        s0_t = jnp.transpose(s0, (1, 3, 2, 0))
        o_scan, s_fin = _wkv_scan(scan_in, s0_t, wend, p_tiles, tc=tc, period=period, natural_out=False)
        o_r = jnp.transpose(o_scan, (3, 1, 0, 2)).reshape(n_tok, RWKV_WIDTH)
        s_new = jnp.transpose(s_fin, (3, 0, 2, 1))

    rope = _rope_tables(pos)
    if kbuf is None:
        sgr3, ma, k_last, v_last = _nat_attn(x, wts["g_norm"], wts["w_nat"], rope, wts["sink_t"],
                                             n_blk=att_tq // WINDOW)
        sgr = sgr3.reshape(n_tok, RWKV_WIDTH)
        k_new = k_last.reshape(B, WINDOW, ATTN_KV_HEADS, HEAD_DIM)
        v_new = v_last.reshape(B, WINDOW, ATTN_KV_HEADS, HEAD_DIM)
    else:
        sgr, q, k, v, sga = _nat_proj(x2d, wts["g_norm"], wts["w_nat"], jnp.tile(rope, (1, tm // T, 1)), tm=tm)
        k3 = k.reshape(B, T, KV_WIDTH)
        v3 = v.reshape(B, T, KV_WIDTH)
        kb = kbuf.reshape(B, WINDOW, KV_WIDTH)
        vb = vbuf.reshape(B, WINDOW, KV_WIDTH)
        ma, k_cache, v_cache = _swa_attn(q.reshape(B, T, ATTN_WIDTH), kb, k3, vb, v3,
                                         sga.reshape(B, T, ATTN_WIDTH), wts["sink_t"], bb=att_bb, tq=att_tq)
        k_new = k_cache.reshape(B, WINDOW, ATTN_KV_HEADS, HEAD_DIM)
        v_new = v_cache.reshape(B, WINDOW, ATTN_KV_HEADS, HEAD_DIM)

    y = _out_mix(x2d, o_r, sgr, ma.reshape(n_tok, ATTN_WIDTH), p.reshape(n_tok, PLE_DIM),
                 wts["w_out"], wts["g_ple"], wts["w_pg"], wts["w_pp"], wts["g_final"], tm=tm)
    return y.reshape(B, T, D_MODEL), s_new, shift_new, k_new, v_new


def kernel(x_prompt, x_sample, state_rwkv_wkv, state_rwkv_shift, cache_swa_k, cache_swa_v,
           p_prompt, p_sample, g_norm, w_in, mu_shift, w0, w2, a0, a2, k_k, k_a, r_k,
           ln_w, ln_b, sinks, w_out, g_ple, w_ple_gate, w_ple_proj, g_final):
    assert w_in.shape[0] == 1, "single layer"
    w_in0 = w_in[0]
    head_order = [g * ATTN_GROUP + i for i in range(ATTN_GROUP) for g in range(ATTN_KV_HEADS)]
    cols = jnp.concatenate([jnp.arange(h * HEAD_DIM, (h + 1) * HEAD_DIM) for h in head_order])
    o_q = SHIFT_DIM + RWKV_WIDTH
    o_ga = o_q + ATTN_WIDTH + 2 * KV_WIDTH
    nat_cols = jnp.concatenate([jnp.arange(SHIFT_DIM, o_q), o_q + cols, jnp.arange(o_q + ATTN_WIDTH, o_ga),
                                o_ga + cols])
    out_rows = jnp.concatenate([jnp.arange(RWKV_WIDTH), RWKV_WIDTH + cols])
    mu = mu_shift[0]
    wts = {
        "g_norm": g_norm[0][None, :],
        "wt_rwkv": w_in0[:, :SHIFT_DIM].T.astype(bf16),
        "w_nat": w_in0[:, nat_cols].astype(bf16),
        "mul_t": _col_tile(mu[None, 3 * RWKV_WIDTH:])[0],
        "c2_t": _col_tile(jnp.stack([w0[0], a0[0]])),
        "w2t": w2[0].T.astype(bf16), "a2t": a2[0].T.astype(bf16),
        "rwkv_params": jnp.stack([mu[:RWKV_WIDTH], mu[RWKV_WIDTH:2 * RWKV_WIDTH],
                                  mu[2 * RWKV_WIDTH:3 * RWKV_WIDTH], k_k[0], k_a[0], r_k[0], ln_w[0], ln_b[0]]),
        "sink_t": _col_tile(sinks)[0],
        "w_out": w_out[0][out_rows].astype(bf16), "g_ple": g_ple[0][None, :],
        "w_pg": w_ple_gate[0].astype(bf16), "w_pp": w_ple_proj[0].astype(bf16),
        "g_final": g_final[None, :],
    }
    Bp, Tp, _ = x_prompt.shape
    Bs, Ts, _ = x_sample.shape
    assert Bp * RWKV_HEADS == LANES and Tp % PROMPT_TILES["att_tq"] == 0 and Tp % PROMPT_TILES["tc"] == 0
    assert Bs == LANES and Ts == SAMPLE_TILES["tc"] and Bs * Ts == SAMPLE_TILES["tm"]

    yp, s1, sh1, k1, v1 = _layer(x_prompt, p_prompt[0], jnp.arange(Tp), None, None, None, None, wts,
                                 heads_on_lanes=True, **PROMPT_TILES)
    ys, s2, sh2, k2, v2 = _layer(x_sample, p_sample[0], PAST_LEN + jnp.arange(Ts),
                                 state_rwkv_wkv[0], state_rwkv_shift[0], cache_swa_k[0], cache_swa_v[0],
                                 wts, heads_on_lanes=False, **SAMPLE_TILES)
    return (yp, ys, s1[None], sh1[None], k1[None], v1[None], s2[None], sh2[None], k2[None], v2[None])
```
